```python
import math
import jax, jax.numpy as jnp
from jax import lax
import numpy as np

D_MODEL = 1024
BATCH = 2
SEQ = 16384
DEPTH = 2
DEC_BATCH = 16
DEC_SEQ = 2048
PAST_LEN = 128

HEAD_DIM = 64
BLOCK = 128
A_HEADS = 6
A_KV = 2
WINDOW = 128
B_HEADS = 4
B_QK_DIM = 32
C_HEADS = 6
C_KV = 2
ROPE_THETA = 10000.0
GRID_W = 64
MIX_WIDTH = (A_HEADS + B_HEADS + C_HEADS) * HEAD_DIM
A_COLS = (A_HEADS + 2 * A_KV) * HEAD_DIM
B_COLS = 3 * B_HEADS * HEAD_DIM
C_COLS = (C_HEADS + 2 * C_KV) * HEAD_DIM
IN_COLS = A_COLS + B_COLS + C_COLS
NUM_BUCKETS = 32
MAX_DISTANCE = 128
N_EXPERTS = 16
CAPACITY_FACTOR = 2
EXPERT_FF = 1024
N_MOD = 6
EPS = 1e-6
NEG_INF = -1e30

kernel_name = 'hymba_style_hybrid_encoder_ec_moe'


def rms_norm(x, gain):
    x32 = x.astype(jnp.float32)
    y = x32 * lax.rsqrt(jnp.mean(x32 * x32, axis=-1, keepdims=True) + EPS)
    return (y * gain.astype(jnp.float32)).astype(x.dtype)


def t5_bucket(rel):
    half = NUM_BUCKETS // 2
    max_exact = half // 2
    n = jnp.abs(rel)
    nf = jnp.maximum(n, 1).astype(jnp.float32)
    large = max_exact + (jnp.log(nf / max_exact) / math.log(MAX_DISTANCE / max_exact)
                         * (half - max_exact)).astype(jnp.int32)
    large = jnp.minimum(large, half - 1)
    return jnp.where(rel > 0, half, 0) + jnp.where(n < max_exact, n, large)


def window_attention(q, k, v, sink, bias_table):
    bsz, s_len = q.shape[:2]
    nb = s_len // BLOCK
    g = A_HEADS // A_KV
    qb = q.reshape(bsz, nb, BLOCK, A_KV, g, HEAD_DIM)

    def band(t):
        tp = jnp.pad(t, ((0, 0), (BLOCK, BLOCK), (0, 0), (0, 0))).reshape(bsz, nb + 2, BLOCK, A_KV, HEAD_DIM)
        return jnp.concatenate([tp[:, :-2], tp[:, 1:-1], tp[:, 2:]], axis=2)

    kb, vb = band(k), band(v)
    s = jnp.einsum('bnqkgd,bnskd->bnkgqs', qb, kb).astype(jnp.float32) * (HEAD_DIM ** -0.5)
    qi = jnp.arange(BLOCK)
    kj = jnp.arange(3 * BLOCK)
    rel = kj[None, :] - BLOCK - qi[:, None]
    bias = jnp.transpose(bias_table[t5_bucket(rel)], (2, 0, 1))
    bias = bias.reshape(A_KV, g, BLOCK, 3 * BLOCK).astype(jnp.float32)
    kpos = jnp.arange(nb)[:, None] * BLOCK - BLOCK + kj[None, :]
    mask = (jnp.abs(rel) <= WINDOW)[None] & ((kpos >= 0) & (kpos < s_len))[:, None, :]
    s = jnp.where(mask[None, :, None, None], s + bias, NEG_INF)
    sk = sink.astype(jnp.float32).reshape(A_KV, g, 1, 1)
    m = jnp.maximum(jnp.max(s, axis=-1, keepdims=True), sk)
    p = jnp.exp(s - m)
    denom = jnp.sum(p, axis=-1, keepdims=True) + jnp.exp(sk - m)
    p = (p / denom).astype(v.dtype)
    o = jnp.einsum('bnkgqs,bnskd->bnqkgd', p, vb)
    return o.reshape(bsz, s_len, A_HEADS, HEAD_DIM)


def diff_attention(q, k, v, lam, lam_init, subln_gain, bias_table):
    bsz, s_len = q.shape[:2]
    nb = s_len // BLOCK
    qblocks = jnp.moveaxis(q.reshape(bsz, nb, BLOCK, B_HEADS, 2, B_QK_DIM), 1, 0)
    kpos = jnp.arange(s_len)

    def one_block(args):
        n, qb = args
        s = jnp.einsum('bqhmd,bshmd->bhmqs', qb, k).astype(jnp.float32) * (B_QK_DIM ** -0.5)
        rel = kpos[None, :] - (n * BLOCK + jnp.arange(BLOCK))[:, None]
        bias = jnp.transpose(bias_table[t5_bucket(rel)], (2, 0, 1)).astype(jnp.float32)
        p = jax.nn.softmax(s + bias[None, :, None], axis=-1)
        w = (p[:, :, 0] - lam * p[:, :, 1]).astype(v.dtype)
        return jnp.einsum('bhqs,bshd->bqhd', w, v)

    o = lax.map(one_block, (jnp.arange(nb), qblocks))
    o = jnp.moveaxis(o, 0, 1).reshape(bsz, s_len, B_HEADS, HEAD_DIM)
    return rms_norm(o, subln_gain) * (1.0 - lam_init)


def axial_rope(x, rows, cols):
    half = HEAD_DIM // 2
    quarter = half // 2
    freqs = ROPE_THETA ** (-jnp.arange(quarter, dtype=jnp.float32) / quarter)

    def rot(sec, pos):
        ang = pos.astype(jnp.float32)[:, None] * freqs[None]
        cos = jnp.cos(ang)[None, :, None].astype(x.dtype)
        sin = jnp.sin(ang)[None, :, None].astype(x.dtype)
        a, b = sec[..., :quarter], sec[..., quarter:]
        return jnp.concatenate([a * cos - b * sin, b * cos + a * sin], axis=-1)

    return jnp.concatenate([rot(x[..., :half], rows), rot(x[..., half:], cols)], axis=-1)


def grid_attention(q, k, v):
    bsz, s_len = q.shape[:2]
    nb = s_len // BLOCK
    g = C_HEADS // C_KV
    qblocks = jnp.moveaxis(q.reshape(bsz, nb, BLOCK, C_KV, g, HEAD_DIM), 1, 0)

    def one_block(qb):
        s = jnp.einsum('bqkgd,bskd->bkgqs', qb, k).astype(jnp.float32) * (HEAD_DIM ** -0.5)
        p = jax.nn.softmax(s, axis=-1).astype(v.dtype)
        return jnp.einsum('bkgqs,bskd->bqkgd', p, v)

    o = lax.map(one_block, qblocks)
    return jnp.moveaxis(o, 0, 1).reshape(bsz, s_len, C_HEADS, HEAD_DIM)


def token_mixers(h, layer, rows, cols, w_in, w_out, qk_gain_a, qk_gain_b, qk_gain_c,
                 sink_a, lam_b, subln_b, rel_bias):
    bsz, s_len, _ = h.shape
    proj = h @ w_in
    pa, pb, pc = jnp.split(proj, [A_COLS, A_COLS + B_COLS], axis=-1)
    qa, ka, va = jnp.split(pa, [A_HEADS * HEAD_DIM, (A_HEADS + A_KV) * HEAD_DIM], axis=-1)
    qa = rms_norm(qa.reshape(bsz, s_len, A_HEADS, HEAD_DIM), qk_gain_a[0])
    ka = rms_norm(ka.reshape(bsz, s_len, A_KV, HEAD_DIM), qk_gain_a[1])
    va = va.reshape(bsz, s_len, A_KV, HEAD_DIM)
    oa = window_attention(qa, ka, va, sink_a, rel_bias[:, :A_HEADS])
    qb, kb, vb = jnp.split(pb, 3, axis=-1)
    qb = rms_norm(qb.reshape(bsz, s_len, B_HEADS, 2, B_QK_DIM), qk_gain_b[0])
    kb = rms_norm(kb.reshape(bsz, s_len, B_HEADS, 2, B_QK_DIM), qk_gain_b[1])
    vb = vb.reshape(bsz, s_len, B_HEADS, HEAD_DIM)
    lam_init = 0.8 - 0.6 * math.exp(-0.3 * layer)
    lp = lam_b.astype(jnp.float32)
    lam = jnp.exp(jnp.sum(lp[0] * lp[1])) - jnp.exp(jnp.sum(lp[2] * lp[3])) + lam_init
    ob = diff_attention(qb, kb, vb, lam, lam_init, subln_b, rel_bias[:, A_HEADS:])
    qc, kc, vc = jnp.split(pc, [C_HEADS * HEAD_DIM, (C_HEADS + C_KV) * HEAD_DIM], axis=-1)
    qc = axial_rope(rms_norm(qc.reshape(bsz, s_len, C_HEADS, HEAD_DIM), qk_gain_c[0]), rows, cols)
    kc = axial_rope(rms_norm(kc.reshape(bsz, s_len, C_KV, HEAD_DIM), qk_gain_c[1]), rows, cols)
    vc = vc.reshape(bsz, s_len, C_KV, HEAD_DIM)
    oc = grid_attention(qc, kc, vc)
    o = jnp.concatenate([oa.reshape(bsz, s_len, -1), ob.reshape(bsz, s_len, -1),
                         oc.reshape(bsz, s_len, -1)], axis=-1)
    return o @ w_out


def expert_choice_ffn(h, w_router, w_gate, w_up, w_down):
    bsz, s_len, d = h.shape
    n_tok = bsz * s_len
    cap = CAPACITY_FACTOR * n_tok // N_EXPERTS
    flat = h.reshape(n_tok, d)
    aff = jax.nn.softmax((flat @ w_router).astype(jnp.float32), axis=-1)
    gate, idx = lax.top_k(aff.T, cap)
    xs = flat[idx]
    hid = jax.nn.silu(jnp.einsum('ecd,edf->ecf', xs, w_gate)) * jnp.einsum('ecd,edf->ecf', xs, w_up)
    y = jnp.einsum('ecf,efd->ecd', hid, w_down) * gate[..., None].astype(h.dtype)
    out = jnp.zeros_like(flat).at[idx.reshape(-1)].add(y.reshape(-1, d))
    return out.reshape(bsz, s_len, d)


def encoder_trunk(x, c, w_mod, b_mod, g_attn, g_ffn, w_in, w_out, qk_gain_a, qk_gain_b, qk_gain_c,
                  sink_a, lam_b, subln_b, rel_bias, w_router, w_gate, w_up, w_down):
    s_len = x.shape[1]
    n_rows = s_len // GRID_W
    rows = jnp.repeat(jnp.arange(n_rows), GRID_W)
    cols = jnp.tile(jnp.arange(GRID_W), n_rows)
    for l in range(DEPTH):
        mod = (jax.nn.silu(c) @ w_mod[l] + b_mod[l])[:, None, :]
        sh_a, sc_a, gt_a, sh_f, sc_f, gt_f = jnp.split(mod, N_MOD, axis=-1)
        h = rms_norm(x, g_attn[l]) * (1 + sc_a) + sh_a
        x = x + gt_a * token_mixers(h, l, rows, cols, w_in[l], w_out[l], qk_gain_a[l], qk_gain_b[l],
                                    qk_gain_c[l], sink_a[l], lam_b[l], subln_b[l], rel_bias)
        h = rms_norm(x, g_ffn[l]) * (1 + sc_f) + sh_f
        x = x + gt_f * expert_choice_ffn(h, w_router[l], w_gate[l], w_up[l], w_down[l])
    return x


def setup_inputs(seed: int = 0) -> dict:
    key = jax.random.key(seed)
    ks = jax.random.split(key, 24)
    D = D_MODEL

    def nrm(k, shape, scale):
        return jax.random.normal(k, shape, jnp.float32) * scale

    return {
        'x_prompt': nrm(ks[0], (BATCH, SEQ, D), 1.0),
        'x_sample': nrm(ks[1], (DEC_BATCH, DEC_SEQ, D), 1.0),
        'c_prompt': nrm(ks[2], (BATCH, D), 1.0),
        'c_sample': nrm(ks[3], (DEC_BATCH, D), 1.0),
        'w_mod': nrm(ks[4], (DEPTH, D, N_MOD * D), 0.5 * D ** -0.5),
        'b_mod': nrm(ks[5], (DEPTH, N_MOD * D), 0.02),
        'g_attn': 1.0 + nrm(ks[6], (DEPTH, D), 0.1),
        'g_ffn': 1.0 + nrm(ks[7], (DEPTH, D), 0.1),
        'w_in': nrm(ks[8], (DEPTH, D, IN_COLS), D ** -0.5),
        'w_out': nrm(ks[9], (DEPTH, MIX_WIDTH, D), MIX_WIDTH ** -0.5),
        'qk_gain_a': 1.0 + nrm(ks[10], (DEPTH, 2, HEAD_DIM), 0.1),
        'qk_gain_b': 1.0 + nrm(ks[11], (DEPTH, 2, B_QK_DIM), 0.1),
        'qk_gain_c': 1.0 + nrm(ks[12], (DEPTH, 2, HEAD_DIM), 0.1),
        'sink_a': nrm(ks[13], (DEPTH, A_HEADS), 0.5),
        'lam_b': nrm(ks[14], (DEPTH, 4, B_QK_DIM), 0.1),
        'subln_b': 1.0 + nrm(ks[15], (DEPTH, HEAD_DIM), 0.1),
        'rel_bias': nrm(ks[16], (NUM_BUCKETS, A_HEADS + B_HEADS), 0.5),
        'w_router': nrm(ks[17], (DEPTH, D, N_EXPERTS), D ** -0.5),
        'w_gate': nrm(ks[18], (DEPTH, N_EXPERTS, D, EXPERT_FF), D ** -0.5),
        'w_up': nrm(ks[19], (DEPTH, N_EXPERTS, D, EXPERT_FF), D ** -0.5),
        'w_down': nrm(ks[20], (DEPTH, N_EXPERTS, EXPERT_FF, D), EXPERT_FF ** -0.5),
    }


def reference(x_prompt, x_sample, c_prompt, c_sample, w_mod, b_mod, g_attn, g_ffn, w_in, w_out,
              qk_gain_a, qk_gain_b, qk_gain_c, sink_a, lam_b, subln_b, rel_bias,
              w_router, w_gate, w_up, w_down):
    y_prompt = encoder_trunk(x_prompt, c_prompt, w_mod, b_mod, g_attn, g_ffn, w_in, w_out,
                             qk_gain_a, qk_gain_b, qk_gain_c, sink_a, lam_b, subln_b, rel_bias,
                             w_router, w_gate, w_up, w_down)
    y_sample = encoder_trunk(x_sample, c_sample, w_mod, b_mod, g_attn, g_ffn, w_in, w_out,
                             qk_gain_a, qk_gain_b, qk_gain_c, sink_a, lam_b, subln_b, rel_bias,
                             w_router, w_gate, w_up, w_down)
    return (y_prompt, y_sample)
```

```python
import functools
import math

import jax
import jax.numpy as jnp
from jax import lax
from jax.experimental import pallas as pl
from jax.experimental.pallas import tpu as pltpu

F32 = jnp.float32
BF16 = jnp.bfloat16

HEAD_DIM = 64
A_HEADS, A_KV = 6, 2
B_HEADS, B_QK = 4, 32
C_HEADS, C_KV = 6, 2
WINDOW = 128
A_BLK = 128
NUM_BUCKETS, MAX_DISTANCE = 32, 128
ROPE_THETA = 10000.0
GRID_W = 64
N_EXPERTS = 16
CAPACITY_FACTOR = 2
N_MOD = 6
EPS = 1e-6
NEG_BIG = -1e30
LOG2E = math.log2(math.e)

A_Q, A_K, A_V = A_HEADS * HEAD_DIM, A_KV * HEAD_DIM, A_KV * HEAD_DIM
B_Q, B_K, B_V = B_HEADS * HEAD_DIM, B_HEADS * HEAD_DIM, B_HEADS * HEAD_DIM
C_Q, C_K, C_V = C_HEADS * HEAD_DIM, C_KV * HEAD_DIM, C_KV * HEAD_DIM
A_OFF = 0
B_OFF = A_Q + A_K + A_V
C_OFF = B_OFF + B_Q + B_K + B_V
IN_COLS = C_OFF + C_Q + C_K + C_V

LANES = 128
VMEM_LIMIT_BYTES = 56 * 1024 * 1024

TM_PROJ = 512
TQ_C, TK_C = 256, 512
T_B = 256
MOE_CH = 256
ROUTE_PC = 512


def _cparams(sem):
    return pltpu.CompilerParams(dimension_semantics=sem, vmem_limit_bytes=VMEM_LIMIT_BYTES)


def _dot(a, b):
    return jnp.dot(a, b, preferred_element_type=F32)


def _dot_nt(a, b):
    return lax.dot_general(a, b, (((1,), (1,)), ((), ())), preferred_element_type=F32)


def _mod_kernel(c_ref, w_ref, b_ref, o_ref):
    c = c_ref[...]
    a = c * (1.0 / (1.0 + jnp.exp(-c)))
    a_hi = a.astype(BF16)
    a_lo = (a - a_hi.astype(F32)).astype(BF16)
    w = w_ref[0]
    w_hi = w.astype(BF16)
    w_lo = (w - w_hi.astype(F32)).astype(BF16)
    acc = _dot(a_hi, w_hi) + _dot(a_lo, w_hi) + _dot(a_hi, w_lo)
    o_ref[0] = acc + b_ref[0]


def _modulation(c_all, w_mod, b_mod):
    depth, d, nm = w_mod.shape
    rows = c_all.shape[0]
    bn = 1024
    return pl.pallas_call(
        _mod_kernel,
        grid=(depth, nm // bn),
        in_specs=[
            pl.BlockSpec((rows, d), lambda l, j: (0, 0)),
            pl.BlockSpec((1, d, bn), lambda l, j: (l, 0, j)),
            pl.BlockSpec((1, 1, bn), lambda l, j: (l, 0, j)),
        ],
        out_specs=pl.BlockSpec((1, rows, bn), lambda l, j: (l, 0, j)),
        out_shape=jax.ShapeDtypeStruct((depth, rows, nm), F32),
        compiler_params=_cparams(("arbitrary", "arbitrary")),
        name="modulation",
    )(c_all, w_mod, b_mod.reshape(depth, 1, nm))


def _segnorm(rows, seglen):
    r, t = rows.shape
    r3 = rows.reshape(r // seglen, seglen, t)
    ms = jnp.mean(r3 * r3, axis=1, keepdims=True)
    return (r3 * lax.rsqrt(ms + EPS)).reshape(r, t)


def _rope(rows, cs):
    r, t = rows.shape
    x = rows.reshape(r // HEAD_DIM, HEAD_DIM, t)
    cr, sr, cc, sc = cs[None, 0:16], cs[None, 16:32], cs[None, 32:48], cs[None, 48:64]
    ar, br, ac, bc = x[:, 0:16], x[:, 16:32], x[:, 32:48], x[:, 48:64]
    out = jnp.concatenate(
        [ar * cr - br * sr, br * cr + ar * sr, ac * cc - bc * sc, bc * cc + ac * sc], axis=1)
    return out.reshape(r, t)


def _emit_k(k_ref, kt):
    nh = kt.shape[0] // HEAD_DIM
    for p in range(kt.shape[0] // LANES):
        tok = kt[p * LANES:(p + 1) * LANES].T
        k_ref[0, 2 * p] = tok[:, 0:HEAD_DIM].astype(k_ref.dtype)
        k_ref[0, 2 * p + 1] = tok[:, HEAD_DIM:LANES].astype(k_ref.dtype)
    del nh


def _emit_vt(v_ref, vt):
    nh = vt.shape[0] // HEAD_DIM
    tk = v_ref.shape[-1]
    for h in range(nh):
        for c in range(vt.shape[1] // tk):
            v_ref[0, h, c] = vt[h * HEAD_DIM:(h + 1) * HEAD_DIM, c * tk:(c + 1) * tk].astype(v_ref.dtype)


def _inproj_kernel(x_ref, mod_ref, g_ref, w_ref, gcol_ref, cs_ref,
                   qa_ref, ka_ref, va_ref, qb_ref, kb_ref, vb_ref, qc_ref, kc_ref, vc_ref):
    x = x_ref[...]
    mod = mod_ref[0]
    d = x.shape[1]
    sh, sc = mod[:, 0:d], mod[:, d:2 * d]
    ms = jnp.mean(x * x, axis=-1, keepdims=True)
    h = x * lax.rsqrt(ms + EPS) * g_ref[...]
    h = (h * (1.0 + sc) + sh).astype(BF16)
    pt = _dot_nt(w_ref[...], h)
    gcol = gcol_ref[...]
    cs = cs_ref[...]

    def rows(off, n):
        return pt[off:off + n], gcol[off:off + n]

    q, g = rows(A_OFF, A_Q)
    qa_ref[0] = (_segnorm(q, HEAD_DIM) * g).astype(qa_ref.dtype)
    k, g = rows(A_OFF + A_Q, A_K)
    _emit_k(ka_ref, _segnorm(k, HEAD_DIM) * g)
    _emit_vt(va_ref, pt[A_OFF + A_Q + A_K:A_OFF + A_Q + A_K + A_V])
    q, g = rows(B_OFF, B_Q)
    qb_ref[0] = (_segnorm(q, B_QK) * g).astype(qb_ref.dtype)
    k, g = rows(B_OFF + B_Q, B_K)
    _emit_k(kb_ref, _segnorm(k, B_QK) * g)
    _emit_vt(vb_ref, pt[B_OFF + B_Q + B_K:B_OFF + B_Q + B_K + B_V])
    q, g = rows(C_OFF, C_Q)
    qc_ref[0] = _rope(_segnorm(q, HEAD_DIM) * g, cs).astype(qc_ref.dtype)
    k, g = rows(C_OFF + C_Q, C_K)
    _emit_k(kc_ref, _rope(_segnorm(k, HEAD_DIM) * g, cs))
    _emit_vt(vc_ref, pt[C_OFF + C_Q + C_K:C_OFF + C_Q + C_K + C_V])


def _inproj(x2d, mod3, g_attn, w_in_t, gcol, cs, bsz, s_len, tk_b, tk_c):
    n, d = x2d.shape
    tm = min(TM_PROJ, s_len)
    spb = s_len // tm

    def qt_spec(rows_):
        return pl.BlockSpec((1, rows_, tm), lambda i: (i // spb, 0, i % spb))

    def k_spec(nh):
        return pl.BlockSpec((1, nh, tm, HEAD_DIM), lambda i: (i // spb, 0, i % spb, 0))

    def v_spec(nh, tk):
        return pl.BlockSpec((1, nh, tm // tk, HEAD_DIM, tk), lambda i: (i // spb, 0, i % spb, 0, 0))

    def qt_shape(rows_):
        return jax.ShapeDtypeStruct((bsz, rows_, s_len), BF16)

    def k_shape(nh):
        return jax.ShapeDtypeStruct((bsz, nh, s_len, HEAD_DIM), BF16)

    def v_shape(nh, tk):
        return jax.ShapeDtypeStruct((bsz, nh, s_len // tk, HEAD_DIM, tk), BF16)

    return pl.pallas_call(
        _inproj_kernel,
        grid=(n // tm,),
        in_specs=[
            pl.BlockSpec((tm, d), lambda i: (i, 0)),
            pl.BlockSpec((1, 1, N_MOD * d), lambda i: (i // spb, 0, 0)),
            pl.BlockSpec((1, d), lambda i: (0, 0)),
            pl.BlockSpec((IN_COLS, d), lambda i: (0, 0)),
            pl.BlockSpec((IN_COLS, 1), lambda i: (0, 0)),
            pl.BlockSpec((HEAD_DIM, tm), lambda i: (0, i % spb)),
        ],
        out_specs=[
            qt_spec(A_Q), k_spec(A_KV), v_spec(A_KV, A_BLK),
            qt_spec(B_Q), k_spec(B_HEADS), v_spec(B_HEADS, tk_b),
            qt_spec(C_Q), k_spec(C_KV), v_spec(C_KV, tk_c),
        ],
        out_shape=[
            qt_shape(A_Q), k_shape(A_KV), v_shape(A_KV, A_BLK),
            qt_shape(B_Q), k_shape(B_HEADS), v_shape(B_HEADS, tk_b),
            qt_shape(C_Q), k_shape(C_KV), v_shape(C_KV, tk_c),
        ],
        compiler_params=_cparams(("arbitrary",)),
        name="inproj",
    )(x2d, mod3, g_attn, w_in_t, gcol, cs)


def _softmax_step(s, vt, m_ref, l_ref, acc_ref):
    m_old = m_ref[...]
    m_new = jnp.maximum(m_old, jnp.max(s, axis=0, keepdims=True))
    alpha = jnp.exp2(m_old - m_new)
    p = jnp.exp2(s - m_new)
    l_ref[...] = alpha * l_ref[...] + jnp.sum(p, axis=0, keepdims=True)
    acc_ref[...] = alpha * acc_ref[...] + _dot(vt, p.astype(BF16))
    m_ref[...] = m_new


def _attn_c_kernel(q_ref, k_ref, v_ref, o_ref, m_ref, l_ref, acc_ref):
    g = C_HEADS // C_KV
    tq = q_ref.shape[2]
    nk, _, tk = v_ref.shape[2:]
    q3 = q_ref[0]
    qcat = jnp.concatenate([q3[h * HEAD_DIM:(h + 1) * HEAD_DIM] for h in range(g)], axis=1)
    m_ref[...] = jnp.full(m_ref.shape, NEG_BIG, F32)
    l_ref[...] = jnp.zeros(l_ref.shape, F32)
    acc_ref[...] = jnp.zeros(acc_ref.shape, F32)

    def body(j, carry):
        kc = k_ref[0, 0, pl.ds(pl.multiple_of(j * tk, tk), tk), :]
        s = _dot(kc, qcat)
        _softmax_step(s, v_ref[0, 0, j], m_ref, l_ref, acc_ref)
        return carry

    lax.fori_loop(0, nk, body, 0)
    o = acc_ref[...] / l_ref[...]
    o_ref[0] = jnp.concatenate([o[:, h * tq:(h + 1) * tq] for h in range(g)], axis=0).astype(o_ref.dtype)


def _attn_c(qt, k, vt):
    bsz, _, s_len = qt.shape
    g = C_HEADS // C_KV
    tq = min(TQ_C, s_len)
    nk, _, tk = vt.shape[2:]
    return pl.pallas_call(
        _attn_c_kernel,
        grid=(bsz, C_KV, s_len // tq),
        in_specs=[
            pl.BlockSpec((1, g * HEAD_DIM, tq), lambda b, kv, i: (b, kv, i)),
            pl.BlockSpec((1, 1, s_len, HEAD_DIM), lambda b, kv, i: (b, kv, 0, 0)),
            pl.BlockSpec((1, 1, nk, HEAD_DIM, tk), lambda b, kv, i: (b, kv, 0, 0, 0)),
        ],
        out_specs=pl.BlockSpec((1, g * HEAD_DIM, tq), lambda b, kv, i: (b, kv, i)),
        out_shape=jax.ShapeDtypeStruct((bsz, C_Q, s_len), BF16),
        scratch_shapes=[
            pltpu.VMEM((1, g * tq), F32), pltpu.VMEM((1, g * tq), F32),
            pltpu.VMEM((HEAD_DIM, g * tq), F32),
        ],
        compiler_params=_cparams(("arbitrary", "arbitrary", "arbitrary")),
        name="attn_c",
    )(qt, k, vt)


def _attn_b_kernel(far_ref, lam_ref, q_ref, k_ref, v_ref, bias_ref, gcol_ref, o_ref,
                   m_ref, l_ref, acc_ref):
    h = pl.program_id(1)
    i = pl.program_id(2)
    t = q_ref.shape[2]
    nk = v_ref.shape[2]
    q = q_ref[0]
    z = jnp.zeros((B_QK, t), q.dtype)
    qcat = jnp.concatenate(
        [jnp.concatenate([q[0:B_QK], z], axis=0), jnp.concatenate([z, q[B_QK:HEAD_DIM]], axis=0)], axis=1)
    m_ref[...] = jnp.full(m_ref.shape, NEG_BIG, F32)
    l_ref[...] = jnp.zeros(l_ref.shape, F32)
    acc_ref[...] = jnp.zeros(acc_ref.shape, F32)

    def step(j, bias):
        kc = k_ref[0, 0, pl.ds(pl.multiple_of(j * t, t), t), :]
        s = _dot(kc, qcat) + bias
        _softmax_step(s, v_ref[0, 0, j], m_ref, l_ref, acc_ref)

    def far_neg(j, carry):
        step(j, far_ref[h, 0])
        return carry

    def far_pos(j, carry):
        step(j, far_ref[h, 1])
        return carry

    lax.fori_loop(0, jnp.maximum(i - 1, 0), far_neg, 0)
    for dd in range(3):
        j = i + (dd - 1)

        @pl.when((j >= 0) & (j < nk))
        def _():
            b = bias_ref[0, dd]
            step(j, jnp.concatenate([b, b], axis=1))

    lax.fori_loop(jnp.minimum(i + 2, nk), nk, far_pos, 0)

    o = acc_ref[...] / l_ref[...]
    o = o[:, 0:t] - lam_ref[0] * o[:, t:2 * t]
    ms = jnp.mean(o * o, axis=0, keepdims=True)
    o_ref[0] = (o * lax.rsqrt(ms + EPS) * gcol_ref[...]).astype(o_ref.dtype)


def _attn_b(qt, k, vt, bias_tiles, far, lam, gcol):
    bsz, _, s_len = qt.shape
    nk, _, t = vt.shape[2:]
    return pl.pallas_call(
        _attn_b_kernel,
        grid_spec=pltpu.PrefetchScalarGridSpec(
            num_scalar_prefetch=2,
            grid=(bsz, B_HEADS, s_len // t),
            in_specs=[
                pl.BlockSpec((1, HEAD_DIM, t), lambda b, h, i, *_: (b, h, i)),
                pl.BlockSpec((1, 1, s_len, HEAD_DIM), lambda b, h, i, *_: (b, h, 0, 0)),
                pl.BlockSpec((1, 1, nk, HEAD_DIM, t), lambda b, h, i, *_: (b, h, 0, 0, 0)),
                pl.BlockSpec((1, 3, t, t), lambda b, h, i, *_: (h, 0, 0, 0)),
                pl.BlockSpec((HEAD_DIM, 1), lambda b, h, i, *_: (0, 0)),
            ],
            out_specs=pl.BlockSpec((1, HEAD_DIM, t), lambda b, h, i, *_: (b, h, i)),
            scratch_shapes=[
                pltpu.VMEM((1, 2 * t), F32), pltpu.VMEM((1, 2 * t), F32),
                pltpu.VMEM((HEAD_DIM, 2 * t), F32),
            ],
        ),
        out_shape=jax.ShapeDtypeStruct((bsz, B_Q, s_len), BF16),
        compiler_params=_cparams(("arbitrary", "arbitrary", "arbitrary")),
        name="attn_b",
    )(far, lam, qt, k, vt, bias_tiles, gcol)


def _attn_a_kernel(sink_ref, q_ref, kp_ref, kc_ref, kn_ref, vp_ref, vc_ref, vn_ref, bias_ref, o_ref):
    n = pl.program_id(1)
    nb = pl.num_programs(1)
    g = A_HEADS // A_KV
    blk = A_BLK
    row = lax.broadcasted_iota(jnp.int32, (3 * blk, 1), 0)
    valid = ((row >= blk) | (n > 0)) & ((row < 2 * blk) | (n < nb - 1))
    for kv in range(A_KV):
        q3 = q_ref[0, kv * g * HEAD_DIM:(kv + 1) * g * HEAD_DIM]
        qcat = jnp.concatenate([q3[h * HEAD_DIM:(h + 1) * HEAD_DIM] for h in range(g)], axis=1)
        kwin = jnp.concatenate([kp_ref[0, kv], kc_ref[0, kv], kn_ref[0, kv]], axis=0)
        vwin = jnp.concatenate([vp_ref[0, kv, 0], vc_ref[0, kv, 0], vn_ref[0, kv, 0]], axis=1)
        bias = jnp.concatenate([bias_ref[kv * g + h] for h in range(g)], axis=1)
        s = jnp.where(valid, _dot(kwin, qcat) + bias, NEG_BIG)
        sink = jnp.concatenate(
            [jnp.full((1, blk), sink_ref[kv * g + h], F32) for h in range(g)], axis=1)
        m = jnp.maximum(jnp.max(s, axis=0, keepdims=True), sink)
        p = jnp.exp2(s - m)
        denom = jnp.sum(p, axis=0, keepdims=True) + jnp.exp2(sink - m)
        o = _dot(vwin, p.astype(BF16)) / denom
        o_ref[0, kv * g * HEAD_DIM:(kv + 1) * g * HEAD_DIM] = jnp.concatenate(
            [o[:, h * blk:(h + 1) * blk] for h in range(g)], axis=0).astype(o_ref.dtype)


def _attn_a(qt, k, vt, bias, sink):
    bsz, _, s_len = qt.shape
    nb = s_len // A_BLK

    def kspec(off):
        return pl.BlockSpec((1, A_KV, A_BLK, HEAD_DIM),
                            lambda b, n, *_: (b, 0, jnp.clip(n + off, 0, nb - 1), 0))

    def vspec(off):
        return pl.BlockSpec((1, A_KV, 1, HEAD_DIM, A_BLK),
                            lambda b, n, *_: (b, 0, jnp.clip(n + off, 0, nb - 1), 0, 0))

    return pl.pallas_call(
        _attn_a_kernel,
        grid_spec=pltpu.PrefetchScalarGridSpec(
            num_scalar_prefetch=1,
            grid=(bsz, nb),
            in_specs=[
                pl.BlockSpec((1, A_Q, A_BLK), lambda b, n, *_: (b, 0, n)),
                kspec(-1), kspec(0), kspec(1), vspec(-1), vspec(0), vspec(1),
                pl.BlockSpec((A_HEADS, 3 * A_BLK, A_BLK), lambda b, n, *_: (0, 0, 0)),
            ],
            out_specs=pl.BlockSpec((1, A_Q, A_BLK), lambda b, n, *_: (b, 0, n)),
        ),
        out_shape=jax.ShapeDtypeStruct((bsz, A_Q, s_len), BF16),
        compiler_params=_cparams(("arbitrary", "arbitrary")),
        name="attn_a",
    )(sink, qt, k, k, k, vt, vt, vt, bias)


def _outproj_kernel(oa_ref, ob_ref, oc_ref, wa_ref, wb_ref, wc_ref, x_ref, mod_ref, g_ref,
                    wrh_ref, wrl_ref, x1_ref, h2_ref, aff_ref):
    out_t = _dot(wa_ref[...], oa_ref[0]) + _dot(wb_ref[...], ob_ref[0]) + _dot(wc_ref[...], oc_ref[0])
    out = out_t.T
    mod = mod_ref[0]
    d = out.shape[1]
    gt_a = mod[:, 2 * d:3 * d]
    sh_f, sc_f = mod[:, 3 * d:4 * d], mod[:, 4 * d:5 * d]
    x1 = x_ref[...] + gt_a * out
    x1_ref[...] = x1
    ms = jnp.mean(x1 * x1, axis=-1, keepdims=True)
    h2 = x1 * lax.rsqrt(ms + EPS) * g_ref[...]
    h2 = h2 * (1.0 + sc_f) + sh_f
    h2_ref[...] = h2
    hi = h2.astype(BF16)
    lo = (h2 - hi.astype(F32)).astype(BF16)
    lt = _dot_nt(wrh_ref[...], hi) + _dot_nt(wrh_ref[...], lo) + _dot_nt(wrl_ref[...], hi)
    lt = lt - jnp.max(lt, axis=0, keepdims=True)
    e = jnp.exp(lt)
    aff_ref[...] = e / jnp.sum(e, axis=0, keepdims=True)


def _outproj(oa, ob, oc, wa, wb, wc, x2d, mod3, g_ffn, wr_hi, wr_lo, s_len):
    n, d = x2d.shape
    tm = min(TM_PROJ, s_len)
    spb = s_len // tm

    def ot_spec(rows_):
        return pl.BlockSpec((1, rows_, tm), lambda i: (i // spb, 0, i % spb))

    def full(shape):
        return pl.BlockSpec(shape, lambda i: tuple(0 for _ in shape))

    return pl.pallas_call(
        _outproj_kernel,
        grid=(n // tm,),
        in_specs=[
            ot_spec(A_Q), ot_spec(B_Q), ot_spec(C_Q),
            full((d, A_Q)), full((d, B_Q)), full((d, C_Q)),
            pl.BlockSpec((tm, d), lambda i: (i, 0)),
            pl.BlockSpec((1, 1, N_MOD * d), lambda i: (i // spb, 0, 0)),
            full((1, d)), full((N_EXPERTS, d)), full((N_EXPERTS, d)),
        ],
        out_specs=[
            pl.BlockSpec((tm, d), lambda i: (i, 0)),
            pl.BlockSpec((tm, d), lambda i: (i, 0)),
            pl.BlockSpec((N_EXPERTS, tm), lambda i: (0, i)),
        ],
        out_shape=[
            jax.ShapeDtypeStruct((n, d), F32),
            jax.ShapeDtypeStruct((n, d), F32),
            jax.ShapeDtypeStruct((N_EXPERTS, n), F32),
        ],
        compiler_params=_cparams(("arbitrary",)),
        name="outproj",
    )(oa, ob, oc, wa, wb, wc, x2d, mod3, g_ffn, wr_hi, wr_lo)


def _routing_kernel(aff_ref, idx_ref, gate_ref, thr_ref, need_ref, *, cap, pc):
    ne, nblk, _ = aff_ref.shape
    bits = lax.bitcast_convert_type(aff_ref[...], jnp.int32)

    def bis(it, t):
        cand = t | jnp.left_shift(jnp.int32(1), 30 - it)
        cnt = jnp.sum((bits >= cand).astype(F32), axis=(1, 2), keepdims=True)
        return jnp.where(cnt >= cap, cand, t)

    thr = lax.fori_loop(0, 31, bis, jnp.zeros((ne, 1, 1), jnp.int32))
    n_gt = jnp.sum((bits > thr).astype(F32), axis=(1, 2), keepdims=True)
    thr_ref[...] = jnp.broadcast_to(thr, thr_ref.shape)
    need_ref[...] = jnp.broadcast_to(cap - n_gt, need_ref.shape)

    def tri(nn, fn):
        a = lax.broadcasted_iota(jnp.int32, (nn, nn), 0)
        b = lax.broadcasted_iota(jnp.int32, (nn, nn), 1)
        return jnp.where(fn(a, b), 1.0, 0.0).astype(BF16)

    ones_l = jnp.ones((LANES, LANES), BF16)
    su_l = tri(LANES, lambda a, b: a < b)
    li_l = tri(LANES, lambda a, b: b <= a)
    sl_b = tri(nblk, lambda a, b: b < a)
    li_b = tri(nblk, lambda a, b: b <= a)
    blk_col = lax.broadcasted_iota(jnp.int32, (nblk, 1), 0).astype(F32)
    lane_col = lax.broadcasted_iota(jnp.int32, (LANES, 1), 0).astype(F32)

    def per_expert(e, carry):
        a = aff_ref[e]
        be = lax.bitcast_convert_type(a, jnp.int32)
        te = thr_ref[e]
        gt = be > te
        eq = (be == te)
        eqb = jnp.where(eq, 1.0, 0.0).astype(BF16)
        rank = _dot(eqb, su_l) + _dot(sl_b, _dot(eqb, ones_l).astype(BF16))
        sel = gt | (eq & (rank < need_ref[e]))
        mb = jnp.where(sel, 1.0, 0.0).astype(BF16)
        cnt_b = _dot(mb, ones_l)
        pinc = _dot(li_b, cnt_b.astype(BF16))
        pinc_col = pinc[:, 0:1]
        pexc_col = pinc_col - cnt_b[:, 0:1]
        mt = jnp.where(sel, 1.0, 0.0).T.astype(BF16)
        at = a.T
        at_hi = at.astype(BF16)
        at_lo = (at - at_hi.astype(F32)).astype(BF16)
        for c in range(cap // pc):
            p_row = (lax.broadcasted_iota(jnp.int32, (1, pc), 1) + c * pc).astype(F32)
            blk_row = jnp.sum(jnp.where(pinc_col <= p_row, 1.0, 0.0), axis=0, keepdims=True)
            oh = blk_col == blk_row
            ohb = jnp.where(oh, 1.0, 0.0).astype(BF16)
            r_row = p_row - jnp.sum(jnp.where(oh, pexc_col, 0.0), axis=0, keepdims=True)
            gt_rows = _dot(mt, ohb)
            pc_incl = _dot(li_l, gt_rows.astype(BF16))
            lane_row = jnp.sum(jnp.where(pc_incl <= r_row, 1.0, 0.0), axis=0, keepdims=True)
            idx_ref[e, :, c * pc:(c + 1) * pc] = (blk_row * LANES + lane_row).astype(jnp.int32)
            ag = _dot(at_hi, ohb) + _dot(at_lo, ohb)
            gate_ref[e, :, c * pc:(c + 1) * pc] = jnp.sum(
                jnp.where(lane_col == lane_row, ag, 0.0), axis=0, keepdims=True)
        return carry

    lax.fori_loop(0, ne, per_expert, 0)


def _routing(aff3, cap):
    ne, nblk, _ = aff3.shape
    pc = min(ROUTE_PC, cap)
    return pl.pallas_call(
        functools.partial(_routing_kernel, cap=cap, pc=pc),
        grid=(1,),
        in_specs=[pl.BlockSpec((ne, nblk, LANES), lambda i: (0, 0, 0))],
        out_specs=[
            pl.BlockSpec((ne, 1, cap), lambda i: (0, 0, 0)),
            pl.BlockSpec((ne, 1, cap), lambda i: (0, 0, 0)),
        ],
        out_shape=[
            jax.ShapeDtypeStruct((ne, 1, cap), jnp.int32),
            jax.ShapeDtypeStruct((ne, 1, cap), F32),
        ],
        scratch_shapes=[pltpu.VMEM((ne, 1, LANES), jnp.int32), pltpu.VMEM((ne, 1, LANES), F32)],
        compiler_params=_cparams(("arbitrary",)),
        name="routing",
    )(aff3)


def _moe_kernel(idx_ref, gate_ref, h_hbm, acc_in_hbm, wg_ref, wu_ref, wd_ref, acc_hbm,
                idx_smem, xbuf, abuf, sems):
    del acc_in_hbm
    ch = xbuf.shape[0]
    cp = pltpu.make_async_copy(idx_ref.at[0, 0], idx_smem, sems.at[0])
    cp.start()
    cp.wait()

    def x_copy(r, t):
        return pltpu.make_async_copy(h_hbm.at[pl.ds(t, 1)], xbuf.at[pl.ds(r, 1)], sems.at[1])

    def a_copy(r, t):
        return pltpu.make_async_copy(acc_hbm.at[pl.ds(t, 1)], abuf.at[pl.ds(r, 1)], sems.at[2])

    def s_copy(r, t):
        return pltpu.make_async_copy(abuf.at[pl.ds(r, 1)], acc_hbm.at[pl.ds(t, 1)], sems.at[3])

    def start_gather(r, c):
        t = idx_smem[r]
        x_copy(r, t).start()
        a_copy(r, t).start()
        return c

    def wait_gather(r, c):
        x_copy(r, 0).wait()
        a_copy(r, 0).wait()
        return c

    lax.fori_loop(0, ch, start_gather, 0)
    lax.fori_loop(0, ch, wait_gather, 0)

    x = xbuf[...].astype(BF16)
    gg = _dot(x, wg_ref[0])
    uu = _dot(x, wu_ref[0])
    hid = (gg * (1.0 / (1.0 + jnp.exp(-gg))) * uu).astype(BF16)
    y = _dot(hid, wd_ref[0])
    gate_col = jnp.broadcast_to(gate_ref[0], (LANES, ch)).T[:, 0:1]
    abuf[...] = abuf[...] + y * gate_col

    def start_scatter(r, c):
        s_copy(r, idx_smem[r]).start()
        return c

    def wait_scatter(r, c):
        s_copy(r, 0).wait()
        return c

    lax.fori_loop(0, ch, start_scatter, 0)
    lax.fori_loop(0, ch, wait_scatter, 0)


def _moe(idx3, gate3, h2, acc0, wg, wu, wd, cap):
    n, d = h2.shape
    ne, _, ff = wg.shape
    ch = min(MOE_CH, cap)
    nch = cap // ch
    return pl.pallas_call(
        _moe_kernel,
        grid=(ne, nch),
        in_specs=[
            pl.BlockSpec((1, 1, ch), lambda e, c: (e * nch + c, 0, 0)),
            pl.BlockSpec((1, 1, ch), lambda e, c: (e * nch + c, 0, 0)),
            pl.BlockSpec(memory_space=pl.ANY),
            pl.BlockSpec(memory_space=pl.ANY),
            pl.BlockSpec((1, d, ff), lambda e, c: (e, 0, 0)),
            pl.BlockSpec((1, d, ff), lambda e, c: (e, 0, 0)),
            pl.BlockSpec((1, ff, d), lambda e, c: (e, 0, 0)),
        ],
        out_specs=pl.BlockSpec(memory_space=pl.ANY),
        out_shape=jax.ShapeDtypeStruct((n, d), F32),
        scratch_shapes=[
            pltpu.SMEM((ch,), jnp.int32),
            pltpu.VMEM((ch, d), F32), pltpu.VMEM((ch, d), F32),
            pltpu.SemaphoreType.DMA((4,)),
        ],
        input_output_aliases={3: 0},
        compiler_params=_cparams(("arbitrary", "arbitrary")),
        name="moe_ffn",
    )(idx3.reshape(ne * nch, 1, ch), gate3.reshape(ne * nch, 1, ch), h2, acc0, wg, wu, wd)


def _combine_kernel(x_ref, acc_ref, mod_ref, o_ref):
    d = x_ref.shape[1]
    gt_f = mod_ref[0][:, 5 * d:6 * d]
    o_ref[...] = x_ref[...] + gt_f * acc_ref[...]


def _combine(x1, acc, mod3, s_len):
    n, d = x1.shape
    tm = min(TM_PROJ, s_len)
    spb = s_len // tm
    return pl.pallas_call(
        _combine_kernel,
        grid=(n // tm,),
        in_specs=[
            pl.BlockSpec((tm, d), lambda i: (i, 0)),
            pl.BlockSpec((tm, d), lambda i: (i, 0)),
            pl.BlockSpec((1, 1, N_MOD * d), lambda i: (i // spb, 0, 0)),
        ],
        out_specs=pl.BlockSpec((tm, d), lambda i: (i, 0)),
        out_shape=jax.ShapeDtypeStruct((n, d), F32),
        compiler_params=_cparams(("arbitrary",)),
        name="combine",
    )(x1, acc, mod3)


def _t5_bucket(rel):
    half = NUM_BUCKETS // 2
    max_exact = half // 2
    n = jnp.abs(rel)
    nf = jnp.maximum(n, 1).astype(F32)
    large = max_exact + (jnp.log(nf / max_exact) / math.log(MAX_DISTANCE / max_exact)
                         * (half - max_exact)).astype(jnp.int32)
    large = jnp.minimum(large, half - 1)
    return jnp.where(rel > 0, half, 0) + jnp.where(n < max_exact, n, large)


def _rope_table(s_len):
    quarter = HEAD_DIM // 4
    freqs = ROPE_THETA ** (-jnp.arange(quarter, dtype=F32) / quarter)
    t = jnp.arange(s_len)
    ang_r = freqs[:, None] * (t // GRID_W).astype(F32)[None, :]
    ang_c = freqs[:, None] * (t % GRID_W).astype(F32)[None, :]
    return jnp.concatenate([jnp.cos(ang_r), jnp.sin(ang_r), jnp.cos(ang_c), jnp.sin(ang_c)], axis=0)


def _bias_tables(rel_bias, t_b):
    j = jnp.arange(3 * A_BLK)[:, None]
    i = jnp.arange(A_BLK)[None, :]
    rel = j - A_BLK - i
    ba = jnp.transpose(rel_bias[_t5_bucket(rel)][..., :A_HEADS], (2, 0, 1)) * LOG2E
    bias_a = jnp.where((jnp.abs(rel) <= WINDOW)[None], ba, NEG_BIG).astype(F32)
    jj = jnp.arange(t_b)[:, None]
    ii = jnp.arange(t_b)[None, :]
    tiles = []
    for dd in (-1, 0, 1):
        rel = dd * t_b + jj - ii
        tiles.append(jnp.transpose(rel_bias[_t5_bucket(rel)][..., A_HEADS:], (2, 0, 1)))
    bias_b = (jnp.stack(tiles, axis=1) * LOG2E).astype(F32)
    half = NUM_BUCKETS // 2
    far = jnp.stack([rel_bias[half - 1, A_HEADS:], rel_bias[NUM_BUCKETS - 1, A_HEADS:]], axis=1) * LOG2E
    return bias_a, bias_b, far.astype(F32)


def _gain_column(ga, gb, gc):
    sa = HEAD_DIM ** -0.5 * LOG2E
    sb = B_QK ** -0.5 * LOG2E
    parts = [
        jnp.tile(ga[0] * sa, A_HEADS), jnp.tile(ga[1], A_KV), jnp.ones((A_V,), F32),
        jnp.tile(gb[0] * sb, 2 * B_HEADS), jnp.tile(gb[1], 2 * B_HEADS), jnp.ones((B_V,), F32),
        jnp.tile(gc[0] * sa, C_HEADS), jnp.tile(gc[1], C_KV), jnp.ones((C_V,), F32),
    ]
    return jnp.concatenate(parts).reshape(IN_COLS, 1).astype(F32)


def _trunk(x, mod_l, prep, s_len, bsz):
    n = bsz * s_len
    d = x.shape[-1]
    x2d = x.reshape(n, d)
    t_b = min(T_B, s_len)
    tk_c = min(TK_C, s_len)
    cap = CAPACITY_FACTOR * n // N_EXPERTS
    cs = _rope_table(s_len)
    for l, lp in enumerate(prep["layers"]):
        mod3 = mod_l[l][:, None, :]
        qa, ka, va, qb, kb, vb, qc, kc, vc = _inproj(
            x2d, mod3, lp["g_attn"], lp["w_in_t"], lp["gcol"], cs, bsz, s_len, t_b, tk_c)
        oa = _attn_a(qa, ka, va, prep["bias_a"], lp["sink"])
        ob = _attn_b(qb, kb, vb, prep["bias_b"][t_b], prep["far_b"], lp["lam"], lp["subln_col"])
        oc = _attn_c(qc, kc, vc)
        x1, h2, aff = _outproj(oa, ob, oc, lp["wo_a"], lp["wo_b"], lp["wo_c"], x2d, mod3,
                               lp["g_ffn"], lp["wr_hi"], lp["wr_lo"], s_len)
        idx3, gate3 = _routing(aff.reshape(N_EXPERTS, n // LANES, LANES), cap)
        acc = _moe(idx3, gate3, h2, jnp.zeros((n, d), F32), lp["wg"], lp["wu"], lp["wd"], cap)
        x2d = _combine(x1, acc, mod3, s_len)
    return x2d.reshape(bsz, s_len, d)


def kernel(x_prompt, x_sample, c_prompt, c_sample, w_mod, b_mod, g_attn, g_ffn, w_in, w_out,
           qk_gain_a, qk_gain_b, qk_gain_c, sink_a, lam_b, subln_b, rel_bias,
           w_router, w_gate, w_up, w_down):
    depth = w_mod.shape[0]
    bp, sp, d = x_prompt.shape
    bs, ss, _ = x_sample.shape
    rows = -(-(bp + bs) // 16) * 16
    c_all = jnp.zeros((rows, d), F32).at[:bp].set(c_prompt).at[bp:bp + bs].set(c_sample)
    mod_all = _modulation(c_all, w_mod, b_mod)
    mod_p, mod_s = mod_all[:, :bp], mod_all[:, bp:bp + bs]

    tbs = sorted({min(T_B, sp), min(T_B, ss)})
    tabs = {t: _bias_tables(rel_bias, t) for t in tbs}
    prep = {"bias_a": tabs[tbs[0]][0], "far_b": tabs[tbs[0]][2],
            "bias_b": {t: tabs[t][1] for t in tbs}, "layers": []}
    for l in range(depth):
        lam_init = 0.8 - 0.6 * math.exp(-0.3 * l)
        lp = lam_b[l].astype(F32)
        lam = jnp.exp(jnp.sum(lp[0] * lp[1])) - jnp.exp(jnp.sum(lp[2] * lp[3])) + lam_init
        wo_t = w_out[l].T.astype(BF16)
        wr_t = w_router[l].T
        wr_hi = wr_t.astype(BF16)
        prep["layers"].append({
            "g_attn": g_attn[l].reshape(1, d), "g_ffn": g_ffn[l].reshape(1, d),
            "w_in_t": w_in[l].T.astype(BF16),
            "gcol": _gain_column(qk_gain_a[l], qk_gain_b[l], qk_gain_c[l]),
            "sink": (sink_a[l] * LOG2E).astype(F32),
            "lam": lam.reshape(1).astype(F32),
            "subln_col": (subln_b[l] * (1.0 - lam_init)).reshape(HEAD_DIM, 1).astype(F32),
            "wo_a": wo_t[:, 0:A_Q], "wo_b": wo_t[:, A_Q:A_Q + B_Q], "wo_c": wo_t[:, A_Q + B_Q:],
            "wr_hi": wr_hi, "wr_lo": (wr_t - wr_hi.astype(F32)).astype(BF16),
            "wg": w_gate[l].astype(BF16), "wu": w_up[l].astype(BF16), "wd": w_down[l].astype(BF16),
        })
    y_p = _trunk(x_prompt, mod_p, prep, sp, bp)
    y_s = _trunk(x_sample, mod_s, prep, ss, bs)
    return (y_p, y_s)
```

```python
import functools
import math

import jax
import jax.numpy as jnp
from jax import lax
from jax.experimental import pallas as pl
from jax.experimental.pallas import tpu as pltpu

F32 = jnp.float32
BF16 = jnp.bfloat16

HEAD_DIM = 64
A_HEADS, A_KV = 6, 2
B_HEADS, B_QK = 4, 32
C_HEADS, C_KV = 6, 2
WINDOW = 128
A_BLK = 128
NUM_BUCKETS, MAX_DISTANCE = 32, 128
ROPE_THETA = 10000.0
GRID_W = 64
N_EXPERTS = 16
CAPACITY_FACTOR = 2
N_MOD = 6
EPS = 1e-6
NEG_BIG = -1e30
LOG2E = math.log2(math.e)

A_Q, A_K, A_V = A_HEADS * HEAD_DIM, A_KV * HEAD_DIM, A_KV * HEAD_DIM
B_Q, B_K, B_V = B_HEADS * HEAD_DIM, B_HEADS * HEAD_DIM, B_HEADS * HEAD_DIM
C_Q, C_K, C_V = C_HEADS * HEAD_DIM, C_KV * HEAD_DIM, C_KV * HEAD_DIM
A_OFF = 0
B_OFF = A_Q + A_K + A_V
C_OFF = B_OFF + B_Q + B_K + B_V
IN_COLS = C_OFF + C_Q + C_K + C_V

LANES = 128
VMEM_LIMIT_BYTES = 56 * 1024 * 1024

TM_PROJ = 512
TQ_C, TK_C = 512, 1024
T_B = 512
MOE_CH = 256
ROUTE_PC = 512


def _cparams(sem):
    return pltpu.CompilerParams(dimension_semantics=sem, vmem_limit_bytes=VMEM_LIMIT_BYTES)


def _dot(a, b):
    return jnp.dot(a, b, preferred_element_type=F32)


def _dot_nt(a, b):
    return lax.dot_general(a, b, (((1,), (1,)), ((), ())), preferred_element_type=F32)


def _mod_kernel(c_ref, w_ref, b_ref, o_ref):
    c = c_ref[...]
    a = c * (1.0 / (1.0 + jnp.exp(-c)))
    a_hi = a.astype(BF16)
    a_lo = (a - a_hi.astype(F32)).astype(BF16)
    w = w_ref[0]
    w_hi = w.astype(BF16)
    w_lo = (w - w_hi.astype(F32)).astype(BF16)
    acc = _dot(a_hi, w_hi) + _dot(a_lo, w_hi) + _dot(a_hi, w_lo)
    o_ref[0] = acc + b_ref[0]


def _modulation(c_all, w_mod, b_mod):
    depth, d, nm = w_mod.shape
    rows = c_all.shape[0]
    bn = 1024
    return pl.pallas_call(
        _mod_kernel,
        grid=(depth, nm // bn),
        in_specs=[
            pl.BlockSpec((rows, d), lambda l, j: (0, 0)),
            pl.BlockSpec((1, d, bn), lambda l, j: (l, 0, j)),
            pl.BlockSpec((1, 1, bn), lambda l, j: (l, 0, j)),
        ],
        out_specs=pl.BlockSpec((1, rows, bn), lambda l, j: (l, 0, j)),
        out_shape=jax.ShapeDtypeStruct((depth, rows, nm), F32),
        compiler_params=_cparams(("arbitrary", "arbitrary")),
        name="modulation",
    )(c_all, w_mod, b_mod.reshape(depth, 1, nm))


def _segnorm(rows, seglen):
    r, t = rows.shape
    r3 = rows.reshape(r // seglen, seglen, t)
    ms = jnp.mean(r3 * r3, axis=1, keepdims=True)
    return (r3 * lax.rsqrt(ms + EPS)).reshape(r, t)


def _rope(rows, cs):
    r, t = rows.shape
    x = rows.reshape(r // HEAD_DIM, HEAD_DIM, t)
    cr, sr, cc, sc = cs[None, 0:16], cs[None, 16:32], cs[None, 32:48], cs[None, 48:64]
    ar, br, ac, bc = x[:, 0:16], x[:, 16:32], x[:, 32:48], x[:, 48:64]
    out = jnp.concatenate(
        [ar * cr - br * sr, br * cr + ar * sr, ac * cc - bc * sc, bc * cc + ac * sc], axis=1)
    return out.reshape(r, t)


def _emit_k(k_ref, kt):
    nh = kt.shape[0] // HEAD_DIM
    for p in range(kt.shape[0] // LANES):
        tok = kt[p * LANES:(p + 1) * LANES].T
        k_ref[0, 2 * p] = tok[:, 0:HEAD_DIM].astype(k_ref.dtype)
        k_ref[0, 2 * p + 1] = tok[:, HEAD_DIM:LANES].astype(k_ref.dtype)
    del nh


def _emit_vt(v_ref, vt):
    v_ref[0] = vt.astype(v_ref.dtype)


def _inproj_kernel(x_ref, mod_ref, g_ref, w_ref, gcol_ref, cs_ref,
                   qa_ref, ka_ref, va_ref, qb_ref, kb_ref, vb_ref, qc_ref, kc_ref, vc_ref):
    x = x_ref[...]
    mod = mod_ref[0]
    d = x.shape[1]
    sh, sc = mod[:, 0:d], mod[:, d:2 * d]
    ms = jnp.mean(x * x, axis=-1, keepdims=True)
    h = x * lax.rsqrt(ms + EPS) * g_ref[...]
    h = (h * (1.0 + sc) + sh).astype(BF16)
    pt = _dot_nt(w_ref[...], h)
    gcol = gcol_ref[...]
    cs = cs_ref[...]

    def rows(off, n):
        return pt[off:off + n], gcol[off:off + n]

    q, g = rows(A_OFF, A_Q)
    qa_ref[0] = (_segnorm(q, HEAD_DIM) * g).astype(qa_ref.dtype)
    k, g = rows(A_OFF + A_Q, A_K)
    _emit_k(ka_ref, _segnorm(k, HEAD_DIM) * g)
    _emit_vt(va_ref, pt[A_OFF + A_Q + A_K:A_OFF + A_Q + A_K + A_V])
    q, g = rows(B_OFF, B_Q)
    qb_ref[0] = (_segnorm(q, B_QK) * g).astype(qb_ref.dtype)
    k, g = rows(B_OFF + B_Q, B_K)
    _emit_k(kb_ref, _segnorm(k, B_QK) * g)
    _emit_vt(vb_ref, pt[B_OFF + B_Q + B_K:B_OFF + B_Q + B_K + B_V])
    q, g = rows(C_OFF, C_Q)
    qc_ref[0] = _rope(_segnorm(q, HEAD_DIM) * g, cs).astype(qc_ref.dtype)
    k, g = rows(C_OFF + C_Q, C_K)
    _emit_k(kc_ref, _rope(_segnorm(k, HEAD_DIM) * g, cs))
    _emit_vt(vc_ref, pt[C_OFF + C_Q + C_K:C_OFF + C_Q + C_K + C_V])


def _inproj(x2d, mod3, g_attn, w_in_t, gcol, cs, bsz, s_len):
    n, d = x2d.shape
    tm = min(TM_PROJ, s_len)
    spb = s_len // tm

    def qt_spec(rows_):
        return pl.BlockSpec((1, rows_, tm), lambda i: (i // spb, 0, i % spb))

    def k_spec(nh):
        return pl.BlockSpec((1, nh, tm, HEAD_DIM), lambda i: (i // spb, 0, i % spb, 0))

    def qt_shape(rows_):
        return jax.ShapeDtypeStruct((bsz, rows_, s_len), BF16)

    def k_shape(nh):
        return jax.ShapeDtypeStruct((bsz, nh, s_len, HEAD_DIM), BF16)

    return pl.pallas_call(
        _inproj_kernel,
        grid=(n // tm,),
        in_specs=[
            pl.BlockSpec((tm, d), lambda i: (i, 0)),
            pl.BlockSpec((1, 1, N_MOD * d), lambda i: (i // spb, 0, 0)),
            pl.BlockSpec((1, d), lambda i: (0, 0)),
            pl.BlockSpec((IN_COLS, d), lambda i: (0, 0)),
            pl.BlockSpec((IN_COLS, 1), lambda i: (0, 0)),
            pl.BlockSpec((HEAD_DIM, tm), lambda i: (0, i % spb)),
        ],
        out_specs=[
            qt_spec(A_Q), k_spec(A_KV), qt_spec(A_V),
            qt_spec(B_Q), k_spec(B_HEADS), qt_spec(B_V),
            qt_spec(C_Q), k_spec(C_KV), qt_spec(C_V),
        ],
        out_shape=[
            qt_shape(A_Q), k_shape(A_KV), qt_shape(A_V),
            qt_shape(B_Q), k_shape(B_HEADS), qt_shape(B_V),
            qt_shape(C_Q), k_shape(C_KV), qt_shape(C_V),
        ],
        compiler_params=_cparams(("arbitrary",)),
        name="inproj",
    )(x2d, mod3, g_attn, w_in_t, gcol, cs)


def _pipe_init(m_ref, l_ref, alpha_ref, acc_ref, p_ref):
    m_ref[...] = jnp.full(m_ref.shape, NEG_BIG, F32)
    l_ref[...] = jnp.zeros(l_ref.shape, F32)
    alpha_ref[...] = jnp.ones(alpha_ref.shape, F32)
    acc_ref[...] = jnp.zeros(acc_ref.shape, F32)
    p_ref[...] = jnp.zeros(p_ref.shape, p_ref.dtype)


def _pipe_pv(vt, p_ref, alpha_ref, acc_ref):
    acc_ref[...] = alpha_ref[...] * acc_ref[...] + _dot(vt, p_ref[...])


def _pipe_sm(c, s_ref, bm_ref, m_ref, l_ref, alpha_ref, p_ref):
    m_old = m_ref[...]
    m_new = jnp.maximum(m_old, bm_ref[...] + c)
    alpha = jnp.exp2(m_old - m_new)
    p = jnp.exp2(s_ref[...] - (m_new - c))
    l_ref[...] = alpha * l_ref[...] + jnp.sum(p, axis=0, keepdims=True)
    p_ref[...] = p.astype(p_ref.dtype)
    m_ref[...] = m_new
    alpha_ref[...] = alpha


def _pipe_qk(kc, q_ref, s_ref, bm_ref, tile=None):
    s = _dot(kc, q_ref[...])
    if tile is not None:
        s = s + tile
    s_ref[...] = s
    bm_ref[...] = jnp.max(s, axis=0, keepdims=True)


def _pipe_scratch(tk, n):
    return [
        pltpu.VMEM((HEAD_DIM, n), BF16),
        pltpu.VMEM((tk, n), F32),
        pltpu.VMEM((tk, n), BF16),
        pltpu.VMEM((1, n), F32),
        pltpu.VMEM((1, n), F32),
        pltpu.VMEM((1, n), F32),
        pltpu.VMEM((1, n), F32),
        pltpu.VMEM((HEAD_DIM, n), F32),
    ]


def _attn_c_kernel(q_ref, k_ref, v_ref, o_ref, qs_ref, s_ref, p_ref, bm_ref, m_ref, l_ref, alpha_ref, acc_ref,
                   *, tk):
    g = C_HEADS // C_KV
    tq = q_ref.shape[2]
    nk = k_ref.shape[2] // tk
    q3 = q_ref[0]
    qs_ref[...] = jnp.concatenate([q3[h * HEAD_DIM:(h + 1) * HEAD_DIM] for h in range(g)], axis=1)
    _pipe_init(m_ref, l_ref, alpha_ref, acc_ref, p_ref)

    def kchunk(j):
        return k_ref[0, 0, pl.ds(pl.multiple_of(j * tk, tk), tk), :]

    def vchunk(j):
        return v_ref[0, :, pl.ds(pl.multiple_of(j * tk, tk), tk)]

    _pipe_qk(kchunk(0), qs_ref, s_ref, bm_ref)

    def body(j, carry):
        _pipe_pv(vchunk(jnp.maximum(j - 1, 0)), p_ref, alpha_ref, acc_ref)
        _pipe_sm(0.0, s_ref, bm_ref, m_ref, l_ref, alpha_ref, p_ref)
        _pipe_qk(kchunk(j + 1), qs_ref, s_ref, bm_ref)
        return carry

    lax.fori_loop(0, nk - 1, body, 0)
    if nk > 1:
        _pipe_pv(vchunk(nk - 2), p_ref, alpha_ref, acc_ref)
    _pipe_sm(0.0, s_ref, bm_ref, m_ref, l_ref, alpha_ref, p_ref)
    _pipe_pv(vchunk(nk - 1), p_ref, alpha_ref, acc_ref)
    o = acc_ref[...] / l_ref[...]
    o_ref[0] = jnp.concatenate([o[:, h * tq:(h + 1) * tq] for h in range(g)], axis=0).astype(o_ref.dtype)


def _attn_c(qt, k, vt):
    bsz, _, s_len = qt.shape
    g = C_HEADS // C_KV
    tq = min(TQ_C, s_len)
    tk = min(TK_C, s_len)
    return pl.pallas_call(
        functools.partial(_attn_c_kernel, tk=tk),
        grid=(bsz, C_KV, s_len // tq),
        in_specs=[
            pl.BlockSpec((1, g * HEAD_DIM, tq), lambda b, kv, i: (b, kv, i)),
            pl.BlockSpec((1, 1, s_len, HEAD_DIM), lambda b, kv, i: (b, kv, 0, 0)),
            pl.BlockSpec((1, HEAD_DIM, s_len), lambda b, kv, i: (b, kv, 0)),
        ],
        out_specs=pl.BlockSpec((1, g * HEAD_DIM, tq), lambda b, kv, i: (b, kv, i)),
        out_shape=jax.ShapeDtypeStruct((bsz, C_Q, s_len), BF16),
        scratch_shapes=_pipe_scratch(tk, g * tq),
        compiler_params=_cparams(("arbitrary", "arbitrary", "arbitrary")),
        name="attn_c",
    )(qt, k, vt)


B_WIN = 3
B_TILES = 5


def _attn_b_kernel(far_ref, lam_ref, q_ref, k_ref, v_ref, bias_ref, gcol_ref, o_ref,
                   qs_ref, s_ref, p_ref, bm_ref, m_ref, l_ref, alpha_ref, acc_ref):
    h = pl.program_id(1)
    i = pl.program_id(2)
    t = q_ref.shape[2]
    nk = k_ref.shape[2] // t
    win = min(B_WIN, nk)
    nfar = nk - win
    q = q_ref[0]
    z = jnp.zeros((B_QK, t), q.dtype)
    qs_ref[...] = jnp.concatenate(
        [jnp.concatenate([q[0:B_QK], z], axis=0), jnp.concatenate([z, q[B_QK:HEAD_DIM]], axis=0)], axis=1)
    _pipe_init(m_ref, l_ref, alpha_ref, acc_ref, p_ref)
    c_neg = far_ref[h, 0]
    c_pos = far_ref[h, 1]
    w0 = jnp.clip(i - 1, 0, nk - win)

    def kchunk(j):
        return k_ref[0, 0, pl.ds(pl.multiple_of(j * t, t), t), :]

    def vchunk(j):
        return v_ref[0, :, pl.ds(pl.multiple_of(j * t, t), t)]

    def far_chunk(f):
        return jnp.where(f < w0, f, f + win)

    def qk_win(j):
        b = bias_ref[0, j - i + B_TILES // 2]
        _pipe_qk(kchunk(j), qs_ref, s_ref, bm_ref, jnp.concatenate([b, b], axis=1))

    def sm(c):
        _pipe_sm(c, s_ref, bm_ref, m_ref, l_ref, alpha_ref, p_ref)

    def pv(j):
        _pipe_pv(vchunk(j), p_ref, alpha_ref, acc_ref)

    qk_win(w0)
    for w in range(win):
        if w > 0:
            pv(w0 + w - 1)
        sm(0.0)
        if w + 1 < win:
            qk_win(w0 + w + 1)
        elif nfar > 0:
            _pipe_qk(kchunk(far_chunk(0)), qs_ref, s_ref, bm_ref)

    def far_step(f):
        pv(jnp.where(f == 0, w0 + win - 1, far_chunk(f - 1)))
        sm(jnp.where(f < w0, c_neg, c_pos))
        _pipe_qk(kchunk(far_chunk(f + 1)), qs_ref, s_ref, bm_ref)

    if nfar > 0:
        n_full = nfar - 1

        def pair(f2, carry):
            far_step(2 * f2)
            far_step(2 * f2 + 1)
            return carry

        lax.fori_loop(0, n_full // 2, pair, 0)
        if n_full % 2:
            far_step(n_full - 1)
        pv(far_chunk(nfar - 2) if nfar > 1 else w0 + win - 1)
        sm(jnp.where(nfar - 1 < w0, c_neg, c_pos))
        pv(far_chunk(nfar - 1))
    else:
        pv(w0 + win - 1)

    o = acc_ref[...] / l_ref[...]
    o = o[:, 0:t] - lam_ref[0] * o[:, t:2 * t]
    ms = jnp.mean(o * o, axis=0, keepdims=True)
    o_ref[0] = (o * lax.rsqrt(ms + EPS) * gcol_ref[...]).astype(o_ref.dtype)


def _attn_b(qt, k, vt, bias_tiles, far, lam, gcol):
    bsz, _, s_len = qt.shape
    t = bias_tiles.shape[-1]
    return pl.pallas_call(
        _attn_b_kernel,
        grid_spec=pltpu.PrefetchScalarGridSpec(
            num_scalar_prefetch=2,
            grid=(bsz, B_HEADS, s_len // t),
            in_specs=[
                pl.BlockSpec((1, HEAD_DIM, t), lambda b, h, i, *_: (b, h, i)),
                pl.BlockSpec((1, 1, s_len, HEAD_DIM), lambda b, h, i, *_: (b, h, 0, 0)),
                pl.BlockSpec((1, HEAD_DIM, s_len), lambda b, h, i, *_: (b, h, 0)),
                pl.BlockSpec((1, B_TILES, t, t), lambda b, h, i, *_: (h, 0, 0, 0)),
                pl.BlockSpec((HEAD_DIM, 1), lambda b, h, i, *_: (0, 0)),
            ],
            out_specs=pl.BlockSpec((1, HEAD_DIM, t), lambda b, h, i, *_: (b, h, i)),
            scratch_shapes=_pipe_scratch(t, 2 * t),
        ),
        out_shape=jax.ShapeDtypeStruct((bsz, B_Q, s_len), BF16),
        compiler_params=_cparams(("arbitrary", "arbitrary", "arbitrary")),
        name="attn_b",
    )(far, lam, qt, k, vt, bias_tiles, gcol)


def _attn_a_kernel(sink_ref, q_ref, kp_ref, kc_ref, kn_ref, vp_ref, vc_ref, vn_ref, bias_ref, o_ref):
    n = pl.program_id(1)
    nb = pl.num_programs(1)
    g = A_HEADS // A_KV
    blk = A_BLK
    row = lax.broadcasted_iota(jnp.int32, (3 * blk, 1), 0)
    valid = ((row >= blk) | (n > 0)) & ((row < 2 * blk) | (n < nb - 1))
    for kv in range(A_KV):
        q3 = q_ref[0, kv * g * HEAD_DIM:(kv + 1) * g * HEAD_DIM]
        qcat = jnp.concatenate([q3[h * HEAD_DIM:(h + 1) * HEAD_DIM] for h in range(g)], axis=1)
        kwin = jnp.concatenate([kp_ref[0, kv], kc_ref[0, kv], kn_ref[0, kv]], axis=0)
        hs = slice(kv * HEAD_DIM, (kv + 1) * HEAD_DIM)
        vwin = jnp.concatenate([vp_ref[0, hs], vc_ref[0, hs], vn_ref[0, hs]], axis=1)
        bias = jnp.concatenate([bias_ref[kv * g + h] for h in range(g)], axis=1)
        s = jnp.where(valid, _dot(kwin, qcat) + bias, NEG_BIG)
        sink = jnp.concatenate(
            [jnp.full((1, blk), sink_ref[kv * g + h], F32) for h in range(g)], axis=1)
        m = jnp.maximum(jnp.max(s, axis=0, keepdims=True), sink)
        p = jnp.exp2(s - m)
        denom = jnp.sum(p, axis=0, keepdims=True) + jnp.exp2(sink - m)
        o = _dot(vwin, p.astype(BF16)) / denom
        o_ref[0, kv * g * HEAD_DIM:(kv + 1) * g * HEAD_DIM] = jnp.concatenate(
            [o[:, h * blk:(h + 1) * blk] for h in range(g)], axis=0).astype(o_ref.dtype)


def _attn_a(qt, k, vt, bias, sink):
    bsz, _, s_len = qt.shape
    nb = s_len // A_BLK

    def kspec(off):
        return pl.BlockSpec((1, A_KV, A_BLK, HEAD_DIM),
                            lambda b, n, *_: (b, 0, jnp.clip(n + off, 0, nb - 1), 0))

    def vspec(off):
        return pl.BlockSpec((1, A_V, A_BLK), lambda b, n, *_: (b, 0, jnp.clip(n + off, 0, nb - 1)))

    return pl.pallas_call(
        _attn_a_kernel,
        grid_spec=pltpu.PrefetchScalarGridSpec(
            num_scalar_prefetch=1,
            grid=(bsz, nb),
            in_specs=[
                pl.BlockSpec((1, A_Q, A_BLK), lambda b, n, *_: (b, 0, n)),
                kspec(-1), kspec(0), kspec(1), vspec(-1), vspec(0), vspec(1),
                pl.BlockSpec((A_HEADS, 3 * A_BLK, A_BLK), lambda b, n, *_: (0, 0, 0)),
            ],
            out_specs=pl.BlockSpec((1, A_Q, A_BLK), lambda b, n, *_: (b, 0, n)),
        ),
        out_shape=jax.ShapeDtypeStruct((bsz, A_Q, s_len), BF16),
        compiler_params=_cparams(("arbitrary", "arbitrary")),
        name="attn_a",
    )(sink, qt, k, k, k, vt, vt, vt, bias)


def _outproj_kernel(oa_ref, ob_ref, oc_ref, wa_ref, wb_ref, wc_ref, x_ref, mod_ref, g_ref,
                    wrh_ref, wrl_ref, x1_ref, h2_ref, aff_ref):
    out_t = _dot(wa_ref[...], oa_ref[0]) + _dot(wb_ref[...], ob_ref[0]) + _dot(wc_ref[...], oc_ref[0])
    out = out_t.T
    mod = mod_ref[0]
    d = out.shape[1]
    gt_a = mod[:, 2 * d:3 * d]
    sh_f, sc_f = mod[:, 3 * d:4 * d], mod[:, 4 * d:5 * d]
    x1 = x_ref[...] + gt_a * out
    x1_ref[...] = x1
    ms = jnp.mean(x1 * x1, axis=-1, keepdims=True)
    h2 = x1 * lax.rsqrt(ms + EPS) * g_ref[...]
    h2 = h2 * (1.0 + sc_f) + sh_f
    h2_ref[...] = h2
    hi = h2.astype(BF16)
    lo = (h2 - hi.astype(F32)).astype(BF16)
    lt = _dot_nt(wrh_ref[...], hi) + _dot_nt(wrh_ref[...], lo) + _dot_nt(wrl_ref[...], hi)
    lt = lt - jnp.max(lt, axis=0, keepdims=True)
    e = jnp.exp(lt)
    aff_ref[...] = e / jnp.sum(e, axis=0, keepdims=True)


def _outproj(oa, ob, oc, wa, wb, wc, x2d, mod3, g_ffn, wr_hi, wr_lo, s_len):
    n, d = x2d.shape
    tm = min(TM_PROJ, s_len)
    spb = s_len // tm

    def ot_spec(rows_):
        return pl.BlockSpec((1, rows_, tm), lambda i: (i // spb, 0, i % spb))

    def full(shape):
        return pl.BlockSpec(shape, lambda i: tuple(0 for _ in shape))

    return pl.pallas_call(
        _outproj_kernel,
        grid=(n // tm,),
        in_specs=[
            ot_spec(A_Q), ot_spec(B_Q), ot_spec(C_Q),
            full((d, A_Q)), full((d, B_Q)), full((d, C_Q)),
            pl.BlockSpec((tm, d), lambda i: (i, 0)),
            pl.BlockSpec((1, 1, N_MOD * d), lambda i: (i // spb, 0, 0)),
            full((1, d)), full((N_EXPERTS, d)), full((N_EXPERTS, d)),
        ],
        out_specs=[
            pl.BlockSpec((tm, d), lambda i: (i, 0)),
            pl.BlockSpec((tm, d), lambda i: (i, 0)),
            pl.BlockSpec((N_EXPERTS, tm), lambda i: (0, i)),
        ],
        out_shape=[
            jax.ShapeDtypeStruct((n, d), F32),
            jax.ShapeDtypeStruct((n, d), F32),
            jax.ShapeDtypeStruct((N_EXPERTS, n), F32),
        ],
        compiler_params=_cparams(("arbitrary",)),
        name="outproj",
    )(oa, ob, oc, wa, wb, wc, x2d, mod3, g_ffn, wr_hi, wr_lo)


def _routing_kernel(aff_ref, idx_ref, gate_ref, thr_ref, need_ref, *, cap, pc):
    ne, nblk, _ = aff_ref.shape
    bits = lax.bitcast_convert_type(aff_ref[...], jnp.int32)

    def bis(it, t):
        cand = t | jnp.left_shift(jnp.int32(1), 30 - it)
        cnt = jnp.sum((bits >= cand).astype(F32), axis=(1, 2), keepdims=True)
        return jnp.where(cnt >= cap, cand, t)

    thr = lax.fori_loop(0, 31, bis, jnp.zeros((ne, 1, 1), jnp.int32))
    n_gt = jnp.sum((bits > thr).astype(F32), axis=(1, 2), keepdims=True)
    thr_ref[...] = jnp.broadcast_to(thr, thr_ref.shape)
    need_ref[...] = jnp.broadcast_to(cap - n_gt, need_ref.shape)

    def tri(nn, fn):
        a = lax.broadcasted_iota(jnp.int32, (nn, nn), 0)
        b = lax.broadcasted_iota(jnp.int32, (nn, nn), 1)
        return jnp.where(fn(a, b), 1.0, 0.0).astype(BF16)

    ones_l = jnp.ones((LANES, LANES), BF16)
    su_l = tri(LANES, lambda a, b: a < b)
    li_l = tri(LANES, lambda a, b: b <= a)
    sl_b = tri(nblk, lambda a, b: b < a)
    li_b = tri(nblk, lambda a, b: b <= a)
    blk_col = lax.broadcasted_iota(jnp.int32, (nblk, 1), 0).astype(F32)
    lane_col = lax.broadcasted_iota(jnp.int32, (LANES, 1), 0).astype(F32)

    def per_expert(e, carry):
        a = aff_ref[e]
        be = lax.bitcast_convert_type(a, jnp.int32)
        te = thr_ref[e]
        gt = be > te
        eq = (be == te)
        eqb = jnp.where(eq, 1.0, 0.0).astype(BF16)
        rank = _dot(eqb, su_l) + _dot(sl_b, _dot(eqb, ones_l).astype(BF16))
        sel = gt | (eq & (rank < need_ref[e]))
        mb = jnp.where(sel, 1.0, 0.0).astype(BF16)
        cnt_b = _dot(mb, ones_l)
        pinc = _dot(li_b, cnt_b.astype(BF16))
        pinc_col = pinc[:, 0:1]
        pexc_col = pinc_col - cnt_b[:, 0:1]
        mt = jnp.where(sel, 1.0, 0.0).T.astype(BF16)
        at = a.T
        at_hi = at.astype(BF16)
        at_lo = (at - at_hi.astype(F32)).astype(BF16)
        for c in range(cap // pc):
            p_row = (lax.broadcasted_iota(jnp.int32, (1, pc), 1) + c * pc).astype(F32)
            blk_row = jnp.sum(jnp.where(pinc_col <= p_row, 1.0, 0.0), axis=0, keepdims=True)
            oh = blk_col == blk_row
            ohb = jnp.where(oh, 1.0, 0.0).astype(BF16)
            r_row = p_row - jnp.sum(jnp.where(oh, pexc_col, 0.0), axis=0, keepdims=True)
            gt_rows = _dot(mt, ohb)
            pc_incl = _dot(li_l, gt_rows.astype(BF16))
            lane_row = jnp.sum(jnp.where(pc_incl <= r_row, 1.0, 0.0), axis=0, keepdims=True)
            idx_ref[e, :, c * pc:(c + 1) * pc] = (blk_row * LANES + lane_row).astype(jnp.int32)
            ag = _dot(at_hi, ohb) + _dot(at_lo, ohb)
            gate_ref[e, :, c * pc:(c + 1) * pc] = jnp.sum(
                jnp.where(lane_col == lane_row, ag, 0.0), axis=0, keepdims=True)
        return carry

    lax.fori_loop(0, ne, per_expert, 0)


def _routing(aff3, cap):
    ne, nblk, _ = aff3.shape
    pc = min(ROUTE_PC, cap)
    return pl.pallas_call(
        functools.partial(_routing_kernel, cap=cap, pc=pc),
        grid=(1,),
        in_specs=[pl.BlockSpec((ne, nblk, LANES), lambda i: (0, 0, 0))],
        out_specs=[
            pl.BlockSpec((ne, 1, cap), lambda i: (0, 0, 0)),
            pl.BlockSpec((ne, 1, cap), lambda i: (0, 0, 0)),
        ],
        out_shape=[
            jax.ShapeDtypeStruct((ne, 1, cap), jnp.int32),
            jax.ShapeDtypeStruct((ne, 1, cap), F32),
        ],
        scratch_shapes=[pltpu.VMEM((ne, 1, LANES), jnp.int32), pltpu.VMEM((ne, 1, LANES), F32)],
        compiler_params=_cparams(("arbitrary",)),
        name="routing",
    )(aff3)


def _moe_kernel(idx_ref, gate_ref, h_hbm, acc_in_hbm, wg_ref, wu_ref, wd_ref, acc_hbm,
                idx_smem, xbuf, abuf, sems):
    del acc_in_hbm
    ch = xbuf.shape[0]
    cp = pltpu.make_async_copy(idx_ref.at[0, 0], idx_smem, sems.at[0])
    cp.start()
    cp.wait()

    def x_copy(r, t):
        return pltpu.make_async_copy(h_hbm.at[pl.ds(t, 1)], xbuf.at[pl.ds(r, 1)], sems.at[1])

    def a_copy(r, t):
        return pltpu.make_async_copy(acc_hbm.at[pl.ds(t, 1)], abuf.at[pl.ds(r, 1)], sems.at[2])

    def s_copy(r, t):
        return pltpu.make_async_copy(abuf.at[pl.ds(r, 1)], acc_hbm.at[pl.ds(t, 1)], sems.at[3])

    def start_gather(r, c):
        t = idx_smem[r]
        x_copy(r, t).start()
        a_copy(r, t).start()
        return c

    def wait_gather(r, c):
        x_copy(r, 0).wait()
        a_copy(r, 0).wait()
        return c

    lax.fori_loop(0, ch, start_gather, 0)
    lax.fori_loop(0, ch, wait_gather, 0)

    x = xbuf[...].astype(BF16)
    gg = _dot(x, wg_ref[0])
    uu = _dot(x, wu_ref[0])
    hid = (gg * (1.0 / (1.0 + jnp.exp(-gg))) * uu).astype(BF16)
    y = _dot(hid, wd_ref[0])
    gate_col = jnp.broadcast_to(gate_ref[0], (LANES, ch)).T[:, 0:1]
    abuf[...] = abuf[...] + y * gate_col

    def start_scatter(r, c):
        s_copy(r, idx_smem[r]).start()
        return c

    def wait_scatter(r, c):
        s_copy(r, 0).wait()
        return c

    lax.fori_loop(0, ch, start_scatter, 0)
    lax.fori_loop(0, ch, wait_scatter, 0)


def _moe(idx3, gate3, h2, acc0, wg, wu, wd, cap):
    n, d = h2.shape
    ne, _, ff = wg.shape
    ch = min(MOE_CH, cap)
    nch = cap // ch
    return pl.pallas_call(
        _moe_kernel,
        grid=(ne, nch),
        in_specs=[
            pl.BlockSpec((1, 1, ch), lambda e, c: (e * nch + c, 0, 0)),
            pl.BlockSpec((1, 1, ch), lambda e, c: (e * nch + c, 0, 0)),
            pl.BlockSpec(memory_space=pl.ANY),
            pl.BlockSpec(memory_space=pl.ANY),
            pl.BlockSpec((1, d, ff), lambda e, c: (e, 0, 0)),
            pl.BlockSpec((1, d, ff), lambda e, c: (e, 0, 0)),
            pl.BlockSpec((1, ff, d), lambda e, c: (e, 0, 0)),
        ],
        out_specs=pl.BlockSpec(memory_space=pl.ANY),
        out_shape=jax.ShapeDtypeStruct((n, d), F32),
        scratch_shapes=[
            pltpu.SMEM((ch,), jnp.int32),
            pltpu.VMEM((ch, d), F32), pltpu.VMEM((ch, d), F32),
            pltpu.SemaphoreType.DMA((4,)),
        ],
        input_output_aliases={3: 0},
        compiler_params=_cparams(("arbitrary", "arbitrary")),
        name="moe_ffn",
    )(idx3.reshape(ne * nch, 1, ch), gate3.reshape(ne * nch, 1, ch), h2, acc0, wg, wu, wd)


def _combine_kernel(x_ref, acc_ref, mod_ref, o_ref):
    d = x_ref.shape[1]
    gt_f = mod_ref[0][:, 5 * d:6 * d]
    o_ref[...] = x_ref[...] + gt_f * acc_ref[...]


def _combine(x1, acc, mod3, s_len):
    n, d = x1.shape
    tm = min(TM_PROJ, s_len)
    spb = s_len // tm
    return pl.pallas_call(
        _combine_kernel,
        grid=(n // tm,),
        in_specs=[
            pl.BlockSpec((tm, d), lambda i: (i, 0)),
            pl.BlockSpec((tm, d), lambda i: (i, 0)),
            pl.BlockSpec((1, 1, N_MOD * d), lambda i: (i // spb, 0, 0)),
        ],
        out_specs=pl.BlockSpec((tm, d), lambda i: (i, 0)),
        out_shape=jax.ShapeDtypeStruct((n, d), F32),
        compiler_params=_cparams(("arbitrary",)),
        name="combine",
    )(x1, acc, mod3)


def _t5_bucket(rel):
    half = NUM_BUCKETS // 2
    max_exact = half // 2
    n = jnp.abs(rel)
    nf = jnp.maximum(n, 1).astype(F32)
    large = max_exact + (jnp.log(nf / max_exact) / math.log(MAX_DISTANCE / max_exact)
                         * (half - max_exact)).astype(jnp.int32)
    large = jnp.minimum(large, half - 1)
    return jnp.where(rel > 0, half, 0) + jnp.where(n < max_exact, n, large)


def _rope_table(s_len):
    quarter = HEAD_DIM // 4
    freqs = ROPE_THETA ** (-jnp.arange(quarter, dtype=F32) / quarter)
    t = jnp.arange(s_len)
    ang_r = freqs[:, None] * (t // GRID_W).astype(F32)[None, :]
    ang_c = freqs[:, None] * (t % GRID_W).astype(F32)[None, :]
    return jnp.concatenate([jnp.cos(ang_r), jnp.sin(ang_r), jnp.cos(ang_c), jnp.sin(ang_c)], axis=0)


def _bias_tables(rel_bias, t_b):
    j = jnp.arange(3 * A_BLK)[:, None]
    i = jnp.arange(A_BLK)[None, :]
    rel = j - A_BLK - i
    ba = jnp.transpose(rel_bias[_t5_bucket(rel)][..., :A_HEADS], (2, 0, 1)) * LOG2E
    bias_a = jnp.where((jnp.abs(rel) <= WINDOW)[None], ba, NEG_BIG).astype(F32)
    jj = jnp.arange(t_b)[:, None]
    ii = jnp.arange(t_b)[None, :]
    tiles = []
    for dd in range(-(B_TILES // 2), B_TILES // 2 + 1):
        rel = dd * t_b + jj - ii
        tiles.append(jnp.transpose(rel_bias[_t5_bucket(rel)][..., A_HEADS:], (2, 0, 1)))
    bias_b = (jnp.stack(tiles, axis=1) * LOG2E).astype(F32)
    half = NUM_BUCKETS // 2
    far = jnp.stack([rel_bias[half - 1, A_HEADS:], rel_bias[NUM_BUCKETS - 1, A_HEADS:]], axis=1) * LOG2E
    return bias_a, bias_b, far.astype(F32)


def _gain_column(ga, gb, gc):
    sa = HEAD_DIM ** -0.5 * LOG2E
    sb = B_QK ** -0.5 * LOG2E
    parts = [
        jnp.tile(ga[0] * sa, A_HEADS), jnp.tile(ga[1], A_KV), jnp.ones((A_V,), F32),
        jnp.tile(gb[0] * sb, 2 * B_HEADS), jnp.tile(gb[1], 2 * B_HEADS), jnp.ones((B_V,), F32),
        jnp.tile(gc[0] * sa, C_HEADS), jnp.tile(gc[1], C_KV), jnp.ones((C_V,), F32),
    ]
    return jnp.concatenate(parts).reshape(IN_COLS, 1).astype(F32)


def _trunk(x, mod_l, prep, s_len, bsz):
    n = bsz * s_len
    d = x.shape[-1]
    x2d = x.reshape(n, d)
    t_b = min(T_B, s_len)
    cap = CAPACITY_FACTOR * n // N_EXPERTS
    cs = _rope_table(s_len)
    for l, lp in enumerate(prep["layers"]):
        mod3 = mod_l[l][:, None, :]
        qa, ka, va, qb, kb, vb, qc, kc, vc = _inproj(
            x2d, mod3, lp["g_attn"], lp["w_in_t"], lp["gcol"], cs, bsz, s_len)
        oa = _attn_a(qa, ka, va, prep["bias_a"], lp["sink"])
        ob = _attn_b(qb, kb, vb, prep["bias_b"][t_b], prep["far_b"], lp["lam"], lp["subln_col"])
        oc = _attn_c(qc, kc, vc)
        x1, h2, aff = _outproj(oa, ob, oc, lp["wo_a"], lp["wo_b"], lp["wo_c"], x2d, mod3,
                               lp["g_ffn"], lp["wr_hi"], lp["wr_lo"], s_len)
        idx3, gate3 = _routing(aff.reshape(N_EXPERTS, n // LANES, LANES), cap)
        acc = _moe(idx3, gate3, h2, jnp.zeros((n, d), F32), lp["wg"], lp["wu"], lp["wd"], cap)
        x2d = _combine(x1, acc, mod3, s_len)
    return x2d.reshape(bsz, s_len, d)


def kernel(x_prompt, x_sample, c_prompt, c_sample, w_mod, b_mod, g_attn, g_ffn, w_in, w_out,
           qk_gain_a, qk_gain_b, qk_gain_c, sink_a, lam_b, subln_b, rel_bias,
           w_router, w_gate, w_up, w_down):
    depth = w_mod.shape[0]
    bp, sp, d = x_prompt.shape
    bs, ss, _ = x_sample.shape
    rows = -(-(bp + bs) // 16) * 16
    c_all = jnp.zeros((rows, d), F32).at[:bp].set(c_prompt).at[bp:bp + bs].set(c_sample)
    mod_all = _modulation(c_all, w_mod, b_mod)
    mod_p, mod_s = mod_all[:, :bp], mod_all[:, bp:bp + bs]

    tbs = sorted({min(T_B, sp), min(T_B, ss)})
    tabs = {t: _bias_tables(rel_bias, t) for t in tbs}
    prep = {"bias_a": tabs[tbs[0]][0], "far_b": tabs[tbs[0]][2],
            "bias_b": {t: tabs[t][1] for t in tbs}, "layers": []}
    for l in range(depth):
        lam_init = 0.8 - 0.6 * math.exp(-0.3 * l)
        lp = lam_b[l].astype(F32)
        lam = jnp.exp(jnp.sum(lp[0] * lp[1])) - jnp.exp(jnp.sum(lp[2] * lp[3])) + lam_init
        wo_t = w_out[l].T.astype(BF16)
        wr_t = w_router[l].T
        wr_hi = wr_t.astype(BF16)
        prep["layers"].append({
            "g_attn": g_attn[l].reshape(1, d), "g_ffn": g_ffn[l].reshape(1, d),
            "w_in_t": w_in[l].T.astype(BF16),
            "gcol": _gain_column(qk_gain_a[l], qk_gain_b[l], qk_gain_c[l]),
            "sink": (sink_a[l] * LOG2E).astype(F32),
            "lam": lam.reshape(1).astype(F32),
            "subln_col": (subln_b[l] * (1.0 - lam_init)).reshape(HEAD_DIM, 1).astype(F32),
            "wo_a": wo_t[:, 0:A_Q], "wo_b": wo_t[:, A_Q:A_Q + B_Q], "wo_c": wo_t[:, A_Q + B_Q:],
            "wr_hi": wr_hi, "wr_lo": (wr_t - wr_hi.astype(F32)).astype(BF16),
            "wg": w_gate[l].astype(BF16), "wu": w_up[l].astype(BF16), "wd": w_down[l].astype(BF16),
        })
    y_p = _trunk(x_prompt, mod_p, prep, sp, bp)
    y_s = _trunk(x_sample, mod_s, prep, ss, bs)
    return (y_p, y_s)
```

```python
import functools
import math

import jax
import jax.numpy as jnp
from jax import lax
from jax.experimental import pallas as pl
from jax.experimental.pallas import tpu as pltpu

F32 = jnp.float32
BF16 = jnp.bfloat16

HEAD_DIM = 64
A_HEADS, A_KV = 6, 2
B_HEADS, B_QK = 4, 32
C_HEADS, C_KV = 6, 2
WINDOW = 128
A_BLK = 128
NUM_BUCKETS, MAX_DISTANCE = 32, 128
ROPE_THETA = 10000.0
GRID_W = 64
N_EXPERTS = 16
CAPACITY_FACTOR = 2
N_MOD = 6
EPS = 1e-6
NEG_BIG = -1e30
LOG2E = math.log2(math.e)

A_Q, A_K, A_V = A_HEADS * HEAD_DIM, A_KV * HEAD_DIM, A_KV * HEAD_DIM
B_Q, B_K, B_V = B_HEADS * HEAD_DIM, B_HEADS * HEAD_DIM, B_HEADS * HEAD_DIM
C_Q, C_K, C_V = C_HEADS * HEAD_DIM, C_KV * HEAD_DIM, C_KV * HEAD_DIM
A_OFF = 0
B_OFF = A_Q + A_K + A_V
C_OFF = B_OFF + B_Q + B_K + B_V
IN_COLS = C_OFF + C_Q + C_K + C_V

LANES = 128
VMEM_LIMIT_BYTES = 56 * 1024 * 1024

TM_PROJ = 512
TQ_C, TK_C = 512, 1024
T_B = 512
MOE_CH = 256
ROUTE_PC = 512
COMBINE_TT = 256
COMBINE_WR = 64


def _cparams(sem):
    return pltpu.CompilerParams(dimension_semantics=sem, vmem_limit_bytes=VMEM_LIMIT_BYTES)


def _dot(a, b):
    return jnp.dot(a, b, preferred_element_type=F32)


def _dot_nt(a, b):
    return lax.dot_general(a, b, (((1,), (1,)), ((), ())), preferred_element_type=F32)


def _mod_kernel(c_ref, w_ref, b_ref, o_ref):
    c = c_ref[...]
    a = c * (1.0 / (1.0 + jnp.exp(-c)))
    a_hi = a.astype(BF16)
    a_lo = (a - a_hi.astype(F32)).astype(BF16)
    w = w_ref[0]
    w_hi = w.astype(BF16)
    w_lo = (w - w_hi.astype(F32)).astype(BF16)
    acc = _dot(a_hi, w_hi) + _dot(a_lo, w_hi) + _dot(a_hi, w_lo)
    o_ref[0] = acc + b_ref[0]


def _modulation(c_all, w_mod, b_mod):
    depth, d, nm = w_mod.shape
    rows = c_all.shape[0]
    bn = 1024
    return pl.pallas_call(
        _mod_kernel,
        grid=(depth, nm // bn),
        in_specs=[
            pl.BlockSpec((rows, d), lambda l, j: (0, 0)),
            pl.BlockSpec((1, d, bn), lambda l, j: (l, 0, j)),
            pl.BlockSpec((1, 1, bn), lambda l, j: (l, 0, j)),
        ],
        out_specs=pl.BlockSpec((1, rows, bn), lambda l, j: (l, 0, j)),
        out_shape=jax.ShapeDtypeStruct((depth, rows, nm), F32),
        compiler_params=_cparams(("arbitrary", "arbitrary")),
        name="modulation",
    )(c_all, w_mod, b_mod.reshape(depth, 1, nm))


def _segnorm(rows, seglen):
    r, t = rows.shape
    r3 = rows.reshape(r // seglen, seglen, t)
    ms = jnp.mean(r3 * r3, axis=1, keepdims=True)
    return (r3 * lax.rsqrt(ms + EPS)).reshape(r, t)


def _rope(rows, cs):
    r, t = rows.shape
    x = rows.reshape(r // HEAD_DIM, HEAD_DIM, t)
    cr, sr, cc, sc = cs[None, 0:16], cs[None, 16:32], cs[None, 32:48], cs[None, 48:64]
    ar, br, ac, bc = x[:, 0:16], x[:, 16:32], x[:, 32:48], x[:, 48:64]
    out = jnp.concatenate(
        [ar * cr - br * sr, br * cr + ar * sr, ac * cc - bc * sc, bc * cc + ac * sc], axis=1)
    return out.reshape(r, t)


def _emit_k(k_ref, kt):
    nh = kt.shape[0] // HEAD_DIM
    for p in range(kt.shape[0] // LANES):
        tok = kt[p * LANES:(p + 1) * LANES].T
        k_ref[0, 2 * p] = tok[:, 0:HEAD_DIM].astype(k_ref.dtype)
        k_ref[0, 2 * p + 1] = tok[:, HEAD_DIM:LANES].astype(k_ref.dtype)
    del nh


def _emit_vt(v_ref, vt):
    v_ref[0] = vt.astype(v_ref.dtype)


def _inproj_kernel(x_ref, mod_ref, g_ref, w_ref, gcol_ref, cs_ref,
                   qa_ref, ka_ref, va_ref, qb_ref, kb_ref, vb_ref, qc_ref, kc_ref, vc_ref):
    x = x_ref[...]
    mod = mod_ref[0]
    d = x.shape[1]
    sh, sc = mod[:, 0:d], mod[:, d:2 * d]
    ms = jnp.mean(x * x, axis=-1, keepdims=True)
    h = x * lax.rsqrt(ms + EPS) * g_ref[...]
    h = (h * (1.0 + sc) + sh).astype(BF16)
    pt = _dot_nt(w_ref[...], h)
    gcol = gcol_ref[...]
    cs = cs_ref[...]

    def rows(off, n):
        return pt[off:off + n], gcol[off:off + n]

    q, g = rows(A_OFF, A_Q)
    qa_ref[0] = (_segnorm(q, HEAD_DIM) * g).astype(qa_ref.dtype)
    k, g = rows(A_OFF + A_Q, A_K)
    _emit_k(ka_ref, _segnorm(k, HEAD_DIM) * g)
    _emit_vt(va_ref, pt[A_OFF + A_Q + A_K:A_OFF + A_Q + A_K + A_V])
    q, g = rows(B_OFF, B_Q)
    qb_ref[0] = (_segnorm(q, B_QK) * g).astype(qb_ref.dtype)
    k, g = rows(B_OFF + B_Q, B_K)
    _emit_k(kb_ref, _segnorm(k, B_QK) * g)
    _emit_vt(vb_ref, pt[B_OFF + B_Q + B_K:B_OFF + B_Q + B_K + B_V])
    q, g = rows(C_OFF, C_Q)
    qc_ref[0] = _rope(_segnorm(q, HEAD_DIM) * g, cs).astype(qc_ref.dtype)
    k, g = rows(C_OFF + C_Q, C_K)
    _emit_k(kc_ref, _rope(_segnorm(k, HEAD_DIM) * g, cs))
    _emit_vt(vc_ref, pt[C_OFF + C_Q + C_K:C_OFF + C_Q + C_K + C_V])


def _inproj(x2d, mod3, g_attn, w_in_t, gcol, cs, bsz, s_len):
    n, d = x2d.shape
    tm = min(TM_PROJ, s_len)
    spb = s_len // tm

    def qt_spec(rows_):
        return pl.BlockSpec((1, rows_, tm), lambda i: (i // spb, 0, i % spb))

    def k_spec(nh):
        return pl.BlockSpec((1, nh, tm, HEAD_DIM), lambda i: (i // spb, 0, i % spb, 0))

    def qt_shape(rows_):
        return jax.ShapeDtypeStruct((bsz, rows_, s_len), BF16)

    def k_shape(nh):
        return jax.ShapeDtypeStruct((bsz, nh, s_len, HEAD_DIM), BF16)

    return pl.pallas_call(
        _inproj_kernel,
        grid=(n // tm,),
        in_specs=[
            pl.BlockSpec((tm, d), lambda i: (i, 0)),
            pl.BlockSpec((1, 1, N_MOD * d), lambda i: (i // spb, 0, 0)),
            pl.BlockSpec((1, d), lambda i: (0, 0)),
            pl.BlockSpec((IN_COLS, d), lambda i: (0, 0)),
            pl.BlockSpec((IN_COLS, 1), lambda i: (0, 0)),
            pl.BlockSpec((HEAD_DIM, tm), lambda i: (0, i % spb)),
        ],
        out_specs=[
            qt_spec(A_Q), k_spec(A_KV), qt_spec(A_V),
            qt_spec(B_Q), k_spec(B_HEADS), qt_spec(B_V),
            qt_spec(C_Q), k_spec(C_KV), qt_spec(C_V),
        ],
        out_shape=[
            qt_shape(A_Q), k_shape(A_KV), qt_shape(A_V),
            qt_shape(B_Q), k_shape(B_HEADS), qt_shape(B_V),
            qt_shape(C_Q), k_shape(C_KV), qt_shape(C_V),
        ],
        compiler_params=_cparams(("arbitrary",)),
        name="inproj",
    )(x2d, mod3, g_attn, w_in_t, gcol, cs)


def _pipe_init(m_ref, l_ref, alpha_ref, acc_ref, p_ref):
    m_ref[...] = jnp.full(m_ref.shape, NEG_BIG, F32)
    l_ref[...] = jnp.zeros(l_ref.shape, F32)
    alpha_ref[...] = jnp.ones(alpha_ref.shape, F32)
    acc_ref[...] = jnp.zeros(acc_ref.shape, F32)
    p_ref[...] = jnp.zeros(p_ref.shape, p_ref.dtype)


def _pipe_pv(vt, p_ref, alpha_ref, acc_ref):
    acc_ref[...] = alpha_ref[...] * acc_ref[...] + _dot(vt, p_ref[...])


def _pipe_sm(c, s_ref, bm_ref, m_ref, l_ref, alpha_ref, p_ref):
    m_old = m_ref[...]
    m_new = jnp.maximum(m_old, bm_ref[...] + c)
    alpha = jnp.exp2(m_old - m_new)
    p = jnp.exp2(s_ref[...] - (m_new - c))
    l_ref[...] = alpha * l_ref[...] + jnp.sum(p, axis=0, keepdims=True)
    p_ref[...] = p.astype(p_ref.dtype)
    m_ref[...] = m_new
    alpha_ref[...] = alpha


def _pipe_qk(kc, q_ref, s_ref, bm_ref, tile=None):
    s = _dot(kc, q_ref[...])
    if tile is not None:
        s = s + tile
    s_ref[...] = s
    bm_ref[...] = jnp.max(s, axis=0, keepdims=True)


def _pipe_scratch(tk, n):
    return [
        pltpu.VMEM((HEAD_DIM, n), BF16),
        pltpu.VMEM((tk, n), F32),
        pltpu.VMEM((tk, n), BF16),
        pltpu.VMEM((1, n), F32),
        pltpu.VMEM((1, n), F32),
        pltpu.VMEM((1, n), F32),
        pltpu.VMEM((1, n), F32),
        pltpu.VMEM((HEAD_DIM, n), F32),
    ]


def _attn_c_kernel(q_ref, k_ref, v_ref, o_ref, qs_ref, s_ref, p_ref, bm_ref, m_ref, l_ref, alpha_ref, acc_ref,
                   *, tk):
    g = C_HEADS // C_KV
    tq = q_ref.shape[2]
    nk = k_ref.shape[2] // tk
    q3 = q_ref[0]
    qs_ref[...] = jnp.concatenate([q3[h * HEAD_DIM:(h + 1) * HEAD_DIM] for h in range(g)], axis=1)
    _pipe_init(m_ref, l_ref, alpha_ref, acc_ref, p_ref)

    def kchunk(j):
        return k_ref[0, 0, pl.ds(pl.multiple_of(j * tk, tk), tk), :]

    def vchunk(j):
        return v_ref[0, :, pl.ds(pl.multiple_of(j * tk, tk), tk)]

    _pipe_qk(kchunk(0), qs_ref, s_ref, bm_ref)

    def step(j):
        _pipe_pv(vchunk(jnp.maximum(j - 1, 0)), p_ref, alpha_ref, acc_ref)
        _pipe_sm(0.0, s_ref, bm_ref, m_ref, l_ref, alpha_ref, p_ref)
        _pipe_qk(kchunk(j + 1), qs_ref, s_ref, bm_ref)

    def group(jg, carry):
        for u in range(C_UNROLL):
            step(C_UNROLL * jg + u)
        return carry

    n_full = nk - 1
    lax.fori_loop(0, n_full // C_UNROLL, group, 0)
    for j in range(n_full - n_full % C_UNROLL, n_full):
        step(j)
    if nk > 1:
        _pipe_pv(vchunk(nk - 2), p_ref, alpha_ref, acc_ref)
    _pipe_sm(0.0, s_ref, bm_ref, m_ref, l_ref, alpha_ref, p_ref)
    _pipe_pv(vchunk(nk - 1), p_ref, alpha_ref, acc_ref)
    o = acc_ref[...] / l_ref[...]
    o_ref[0] = jnp.concatenate([o[:, h * tq:(h + 1) * tq] for h in range(g)], axis=0).astype(o_ref.dtype)


def _attn_c(qt, k, vt):
    bsz, _, s_len = qt.shape
    g = C_HEADS // C_KV
    tq = min(TQ_C, s_len)
    tk = min(TK_C, s_len)
    return pl.pallas_call(
        functools.partial(_attn_c_kernel, tk=tk),
        grid=(bsz, C_KV, s_len // tq),
        in_specs=[
            pl.BlockSpec((1, g * HEAD_DIM, tq), lambda b, kv, i: (b, kv, i)),
            pl.BlockSpec((1, 1, s_len, HEAD_DIM), lambda b, kv, i: (b, kv, 0, 0)),
            pl.BlockSpec((1, HEAD_DIM, s_len), lambda b, kv, i: (b, kv, 0)),
        ],
        out_specs=pl.BlockSpec((1, g * HEAD_DIM, tq), lambda b, kv, i: (b, kv, i)),
        out_shape=jax.ShapeDtypeStruct((bsz, C_Q, s_len), BF16),
        scratch_shapes=_pipe_scratch(tk, g * tq),
        compiler_params=_cparams(("arbitrary", "arbitrary", "arbitrary")),
        name="attn_c",
    )(qt, k, vt)


B_WIN = 3
B_TILES = 5
B_UNROLL = 2
C_UNROLL = 1


def _attn_b_kernel(far_ref, lam_ref, q_ref, k_ref, v_ref, bias_ref, gcol_ref, o_ref,
                   qs_ref, s_ref, p_ref, bm_ref, m_ref, l_ref, alpha_ref, acc_ref):
    h = pl.program_id(1)
    i = pl.program_id(2)
    t = q_ref.shape[2]
    nk = k_ref.shape[2] // t
    win = min(B_WIN, nk)
    nfar = nk - win
    q = q_ref[0]
    z = jnp.zeros((B_QK, t), q.dtype)
    qs_ref[...] = jnp.concatenate(
        [jnp.concatenate([q[0:B_QK], z], axis=0), jnp.concatenate([z, q[B_QK:HEAD_DIM]], axis=0)], axis=1)
    _pipe_init(m_ref, l_ref, alpha_ref, acc_ref, p_ref)
    c_neg = far_ref[h, 0]
    c_pos = far_ref[h, 1]
    w0 = jnp.clip(i - 1, 0, nk - win)

    def kchunk(j):
        return k_ref[0, 0, pl.ds(pl.multiple_of(j * t, t), t), :]

    def vchunk(j):
        return v_ref[0, :, pl.ds(pl.multiple_of(j * t, t), t)]

    def far_chunk(f):
        return jnp.where(f < w0, f, f + win)

    def qk_win(j):
        b = bias_ref[0, j - i + B_TILES // 2]
        _pipe_qk(kchunk(j), qs_ref, s_ref, bm_ref, jnp.concatenate([b, b], axis=1))

    def sm(c):
        _pipe_sm(c, s_ref, bm_ref, m_ref, l_ref, alpha_ref, p_ref)

    def pv(j):
        _pipe_pv(vchunk(j), p_ref, alpha_ref, acc_ref)

    qk_win(w0)
    for w in range(win):
        if w > 0:
            pv(w0 + w - 1)
        sm(0.0)
        if w + 1 < win:
            qk_win(w0 + w + 1)
        elif nfar > 0:
            _pipe_qk(kchunk(far_chunk(0)), qs_ref, s_ref, bm_ref)

    def far_step(f):
        pv(jnp.where(f == 0, w0 + win - 1, far_chunk(f - 1)))
        sm(jnp.where(f < w0, c_neg, c_pos))
        _pipe_qk(kchunk(far_chunk(f + 1)), qs_ref, s_ref, bm_ref)

    if nfar > 0:
        n_full = nfar - 1

        def group(fg, carry):
            for u in range(B_UNROLL):
                far_step(B_UNROLL * fg + u)
            return carry

        lax.fori_loop(0, n_full // B_UNROLL, group, 0)
        for f in range(n_full - n_full % B_UNROLL, n_full):
            far_step(f)
        pv(far_chunk(nfar - 2) if nfar > 1 else w0 + win - 1)
        sm(jnp.where(nfar - 1 < w0, c_neg, c_pos))
        pv(far_chunk(nfar - 1))
    else:
        pv(w0 + win - 1)

    o = acc_ref[...] / l_ref[...]
    o = o[:, 0:t] - lam_ref[0] * o[:, t:2 * t]
    ms = jnp.mean(o * o, axis=0, keepdims=True)
    o_ref[0] = (o * lax.rsqrt(ms + EPS) * gcol_ref[...]).astype(o_ref.dtype)


def _attn_b(qt, k, vt, bias_tiles, far, lam, gcol):
    bsz, _, s_len = qt.shape
    t = bias_tiles.shape[-1]
    return pl.pallas_call(
        _attn_b_kernel,
        grid_spec=pltpu.PrefetchScalarGridSpec(
            num_scalar_prefetch=2,
            grid=(bsz, B_HEADS, s_len // t),
            in_specs=[
                pl.BlockSpec((1, HEAD_DIM, t), lambda b, h, i, *_: (b, h, i)),
                pl.BlockSpec((1, 1, s_len, HEAD_DIM), lambda b, h, i, *_: (b, h, 0, 0)),
                pl.BlockSpec((1, HEAD_DIM, s_len), lambda b, h, i, *_: (b, h, 0)),
                pl.BlockSpec((1, B_TILES, t, t), lambda b, h, i, *_: (h, 0, 0, 0)),
                pl.BlockSpec((HEAD_DIM, 1), lambda b, h, i, *_: (0, 0)),
            ],
            out_specs=pl.BlockSpec((1, HEAD_DIM, t), lambda b, h, i, *_: (b, h, i)),
            scratch_shapes=_pipe_scratch(t, 2 * t),
        ),
        out_shape=jax.ShapeDtypeStruct((bsz, B_Q, s_len), BF16),
        compiler_params=_cparams(("arbitrary", "arbitrary", "arbitrary")),
        name="attn_b",
    )(far, lam, qt, k, vt, bias_tiles, gcol)


def _attn_a_kernel(sink_ref, q_ref, kp_ref, kc_ref, kn_ref, vp_ref, vc_ref, vn_ref, bias_ref, o_ref):
    n = pl.program_id(1)
    nb = pl.num_programs(1)
    g = A_HEADS // A_KV
    blk = A_BLK
    row = lax.broadcasted_iota(jnp.int32, (3 * blk, 1), 0)
    valid = ((row >= blk) | (n > 0)) & ((row < 2 * blk) | (n < nb - 1))
    for kv in range(A_KV):
        q3 = q_ref[0, kv * g * HEAD_DIM:(kv + 1) * g * HEAD_DIM]
        qcat = jnp.concatenate([q3[h * HEAD_DIM:(h + 1) * HEAD_DIM] for h in range(g)], axis=1)
        kwin = jnp.concatenate([kp_ref[0, kv], kc_ref[0, kv], kn_ref[0, kv]], axis=0)
        hs = slice(kv * HEAD_DIM, (kv + 1) * HEAD_DIM)
        vwin = jnp.concatenate([vp_ref[0, hs], vc_ref[0, hs], vn_ref[0, hs]], axis=1)
        bias = jnp.concatenate([bias_ref[kv * g + h] for h in range(g)], axis=1)
        s = jnp.where(valid, _dot(kwin, qcat) + bias, NEG_BIG)
        sink = jnp.concatenate(
            [jnp.full((1, blk), sink_ref[kv * g + h], F32) for h in range(g)], axis=1)
        m = jnp.maximum(jnp.max(s, axis=0, keepdims=True), sink)
        p = jnp.exp2(s - m)
        denom = jnp.sum(p, axis=0, keepdims=True) + jnp.exp2(sink - m)
        o = _dot(vwin, p.astype(BF16)) / denom
        o_ref[0, kv * g * HEAD_DIM:(kv + 1) * g * HEAD_DIM] = jnp.concatenate(
            [o[:, h * blk:(h + 1) * blk] for h in range(g)], axis=0).astype(o_ref.dtype)


def _attn_a(qt, k, vt, bias, sink):
    bsz, _, s_len = qt.shape
    nb = s_len // A_BLK

    def kspec(off):
        return pl.BlockSpec((1, A_KV, A_BLK, HEAD_DIM),
                            lambda b, n, *_: (b, 0, jnp.clip(n + off, 0, nb - 1), 0))

    def vspec(off):
        return pl.BlockSpec((1, A_V, A_BLK), lambda b, n, *_: (b, 0, jnp.clip(n + off, 0, nb - 1)))

    return pl.pallas_call(
        _attn_a_kernel,
        grid_spec=pltpu.PrefetchScalarGridSpec(
            num_scalar_prefetch=1,
            grid=(bsz, nb),
            in_specs=[
                pl.BlockSpec((1, A_Q, A_BLK), lambda b, n, *_: (b, 0, n)),
                kspec(-1), kspec(0), kspec(1), vspec(-1), vspec(0), vspec(1),
                pl.BlockSpec((A_HEADS, 3 * A_BLK, A_BLK), lambda b, n, *_: (0, 0, 0)),
            ],
            out_specs=pl.BlockSpec((1, A_Q, A_BLK), lambda b, n, *_: (b, 0, n)),
        ),
        out_shape=jax.ShapeDtypeStruct((bsz, A_Q, s_len), BF16),
        compiler_params=_cparams(("arbitrary", "arbitrary")),
        name="attn_a",
    )(sink, qt, k, k, k, vt, vt, vt, bias)


def _outproj_kernel(oa_ref, ob_ref, oc_ref, wa_ref, wb_ref, wc_ref, x_ref, mod_ref, g_ref,
                    wrh_ref, wrl_ref, x1_ref, h2_ref, aff_ref):
    out_t = _dot(wa_ref[...], oa_ref[0]) + _dot(wb_ref[...], ob_ref[0]) + _dot(wc_ref[...], oc_ref[0])
    out = out_t.T
    mod = mod_ref[0]
    d = out.shape[1]
    gt_a = mod[:, 2 * d:3 * d]
    sh_f, sc_f = mod[:, 3 * d:4 * d], mod[:, 4 * d:5 * d]
    x1 = x_ref[...] + gt_a * out
    x1_ref[...] = x1
    ms = jnp.mean(x1 * x1, axis=-1, keepdims=True)
    h2 = x1 * lax.rsqrt(ms + EPS) * g_ref[...]
    h2 = h2 * (1.0 + sc_f) + sh_f
    h2_ref[...] = h2
    hi = h2.astype(BF16)
    lo = (h2 - hi.astype(F32)).astype(BF16)
    lt = _dot_nt(wrh_ref[...], hi) + _dot_nt(wrh_ref[...], lo) + _dot_nt(wrl_ref[...], hi)
    lt = lt - jnp.max(lt, axis=0, keepdims=True)
    e = jnp.exp(lt)
    aff_ref[...] = e / jnp.sum(e, axis=0, keepdims=True)


def _outproj(oa, ob, oc, wa, wb, wc, x2d, mod3, g_ffn, wr_hi, wr_lo, s_len):
    n, d = x2d.shape
    tm = min(TM_PROJ, s_len)
    spb = s_len // tm

    def ot_spec(rows_):
        return pl.BlockSpec((1, rows_, tm), lambda i: (i // spb, 0, i % spb))

    def full(shape):
        return pl.BlockSpec(shape, lambda i: tuple(0 for _ in shape))

    return pl.pallas_call(
        _outproj_kernel,
        grid=(n // tm,),
        in_specs=[
            ot_spec(A_Q), ot_spec(B_Q), ot_spec(C_Q),
            full((d, A_Q)), full((d, B_Q)), full((d, C_Q)),
            pl.BlockSpec((tm, d), lambda i: (i, 0)),
            pl.BlockSpec((1, 1, N_MOD * d), lambda i: (i // spb, 0, 0)),
            full((1, d)), full((N_EXPERTS, d)), full((N_EXPERTS, d)),
        ],
        out_specs=[
            pl.BlockSpec((tm, d), lambda i: (i, 0)),
            pl.BlockSpec((tm, d), lambda i: (i, 0)),
            pl.BlockSpec((N_EXPERTS, tm), lambda i: (0, i)),
        ],
        out_shape=[
            jax.ShapeDtypeStruct((n, d), F32),
            jax.ShapeDtypeStruct((n, d), F32),
            jax.ShapeDtypeStruct((N_EXPERTS, n), F32),
        ],
        compiler_params=_cparams(("arbitrary",)),
        name="outproj",
    )(oa, ob, oc, wa, wb, wc, x2d, mod3, g_ffn, wr_hi, wr_lo)


def _routing_kernel(aff_ref, idx_ref, gate_ref, pexc_ref, thr_ref, need_ref, *, cap, pc):
    ne, nblk, _ = aff_ref.shape
    bits = lax.bitcast_convert_type(aff_ref[...], jnp.int32)

    def bis(it, t):
        cand = t | jnp.left_shift(jnp.int32(1), 30 - it)
        cnt = jnp.sum((bits >= cand).astype(F32), axis=(1, 2), keepdims=True)
        return jnp.where(cnt >= cap, cand, t)

    thr = lax.fori_loop(0, 31, bis, jnp.zeros((ne, 1, 1), jnp.int32))
    n_gt = jnp.sum((bits > thr).astype(F32), axis=(1, 2), keepdims=True)
    thr_ref[...] = jnp.broadcast_to(thr, thr_ref.shape)
    need_ref[...] = jnp.broadcast_to(cap - n_gt, need_ref.shape)

    def tri(nn, fn):
        a = lax.broadcasted_iota(jnp.int32, (nn, nn), 0)
        b = lax.broadcasted_iota(jnp.int32, (nn, nn), 1)
        return jnp.where(fn(a, b), 1.0, 0.0).astype(BF16)

    ones_l = jnp.ones((LANES, LANES), BF16)
    su_l = tri(LANES, lambda a, b: a < b)
    li_l = tri(LANES, lambda a, b: b <= a)
    sl_b = tri(nblk, lambda a, b: b < a)
    li_b = tri(nblk, lambda a, b: b <= a)
    blk_col = lax.broadcasted_iota(jnp.int32, (nblk, 1), 0).astype(F32)
    lane_col = lax.broadcasted_iota(jnp.int32, (LANES, 1), 0).astype(F32)

    def per_expert(e, carry):
        a = aff_ref[e]
        be = lax.bitcast_convert_type(a, jnp.int32)
        te = thr_ref[e]
        gt = be > te
        eq = (be == te)
        eqb = jnp.where(eq, 1.0, 0.0).astype(BF16)
        rank = _dot(eqb, su_l) + _dot(sl_b, _dot(eqb, ones_l).astype(BF16))
        sel = gt | (eq & (rank < need_ref[e]))
        mb = jnp.where(sel, 1.0, 0.0).astype(BF16)
        cnt_b = _dot(mb, ones_l)
        pinc = _dot(li_b, cnt_b.astype(BF16))
        pinc_col = pinc[:, 0:1]
        pexc_col = pinc_col - cnt_b[:, 0:1]
        pexc_ref[e] = (pinc - cnt_b).T[0:1].astype(jnp.int32)
        mt = jnp.where(sel, 1.0, 0.0).T.astype(BF16)
        at = a.T
        at_hi = at.astype(BF16)
        at_lo = (at - at_hi.astype(F32)).astype(BF16)
        for c in range(cap // pc):
            p_row = (lax.broadcasted_iota(jnp.int32, (1, pc), 1) + c * pc).astype(F32)
            blk_row = jnp.sum(jnp.where(pinc_col <= p_row, 1.0, 0.0), axis=0, keepdims=True)
            oh = blk_col == blk_row
            ohb = jnp.where(oh, 1.0, 0.0).astype(BF16)
            r_row = p_row - jnp.sum(jnp.where(oh, pexc_col, 0.0), axis=0, keepdims=True)
            gt_rows = _dot(mt, ohb)
            pc_incl = _dot(li_l, gt_rows.astype(BF16))
            lane_row = jnp.sum(jnp.where(pc_incl <= r_row, 1.0, 0.0), axis=0, keepdims=True)
            idx_ref[e, :, c * pc:(c + 1) * pc] = (blk_row * LANES + lane_row).astype(jnp.int32)
            ag = _dot(at_hi, ohb) + _dot(at_lo, ohb)
            gate_ref[e, :, c * pc:(c + 1) * pc] = jnp.sum(
                jnp.where(lane_col == lane_row, ag, 0.0), axis=0, keepdims=True)
        return carry

    lax.fori_loop(0, ne, per_expert, 0)


def _routing(aff3, cap):
    ne, nblk, _ = aff3.shape
    pc = min(ROUTE_PC, cap)
    return pl.pallas_call(
        functools.partial(_routing_kernel, cap=cap, pc=pc),
        grid=(1,),
        in_specs=[pl.BlockSpec((ne, nblk, LANES), lambda i: (0, 0, 0))],
        out_specs=[
            pl.BlockSpec((ne, 1, cap), lambda i: (0, 0, 0)),
            pl.BlockSpec((ne, 1, cap), lambda i: (0, 0, 0)),
            pl.BlockSpec((ne, 1, nblk), lambda i: (0, 0, 0)),
        ],
        out_shape=[
            jax.ShapeDtypeStruct((ne, 1, cap), jnp.int32),
            jax.ShapeDtypeStruct((ne, 1, cap), F32),
            jax.ShapeDtypeStruct((ne, 1, nblk), jnp.int32),
        ],
        scratch_shapes=[pltpu.VMEM((ne, 1, LANES), jnp.int32), pltpu.VMEM((ne, 1, LANES), F32)],
        compiler_params=_cparams(("arbitrary",)),
        name="routing",
    )(aff3)


TOK_LANES = LANES
TOK_RADIX = 256


def _moe_kernel(idx_ref, idxn_ref, gate_ref, h_hbm, wg_ref, wu_ref, wd_ref, y_ref,
                idx_smem, xbuf, sems):
    ch = xbuf.shape[1]
    d = xbuf.shape[2]
    nsteps = pl.num_programs(0) * pl.num_programs(1)
    s = pl.program_id(0) * pl.num_programs(1) + pl.program_id(1)
    slot = s % 2
    nslot = 1 - slot

    def load_idx(src_ref, sl):
        cp = pltpu.make_async_copy(src_ref.at[0, 0], idx_smem.at[sl], sems.at[2])
        cp.start()
        cp.wait()

    def row_copy(sl, r, t):
        return pltpu.make_async_copy(h_hbm.at[pl.ds(t, 1)], xbuf.at[sl, pl.ds(r, 1)], sems.at[sl])

    @pl.when(s == 0)
    def _():
        load_idx(idx_ref, 0)
        for r in range(ch):
            row_copy(0, r, idx_smem[0, r]).start()

    load_idx(idxn_ref, nslot)
    for r in range(ch):
        row_copy(slot, r, 0).wait()
    for r in range(ch):
        row_copy(nslot, r, idx_smem[nslot, r]).start()

    x = xbuf[slot].astype(BF16)
    gg = _dot(x, wg_ref[0])
    uu = _dot(x, wu_ref[0])
    hid = (gg * (1.0 / (1.0 + jnp.exp(-gg))) * uu).astype(BF16)
    y = _dot(hid, wd_ref[0])
    sub = lax.broadcasted_iota(jnp.int32, (LANES, ch), 0)
    cols = jnp.where(sub == 0, gate_ref[0], jnp.where(sub == 1, idx_ref[0].astype(F32), 0.0)).T
    y_ref[:, 0:d] = (y * cols[:, 0:1]).astype(y_ref.dtype)
    tok = cols[:, 1:2]
    hi = jnp.floor(tok * (1.0 / TOK_RADIX))
    lane = lax.broadcasted_iota(jnp.int32, (ch, TOK_LANES), 1)
    y_ref[:, d:d + TOK_LANES] = jnp.where(
        lane == 0, hi, jnp.where(lane == 1, tok - hi * TOK_RADIX, 0.0)).astype(y_ref.dtype)

    @pl.when(s == nsteps - 1)
    def _():
        for r in range(ch):
            row_copy(nslot, r, 0).wait()


def _moe(idx3, gate3, h2, wg, wu, wd, cap):
    n, d = h2.shape
    ne, _, ff = wg.shape
    ch = min(MOE_CH, cap)
    nch = cap // ch
    nsteps = ne * nch
    return pl.pallas_call(
        _moe_kernel,
        grid=(ne, nch),
        in_specs=[
            pl.BlockSpec((1, 1, ch), lambda e, c: (e * nch + c, 0, 0)),
            pl.BlockSpec((1, 1, ch), lambda e, c: (jnp.minimum(e * nch + c + 1, nsteps - 1), 0, 0)),
            pl.BlockSpec((1, 1, ch), lambda e, c: (e * nch + c, 0, 0)),
            pl.BlockSpec(memory_space=pl.ANY),
            pl.BlockSpec((1, d, ff), lambda e, c: (e, 0, 0)),
            pl.BlockSpec((1, d, ff), lambda e, c: (e, 0, 0)),
            pl.BlockSpec((1, ff, d), lambda e, c: (e, 0, 0)),
        ],
        out_specs=pl.BlockSpec((ch, d + TOK_LANES), lambda e, c: (e * nch + c, 0)),
        out_shape=jax.ShapeDtypeStruct((ne * cap, d + TOK_LANES), BF16),
        scratch_shapes=[
            pltpu.SMEM((2, ch), jnp.int32),
            pltpu.VMEM((2, ch, d), F32),
            pltpu.SemaphoreType.DMA((3,)),
        ],
        compiler_params=_cparams(("arbitrary", "arbitrary")),
        name="moe_ffn",
    )(idx3.reshape(nsteps, 1, ch), idx3.reshape(nsteps, 1, ch), gate3.reshape(nsteps, 1, ch), h2, wg, wu, wd)


def _combine_kernel(starts_ref, y_hbm, x_ref, mod_ref, o_ref, ybuf, xtra, sems, *, cap, ne):
    tt, d = x_ref.shape
    wr = ybuf.shape[2]
    nt = pl.num_programs(0)
    t = pl.program_id(0)
    slot = t % 2
    nslot = 1 - slot

    def bounds(e, tile):
        lo = starts_ref[e * (nt + 1) + tile]
        hi = starts_ref[e * (nt + 1) + tile + 1]
        base = jnp.minimum((lo // 16) * 16, cap - wr)
        return lo, hi, base

    def win_copy(e, base, sl):
        return pltpu.make_async_copy(y_hbm.at[pl.ds(pl.multiple_of(e * cap + base, 16), wr)],
                                     ybuf.at[sl, e], sems.at[sl])

    def fetch(tile, sl):
        for e in range(ne):
            win_copy(e, bounds(e, tile)[2], sl).start()

    @pl.when(t == 0)
    def _():
        fetch(0, 0)

    for e in range(ne):
        win_copy(e, 0, slot).wait()
    fetch(jnp.minimum(t + 1, nt - 1), nslot)

    tok0 = (t * tt).astype(F32)
    lane_tok = lax.broadcasted_iota(jnp.int32, (1, tt), 1).astype(F32) + tok0
    row = lax.broadcasted_iota(jnp.int32, (wr, 1), 0)

    def contribution(yb, first, lo_eff, hi):
        tokc = yb[:, d:d + 1].astype(F32) * TOK_RADIX + yb[:, d + 1:d + 2].astype(F32)
        pos = row + first
        oh = jnp.where((pos >= lo_eff) & (pos < hi) & (tokc == lane_tok), 1.0, 0.0).astype(BF16)
        return lax.dot_general(oh, yb[:, 0:d], (((0,), (0,)), ((), ())), preferred_element_type=F32)

    acc = jnp.zeros((tt, d), F32)
    for e in range(ne):
        lo, hi, base = bounds(e, t)
        acc = acc + contribution(ybuf[slot, e], base, lo, hi)
    o_ref[...] = x_ref[...] + mod_ref[0][:, 5 * d:6 * d] * acc

    for e in range(ne):
        lo, hi, base = bounds(e, t)
        n_more = jnp.maximum(hi - (base + wr) + wr - 1, 0) // wr

        def more(k, carry, e=e, hi=hi, base=base):
            done = base + wr * (k + 1)
            first = jnp.minimum(done, cap - wr)
            cp = pltpu.make_async_copy(y_hbm.at[pl.ds(pl.multiple_of(e * cap + first, 16), wr)], xtra, sems.at[2])
            cp.start()
            cp.wait()
            o_ref[...] = o_ref[...] + mod_ref[0][:, 5 * d:6 * d] * contribution(xtra[...], first, done, hi)
            return carry

        lax.fori_loop(0, n_more, more, 0)

    @pl.when(t == nt - 1)
    def _():
        for e in range(ne):
            win_copy(e, 0, nslot).wait()


def _combine(starts, y, x1, mod3, s_len, cap):
    n, d = x1.shape
    ne = y.shape[0] // cap
    tt = min(COMBINE_TT, s_len)
    spb = s_len // tt
    wr = min(COMBINE_WR, cap)
    return pl.pallas_call(
        functools.partial(_combine_kernel, cap=cap, ne=ne),
        grid_spec=pltpu.PrefetchScalarGridSpec(
            num_scalar_prefetch=1,
            grid=(n // tt,),
            in_specs=[
                pl.BlockSpec(memory_space=pl.ANY),
                pl.BlockSpec((tt, d), lambda i, *_: (i, 0)),
                pl.BlockSpec((1, 1, N_MOD * d), lambda i, *_: (i // spb, 0, 0)),
            ],
            out_specs=pl.BlockSpec((tt, d), lambda i, *_: (i, 0)),
            scratch_shapes=[
                pltpu.VMEM((2, ne, wr, d + TOK_LANES), BF16),
                pltpu.VMEM((wr, d + TOK_LANES), BF16),
                pltpu.SemaphoreType.DMA((3,)),
            ],
        ),
        out_shape=jax.ShapeDtypeStruct((n, d), F32),
        compiler_params=_cparams(("arbitrary",)),
        name="combine",
    )(starts, y, x1, mod3)


def _t5_bucket(rel):
    half = NUM_BUCKETS // 2
    max_exact = half // 2
    n = jnp.abs(rel)
    nf = jnp.maximum(n, 1).astype(F32)
    large = max_exact + (jnp.log(nf / max_exact) / math.log(MAX_DISTANCE / max_exact)
                         * (half - max_exact)).astype(jnp.int32)
    large = jnp.minimum(large, half - 1)
    return jnp.where(rel > 0, half, 0) + jnp.where(n < max_exact, n, large)


def _rope_table(s_len):
    quarter = HEAD_DIM // 4
    freqs = ROPE_THETA ** (-jnp.arange(quarter, dtype=F32) / quarter)
    t = jnp.arange(s_len)
    ang_r = freqs[:, None] * (t // GRID_W).astype(F32)[None, :]
    ang_c = freqs[:, None] * (t % GRID_W).astype(F32)[None, :]
    return jnp.concatenate([jnp.cos(ang_r), jnp.sin(ang_r), jnp.cos(ang_c), jnp.sin(ang_c)], axis=0)


def _toeplitz(fn, nj, ni):
    period = nj + ni
    m = jnp.arange(period)
    b = fn(jnp.where(m < nj, m, m - period))
    flat = jnp.tile(b, ni)[..., :ni * (period - 1)]
    rows = flat.reshape(b.shape[:-1] + (ni, period - 1))
    return jnp.swapaxes(rows[..., :nj], -1, -2)


def _bias_tables(rel_bias, t_b):
    def bias_of(rel, heads):
        return rel_bias[_t5_bucket(rel)][:, heads].T * LOG2E

    def fn_a(r):
        rel = r - A_BLK
        return jnp.where(jnp.abs(rel) <= WINDOW, bias_of(rel, slice(0, A_HEADS)), NEG_BIG)

    bias_a = _toeplitz(fn_a, 3 * A_BLK, A_BLK).astype(F32)
    tiles = [
        _toeplitz(lambda r, dd=dd: bias_of(dd * t_b + r, slice(A_HEADS, None)), t_b, t_b)
        for dd in range(-(B_TILES // 2), B_TILES // 2 + 1)
    ]
    bias_b = jnp.stack(tiles, axis=1).astype(F32)
    half = NUM_BUCKETS // 2
    far = jnp.stack([rel_bias[half - 1, A_HEADS:], rel_bias[NUM_BUCKETS - 1, A_HEADS:]], axis=1) * LOG2E
    return bias_a, bias_b, far.astype(F32)


def _gain_column(ga, gb, gc):
    sa = HEAD_DIM ** -0.5 * LOG2E
    sb = B_QK ** -0.5 * LOG2E
    parts = [
        jnp.tile(ga[0] * sa, A_HEADS), jnp.tile(ga[1], A_KV), jnp.ones((A_V,), F32),
        jnp.tile(gb[0] * sb, 2 * B_HEADS), jnp.tile(gb[1], 2 * B_HEADS), jnp.ones((B_V,), F32),
        jnp.tile(gc[0] * sa, C_HEADS), jnp.tile(gc[1], C_KV), jnp.ones((C_V,), F32),
    ]
    return jnp.concatenate(parts).reshape(IN_COLS, 1).astype(F32)


def _trunk(x, mod_l, prep, s_len, bsz):
    n = bsz * s_len
    d = x.shape[-1]
    x2d = x.reshape(n, d)
    t_b = min(T_B, s_len)
    cap = CAPACITY_FACTOR * n // N_EXPERTS
    cs = _rope_table(s_len)
    for l, lp in enumerate(prep["layers"]):
        mod3 = mod_l[l][:, None, :]
        qa, ka, va, qb, kb, vb, qc, kc, vc = _inproj(
            x2d, mod3, lp["g_attn"], lp["w_in_t"], lp["gcol"], cs, bsz, s_len)
        oa = _attn_a(qa, ka, va, prep["bias_a"], lp["sink"])
        ob = _attn_b(qb, kb, vb, prep["bias_b"][t_b], prep["far_b"], lp["lam"], lp["subln_col"])
        oc = _attn_c(qc, kc, vc)
        x1, h2, aff = _outproj(oa, ob, oc, lp["wo_a"], lp["wo_b"], lp["wo_c"], x2d, mod3,
                               lp["g_ffn"], lp["wr_hi"], lp["wr_lo"], s_len)
        idx3, gate3, pexc = _routing(aff.reshape(N_EXPERTS, n // LANES, LANES), cap)
        y = _moe(idx3, gate3, h2, lp["wg"], lp["wu"], lp["wd"], cap)
        tt = min(COMBINE_TT, s_len)
        starts = jnp.concatenate(
            [pexc[:, 0, ::tt // LANES], jnp.full((N_EXPERTS, 1), cap, jnp.int32)], axis=1).reshape(-1)
        x2d = _combine(starts, y, x1, mod3, s_len, cap)
    return x2d.reshape(bsz, s_len, d)


def kernel(x_prompt, x_sample, c_prompt, c_sample, w_mod, b_mod, g_attn, g_ffn, w_in, w_out,
           qk_gain_a, qk_gain_b, qk_gain_c, sink_a, lam_b, subln_b, rel_bias,
           w_router, w_gate, w_up, w_down):
    depth = w_mod.shape[0]
    bp, sp, d = x_prompt.shape
    bs, ss, _ = x_sample.shape
    rows = -(-(bp + bs) // 16) * 16
    c_all = jnp.zeros((rows, d), F32).at[:bp].set(c_prompt).at[bp:bp + bs].set(c_sample)
    mod_all = _modulation(c_all, w_mod, b_mod)
    mod_p, mod_s = mod_all[:, :bp], mod_all[:, bp:bp + bs]

    tbs = sorted({min(T_B, sp), min(T_B, ss)})
    tabs = {t: _bias_tables(rel_bias, t) for t in tbs}
    prep = {"bias_a": tabs[tbs[0]][0], "far_b": tabs[tbs[0]][2],
            "bias_b": {t: tabs[t][1] for t in tbs}, "layers": []}
    for l in range(depth):
        lam_init = 0.8 - 0.6 * math.exp(-0.3 * l)
        lp = lam_b[l].astype(F32)
        lam = jnp.exp(jnp.sum(lp[0] * lp[1])) - jnp.exp(jnp.sum(lp[2] * lp[3])) + lam_init
        wo_t = w_out[l].T.astype(BF16)
        wr_t = w_router[l].T
        wr_hi = wr_t.astype(BF16)
        prep["layers"].append({
            "g_attn": g_attn[l].reshape(1, d), "g_ffn": g_ffn[l].reshape(1, d),
            "w_in_t": w_in[l].T.astype(BF16),
            "gcol": _gain_column(qk_gain_a[l], qk_gain_b[l], qk_gain_c[l]),
            "sink": (sink_a[l] * LOG2E).astype(F32),
            "lam": lam.reshape(1).astype(F32),
            "subln_col": (subln_b[l] * (1.0 - lam_init)).reshape(HEAD_DIM, 1).astype(F32),
            "wo_a": wo_t[:, 0:A_Q], "wo_b": wo_t[:, A_Q:A_Q + B_Q], "wo_c": wo_t[:, A_Q + B_Q:],
            "wr_hi": wr_hi, "wr_lo": (wr_t - wr_hi.astype(F32)).astype(BF16),
            "wg": w_gate[l].astype(BF16), "wu": w_up[l].astype(BF16), "wd": w_down[l].astype(BF16),
        })
    y_p = _trunk(x_prompt, mod_p, prep, sp, bp)
    y_s = _trunk(x_sample, mod_s, prep, ss, bs)
    return (y_p, y_s)
```

```python
import functools
import math

import jax
import jax.numpy as jnp
from jax import lax
from jax.experimental import pallas as pl
from jax.experimental.pallas import tpu as pltpu

F32 = jnp.float32
BF16 = jnp.bfloat16

HEAD_DIM = 64
A_HEADS, A_KV = 6, 2
B_HEADS, B_QK = 4, 32
C_HEADS, C_KV = 6, 2
WINDOW = 128
A_BLK = 128
NUM_BUCKETS, MAX_DISTANCE = 32, 128
ROPE_THETA = 10000.0
GRID_W = 64
N_EXPERTS = 16
CAPACITY_FACTOR = 2
N_MOD = 6
EPS = 1e-6
NEG_BIG = -1e30
LOG2E = math.log2(math.e)

A_Q, A_K, A_V = A_HEADS * HEAD_DIM, A_KV * HEAD_DIM, A_KV * HEAD_DIM
B_Q, B_K, B_V = B_HEADS * HEAD_DIM, B_HEADS * HEAD_DIM, B_HEADS * HEAD_DIM
C_Q, C_K, C_V = C_HEADS * HEAD_DIM, C_KV * HEAD_DIM, C_KV * HEAD_DIM
A_OFF = 0
B_OFF = A_Q + A_K + A_V
C_OFF = B_OFF + B_Q + B_K + B_V
IN_COLS = C_OFF + C_Q + C_K + C_V

LANES = 128
K_LANES = LANES
VMEM_LIMIT_BYTES = 56 * 1024 * 1024

TM_PROJ = 512
TQ_C, TK_C = 512, 1024
T_B = 512
MOE_CH = 256
ROUTE_PC = 512
COMBINE_TT = 256
COMBINE_WR = 64


def _cparams(sem):
    return pltpu.CompilerParams(dimension_semantics=sem, vmem_limit_bytes=VMEM_LIMIT_BYTES)


def _dot(a, b):
    return jnp.dot(a, b, preferred_element_type=F32)


def _dot_nt(a, b):
    return lax.dot_general(a, b, (((1,), (1,)), ((), ())), preferred_element_type=F32)


def _mod_kernel(c_ref, w_ref, b_ref, o_ref):
    c = c_ref[...]
    a = c * (1.0 / (1.0 + jnp.exp(-c)))
    a_hi = a.astype(BF16)
    a_lo = (a - a_hi.astype(F32)).astype(BF16)
    w = w_ref[0]
    w_hi = w.astype(BF16)
    w_lo = (w - w_hi.astype(F32)).astype(BF16)
    acc = _dot(a_hi, w_hi) + _dot(a_lo, w_hi) + _dot(a_hi, w_lo)
    o_ref[0] = acc + b_ref[0]


def _modulation(c_all, w_mod, b_mod):
    depth, d, nm = w_mod.shape
    rows = c_all.shape[0]
    bn = 1024
    return pl.pallas_call(
        _mod_kernel,
        grid=(depth, nm // bn),
        in_specs=[
            pl.BlockSpec((rows, d), lambda l, j: (0, 0)),
            pl.BlockSpec((1, d, bn), lambda l, j: (l, 0, j)),
            pl.BlockSpec((1, 1, bn), lambda l, j: (l, 0, j)),
        ],
        out_specs=pl.BlockSpec((1, rows, bn), lambda l, j: (l, 0, j)),
        out_shape=jax.ShapeDtypeStruct((depth, rows, nm), F32),
        compiler_params=_cparams(("arbitrary", "arbitrary")),
        name="modulation",
    )(c_all, w_mod, b_mod.reshape(depth, 1, nm))


def _segnorm(rows, seglen):
    r, t = rows.shape
    r3 = rows.reshape(r // seglen, seglen, t)
    ms = jnp.mean(r3 * r3, axis=1, keepdims=True)
    return (r3 * lax.rsqrt(ms + EPS)).reshape(r, t)


def _rope(rows, cs):
    r, t = rows.shape
    x = rows.reshape(r // HEAD_DIM, HEAD_DIM, t)
    cr, sr, cc, sc = cs[None, 0:16], cs[None, 16:32], cs[None, 32:48], cs[None, 48:64]
    ar, br, ac, bc = x[:, 0:16], x[:, 16:32], x[:, 32:48], x[:, 48:64]
    out = jnp.concatenate(
        [ar * cr - br * sr, br * cr + ar * sr, ac * cc - bc * sc, bc * cc + ac * sc], axis=1)
    return out.reshape(r, t)


def _emit_k(k_ref, kt):
    lane = lax.broadcasted_iota(jnp.int32, (1, K_LANES), 1)
    pad = jnp.where(lane == HEAD_DIM, 1.0, 0.0)
    for p in range(kt.shape[0] // LANES):
        tok = kt[p * LANES:(p + 1) * LANES].T
        k_ref[0, 2 * p] = jnp.where(lane < HEAD_DIM, tok, pad).astype(k_ref.dtype)
        k_ref[0, 2 * p + 1] = jnp.where(lane < HEAD_DIM, pltpu.roll(tok, HEAD_DIM, 1), pad).astype(k_ref.dtype)


def _emit_vt(v_ref, vt):
    v_ref[0] = vt.astype(v_ref.dtype)


def _inproj_kernel(x_ref, mod_ref, g_ref, w_ref, gcol_ref, cs_ref,
                   qa_ref, ka_ref, va_ref, qb_ref, kb_ref, vb_ref, qc_ref, kc_ref, vc_ref):
    x = x_ref[...]
    mod = mod_ref[0]
    d = x.shape[1]
    sh, sc = mod[:, 0:d], mod[:, d:2 * d]
    ms = jnp.mean(x * x, axis=-1, keepdims=True)
    h = x * lax.rsqrt(ms + EPS) * g_ref[...]
    h = (h * (1.0 + sc) + sh).astype(BF16)
    pt = _dot_nt(w_ref[...], h)
    gcol = gcol_ref[...]
    cs = cs_ref[...]

    def rows(off, n):
        return pt[off:off + n], gcol[off:off + n]

    q, g = rows(A_OFF, A_Q)
    qa_ref[0] = (_segnorm(q, HEAD_DIM) * g).astype(qa_ref.dtype)
    k, g = rows(A_OFF + A_Q, A_K)
    _emit_k(ka_ref, _segnorm(k, HEAD_DIM) * g)
    _emit_vt(va_ref, pt[A_OFF + A_Q + A_K:A_OFF + A_Q + A_K + A_V])
    q, g = rows(B_OFF, B_Q)
    qb_ref[0] = (_segnorm(q, B_QK) * g).astype(qb_ref.dtype)
    k, g = rows(B_OFF + B_Q, B_K)
    _emit_k(kb_ref, _segnorm(k, B_QK) * g)
    _emit_vt(vb_ref, pt[B_OFF + B_Q + B_K:B_OFF + B_Q + B_K + B_V])
    q, g = rows(C_OFF, C_Q)
    qc_ref[0] = _rope(_segnorm(q, HEAD_DIM) * g, cs).astype(qc_ref.dtype)
    k, g = rows(C_OFF + C_Q, C_K)
    _emit_k(kc_ref, _rope(_segnorm(k, HEAD_DIM) * g, cs))
    _emit_vt(vc_ref, pt[C_OFF + C_Q + C_K:C_OFF + C_Q + C_K + C_V])


def _inproj(x2d, mod3, g_attn, w_in_t, gcol, cs, bsz, s_len):
    n, d = x2d.shape
    tm = min(TM_PROJ, s_len)
    spb = s_len // tm

    def qt_spec(rows_):
        return pl.BlockSpec((1, rows_, tm), lambda i: (i // spb, 0, i % spb))

    def k_spec(nh):
        return pl.BlockSpec((1, nh, tm, K_LANES), lambda i: (i // spb, 0, i % spb, 0))

    def qt_shape(rows_):
        return jax.ShapeDtypeStruct((bsz, rows_, s_len), BF16)

    def k_shape(nh):
        return jax.ShapeDtypeStruct((bsz, nh, s_len, K_LANES), BF16)

    return pl.pallas_call(
        _inproj_kernel,
        grid=(n // tm,),
        in_specs=[
            pl.BlockSpec((tm, d), lambda i: (i, 0)),
            pl.BlockSpec((1, 1, N_MOD * d), lambda i: (i // spb, 0, 0)),
            pl.BlockSpec((1, d), lambda i: (0, 0)),
            pl.BlockSpec((IN_COLS, d), lambda i: (0, 0)),
            pl.BlockSpec((IN_COLS, 1), lambda i: (0, 0)),
            pl.BlockSpec((HEAD_DIM, tm), lambda i: (0, i % spb)),
        ],
        out_specs=[
            qt_spec(A_Q), k_spec(A_KV), qt_spec(A_V),
            qt_spec(B_Q), k_spec(B_HEADS), qt_spec(B_V),
            qt_spec(C_Q), k_spec(C_KV), qt_spec(C_V),
        ],
        out_shape=[
            qt_shape(A_Q), k_shape(A_KV), qt_shape(A_V),
            qt_shape(B_Q), k_shape(B_HEADS), qt_shape(B_V),
            qt_shape(C_Q), k_shape(C_KV), qt_shape(C_V),
        ],
        compiler_params=_cparams(("arbitrary",)),
        name="inproj",
    )(x2d, mod3, g_attn, w_in_t, gcol, cs)


def _pipe_init(m_ref, l_ref, alpha_ref, acc_ref, p_ref):
    m_ref[...] = jnp.full(m_ref.shape, NEG_BIG, F32)
    l_ref[...] = jnp.zeros(l_ref.shape, F32)
    alpha_ref[...] = jnp.ones(alpha_ref.shape, F32)
    acc_ref[...] = jnp.zeros(acc_ref.shape, F32)
    p_ref[...] = jnp.zeros(p_ref.shape, p_ref.dtype)


def _pipe_pv(vt, p_ref, alpha_ref, acc_ref):
    acc_ref[...] = alpha_ref[...] * acc_ref[...] + _dot(vt, p_ref[...])


def _pipe_sm(c, s_ref, bm_ref, m_ref, l_ref, alpha_ref, p_ref):
    m_old = m_ref[...]
    m_new = jnp.maximum(m_old, bm_ref[...] + c)
    alpha = jnp.exp2(m_old - m_new)
    p = jnp.exp2(s_ref[...] - (m_new - c))
    l_ref[...] = alpha * l_ref[...] + jnp.sum(p, axis=0, keepdims=True)
    p_ref[...] = p.astype(p_ref.dtype)
    m_ref[...] = m_new
    alpha_ref[...] = alpha


def _pipe_qk(kc, q_ref, s_ref, bm_ref, tile=None):
    s = _dot(kc, q_ref[...])
    if tile is not None:
        s = s + tile
    s_ref[...] = s
    bm_ref[...] = jnp.max(s, axis=0, keepdims=True)


def _pipe_scratch(tk, n):
    return [
        pltpu.VMEM((K_LANES, n), BF16),
        pltpu.VMEM((tk, n), F32),
        pltpu.VMEM((tk, n), BF16),
        pltpu.VMEM((1, n), F32),
        pltpu.VMEM((1, n), F32),
        pltpu.VMEM((1, n), F32),
        pltpu.VMEM((1, n), F32),
        pltpu.VMEM((HEAD_DIM, n), F32),
    ]


def _attn_c_kernel(q_ref, k_ref, v_ref, o_ref, qs_ref, s_ref, p_ref, bm_ref, m_ref, l_ref, alpha_ref, acc_ref,
                   *, tk):
    g = C_HEADS // C_KV
    tq = q_ref.shape[2]
    nk = k_ref.shape[2] // tk
    q3 = q_ref[0]
    qs_ref[0:HEAD_DIM] = jnp.concatenate([q3[h * HEAD_DIM:(h + 1) * HEAD_DIM] for h in range(g)], axis=1)
    qs_ref[HEAD_DIM:K_LANES] = jnp.zeros((K_LANES - HEAD_DIM, g * tq), qs_ref.dtype)
    _pipe_init(m_ref, l_ref, alpha_ref, acc_ref, p_ref)

    def kchunk(j):
        return k_ref[0, 0, pl.ds(pl.multiple_of(j * tk, tk), tk), :]

    def vchunk(j):
        return v_ref[0, :, pl.ds(pl.multiple_of(j * tk, tk), tk)]

    _pipe_qk(kchunk(0), qs_ref, s_ref, bm_ref)

    def step(j):
        _pipe_pv(vchunk(jnp.maximum(j - 1, 0)), p_ref, alpha_ref, acc_ref)
        _pipe_sm(0.0, s_ref, bm_ref, m_ref, l_ref, alpha_ref, p_ref)
        _pipe_qk(kchunk(j + 1), qs_ref, s_ref, bm_ref)

    def group(jg, carry):
        for u in range(C_UNROLL):
            step(C_UNROLL * jg + u)
        return carry

    n_full = nk - 1
    lax.fori_loop(0, n_full // C_UNROLL, group, 0)
    for j in range(n_full - n_full % C_UNROLL, n_full):
        step(j)
    if nk > 1:
        _pipe_pv(vchunk(nk - 2), p_ref, alpha_ref, acc_ref)
    _pipe_sm(0.0, s_ref, bm_ref, m_ref, l_ref, alpha_ref, p_ref)
    _pipe_pv(vchunk(nk - 1), p_ref, alpha_ref, acc_ref)
    o = acc_ref[...] / l_ref[...]
    o_ref[0] = jnp.concatenate([o[:, h * tq:(h + 1) * tq] for h in range(g)], axis=0).astype(o_ref.dtype)


def _attn_c(qt, k, vt):
    bsz, _, s_len = qt.shape
    g = C_HEADS // C_KV
    tq = min(TQ_C, s_len)
    tk = min(TK_C, s_len)
    return pl.pallas_call(
        functools.partial(_attn_c_kernel, tk=tk),
        grid=(bsz, C_KV, s_len // tq),
        in_specs=[
            pl.BlockSpec((1, g * HEAD_DIM, tq), lambda b, kv, i: (b, kv, i)),
            pl.BlockSpec((1, 1, s_len, K_LANES), lambda b, kv, i: (b, kv, 0, 0)),
            pl.BlockSpec((1, HEAD_DIM, s_len), lambda b, kv, i: (b, kv, 0)),
        ],
        out_specs=pl.BlockSpec((1, g * HEAD_DIM, tq), lambda b, kv, i: (b, kv, i)),
        out_shape=jax.ShapeDtypeStruct((bsz, C_Q, s_len), BF16),
        scratch_shapes=_pipe_scratch(tk, g * tq),
        compiler_params=_cparams(("arbitrary", "arbitrary", "arbitrary")),
        name="attn_c",
    )(qt, k, vt)


B_WIN = 3
B_TILES = 5
B_UNROLL = 2
C_UNROLL = 1


def _attn_b_kernel(far_ref, lam_ref, q_ref, k_ref, v_ref, bias_ref, gcol_ref, o_ref,
                   qs_ref, s_ref, p_ref, bm_ref, m_ref, l_ref, alpha_ref, acc_ref):
    h = pl.program_id(1)
    i = pl.program_id(2)
    t = q_ref.shape[2]
    nk = k_ref.shape[2] // t
    win = min(B_WIN, nk)
    nfar = nk - win
    q = q_ref[0]
    z = jnp.zeros((B_QK, t), q.dtype)
    qs_ref[0:HEAD_DIM] = jnp.concatenate(
        [jnp.concatenate([q[0:B_QK], z], axis=0), jnp.concatenate([z, q[B_QK:HEAD_DIM]], axis=0)], axis=1)
    qs_ref[HEAD_DIM:K_LANES] = jnp.zeros((K_LANES - HEAD_DIM, 2 * t), qs_ref.dtype)
    _pipe_init(m_ref, l_ref, alpha_ref, acc_ref, p_ref)
    c_neg = far_ref[h, 0]
    c_pos = far_ref[h, 1]
    w0 = jnp.clip(i - 1, 0, nk - win)

    def kchunk(j):
        return k_ref[0, 0, pl.ds(pl.multiple_of(j * t, t), t), :]

    def vchunk(j):
        return v_ref[0, :, pl.ds(pl.multiple_of(j * t, t), t)]

    def far_chunk(f):
        return jnp.where(f < w0, f, f + win)

    def qk_win(j):
        b = bias_ref[0, j - i + B_TILES // 2]
        _pipe_qk(kchunk(j), qs_ref, s_ref, bm_ref, jnp.concatenate([b, b], axis=1))

    def sm(c):
        _pipe_sm(c, s_ref, bm_ref, m_ref, l_ref, alpha_ref, p_ref)

    def pv(j):
        _pipe_pv(vchunk(j), p_ref, alpha_ref, acc_ref)

    qk_win(w0)
    for w in range(win):
        if w > 0:
            pv(w0 + w - 1)
        sm(0.0)
        if w + 1 < win:
            qk_win(w0 + w + 1)
        elif nfar > 0:
            _pipe_qk(kchunk(far_chunk(0)), qs_ref, s_ref, bm_ref)

    def far_step(f):
        pv(jnp.where(f == 0, w0 + win - 1, far_chunk(f - 1)))
        sm(jnp.where(f < w0, c_neg, c_pos))
        _pipe_qk(kchunk(far_chunk(f + 1)), qs_ref, s_ref, bm_ref)

    if nfar > 0:
        n_full = nfar - 1

        def group(fg, carry):
            for u in range(B_UNROLL):
                far_step(B_UNROLL * fg + u)
            return carry

        lax.fori_loop(0, n_full // B_UNROLL, group, 0)
        for f in range(n_full - n_full % B_UNROLL, n_full):
            far_step(f)
        pv(far_chunk(nfar - 2) if nfar > 1 else w0 + win - 1)
        sm(jnp.where(nfar - 1 < w0, c_neg, c_pos))
        pv(far_chunk(nfar - 1))
    else:
        pv(w0 + win - 1)

    o = acc_ref[...] / l_ref[...]
    o = o[:, 0:t] - lam_ref[0] * o[:, t:2 * t]
    ms = jnp.mean(o * o, axis=0, keepdims=True)
    o_ref[0] = (o * lax.rsqrt(ms + EPS) * gcol_ref[...]).astype(o_ref.dtype)


def _attn_b(qt, k, vt, bias_tiles, far, lam, gcol):
    bsz, _, s_len = qt.shape
    t = bias_tiles.shape[-1]
    return pl.pallas_call(
        _attn_b_kernel,
        grid_spec=pltpu.PrefetchScalarGridSpec(
            num_scalar_prefetch=2,
            grid=(bsz, B_HEADS, s_len // t),
            in_specs=[
                pl.BlockSpec((1, HEAD_DIM, t), lambda b, h, i, *_: (b, h, i)),
                pl.BlockSpec((1, 1, s_len, K_LANES), lambda b, h, i, *_: (b, h, 0, 0)),
                pl.BlockSpec((1, HEAD_DIM, s_len), lambda b, h, i, *_: (b, h, 0)),
                pl.BlockSpec((1, B_TILES, t, t), lambda b, h, i, *_: (h, 0, 0, 0)),
                pl.BlockSpec((HEAD_DIM, 1), lambda b, h, i, *_: (0, 0)),
            ],
            out_specs=pl.BlockSpec((1, HEAD_DIM, t), lambda b, h, i, *_: (b, h, i)),
            scratch_shapes=_pipe_scratch(t, 2 * t),
        ),
        out_shape=jax.ShapeDtypeStruct((bsz, B_Q, s_len), BF16),
        compiler_params=_cparams(("arbitrary", "arbitrary", "arbitrary")),
        name="attn_b",
    )(far, lam, qt, k, vt, bias_tiles, gcol)


SUM_ROWS = 16
SHIFT_LIMIT = 100.0
FAST_C_UNROLL = 3
FAST_B_UNROLL = 7


def _fast_qk_exp(kc, qaug_ref, p_ref, tile=None):
    s = _dot(kc, qaug_ref[...])
    if tile is not None:
        s = s + tile
    p_ref[...] = jnp.exp2(s).astype(p_ref.dtype)


def _fast_pv(vc, p_ref, acc_ref, cls):
    va = jnp.concatenate([vc, jnp.ones((SUM_ROWS, vc.shape[1]), vc.dtype)], axis=0)
    acc_ref[cls] = acc_ref[cls] + _dot(va, p_ref[...])


def _fast_scratch(tk, n, nacc):
    return [
        pltpu.VMEM((K_LANES, n), BF16),
        pltpu.VMEM((tk, n), BF16),
        pltpu.VMEM((nacc, HEAD_DIM + SUM_ROWS, n), F32),
    ]


def _shift_rows(shift, n):
    row = lax.broadcasted_iota(jnp.int32, (K_LANES - HEAD_DIM, n), 0)
    return jnp.where(row == 0, -shift, 0.0).astype(BF16)


def _attn_c_fast_kernel(shift_ref, q_ref, k_ref, v_ref, o_ref, qaug_ref, p_ref, acc_ref, *, tk):
    g = C_HEADS // C_KV
    tq = q_ref.shape[2]
    nk = k_ref.shape[2] // tk
    q3 = q_ref[0]
    qaug_ref[0:HEAD_DIM] = jnp.concatenate([q3[h * HEAD_DIM:(h + 1) * HEAD_DIM] for h in range(g)], axis=1)
    qaug_ref[HEAD_DIM:K_LANES] = _shift_rows(shift_ref[0], g * tq)
    acc_ref[...] = jnp.zeros(acc_ref.shape, F32)

    def kchunk(j):
        return k_ref[0, 0, pl.ds(pl.multiple_of(j * tk, tk), tk), :]

    def vchunk(j):
        return v_ref[0, :, pl.ds(pl.multiple_of(j * tk, tk), tk)]

    _fast_qk_exp(kchunk(0), qaug_ref, p_ref)

    def step(j):
        _fast_pv(vchunk(j - 1), p_ref, acc_ref, 0)
        _fast_qk_exp(kchunk(j), qaug_ref, p_ref)

    def group(jg, carry):
        for u in range(FAST_C_UNROLL):
            step(1 + FAST_C_UNROLL * jg + u)
        return carry

    n_full = nk - 1
    lax.fori_loop(0, n_full // FAST_C_UNROLL, group, 0)
    for j in range(1 + n_full - n_full % FAST_C_UNROLL, nk):
        step(j)
    _fast_pv(vchunk(nk - 1), p_ref, acc_ref, 0)
    acc = acc_ref[0]
    o = acc[0:HEAD_DIM] / acc[HEAD_DIM:HEAD_DIM + 1]
    o_ref[0] = jnp.concatenate([o[:, h * tq:(h + 1) * tq] for h in range(g)], axis=0).astype(o_ref.dtype)


def _attn_c_fast(qt, k, vt, shift):
    bsz, _, s_len = qt.shape
    g = C_HEADS // C_KV
    tq = min(TQ_C, s_len)
    tk = min(TK_C, s_len)
    return pl.pallas_call(
        functools.partial(_attn_c_fast_kernel, tk=tk),
        grid_spec=pltpu.PrefetchScalarGridSpec(
            num_scalar_prefetch=1,
            grid=(bsz, C_KV, s_len // tq),
            in_specs=[
                pl.BlockSpec((1, g * HEAD_DIM, tq), lambda b, kv, i, *_: (b, kv, i)),
                pl.BlockSpec((1, 1, s_len, K_LANES), lambda b, kv, i, *_: (b, kv, 0, 0)),
                pl.BlockSpec((1, HEAD_DIM, s_len), lambda b, kv, i, *_: (b, kv, 0)),
            ],
            out_specs=pl.BlockSpec((1, g * HEAD_DIM, tq), lambda b, kv, i, *_: (b, kv, i)),
            scratch_shapes=_fast_scratch(tk, g * tq, 1),
        ),
        out_shape=jax.ShapeDtypeStruct((bsz, C_Q, s_len), BF16),
        compiler_params=_cparams(("arbitrary", "arbitrary", "arbitrary")),
        name="attn_c_fast",
    )(shift, qt, k, vt)


def _attn_b_fast_kernel(far_ref, lam_ref, shift_ref, q_ref, k_ref, v_ref, bias_ref, gcol_ref, o_ref,
                        qaug_ref, p_ref, acc_ref):
    h = pl.program_id(1)
    i = pl.program_id(2)
    t = q_ref.shape[2]
    nk = k_ref.shape[2] // t
    win = min(B_WIN, nk)
    nfar = nk - win
    q = q_ref[0]
    z = jnp.zeros((B_QK, t), q.dtype)
    qaug_ref[0:HEAD_DIM] = jnp.concatenate(
        [jnp.concatenate([q[0:B_QK], z], axis=0), jnp.concatenate([z, q[B_QK:HEAD_DIM]], axis=0)], axis=1)
    qaug_ref[HEAD_DIM:K_LANES] = _shift_rows(shift_ref[h], 2 * t)
    acc_ref[...] = jnp.zeros(acc_ref.shape, F32)
    w0 = jnp.clip(i - 1, 0, nk - win)

    def kchunk(j):
        return k_ref[0, 0, pl.ds(pl.multiple_of(j * t, t), t), :]

    def vchunk(j):
        return v_ref[0, :, pl.ds(pl.multiple_of(j * t, t), t)]

    def far_chunk(f):
        return jnp.where(f < w0, f, f + win)

    def far_cls(f):
        return jnp.where(f < w0, 1, 2)

    def qk_win(j):
        b = bias_ref[0, j - i + B_TILES // 2]
        _fast_qk_exp(kchunk(j), qaug_ref, p_ref, jnp.concatenate([b, b], axis=1))

    qk_win(w0)
    for w in range(1, win):
        _fast_pv(vchunk(w0 + w - 1), p_ref, acc_ref, 0)
        qk_win(w0 + w)
    _fast_pv(vchunk(w0 + win - 1), p_ref, acc_ref, 0)

    if nfar > 0:
        _fast_qk_exp(kchunk(far_chunk(0)), qaug_ref, p_ref)

        def far_step(f):
            _fast_pv(vchunk(far_chunk(f - 1)), p_ref, acc_ref, far_cls(f - 1))
            _fast_qk_exp(kchunk(far_chunk(f)), qaug_ref, p_ref)

        n_full = nfar - 1

        def group(fg, carry):
            for u in range(FAST_B_UNROLL):
                far_step(1 + FAST_B_UNROLL * fg + u)
            return carry

        lax.fori_loop(0, n_full // FAST_B_UNROLL, group, 0)
        for f in range(1 + n_full - n_full % FAST_B_UNROLL, nfar):
            far_step(f)
        _fast_pv(vchunk(far_chunk(nfar - 1)), p_ref, acc_ref, far_cls(nfar - 1))

    wneg = jnp.exp2(jnp.full((1, 1), far_ref[h, 0], F32))
    wpos = jnp.exp2(jnp.full((1, 1), far_ref[h, 1], F32))
    acc = acc_ref[0] + wneg * acc_ref[1] + wpos * acc_ref[2]
    o = acc[0:HEAD_DIM] / acc[HEAD_DIM:HEAD_DIM + 1]
    o = o[:, 0:t] - lam_ref[0] * o[:, t:2 * t]
    ms = jnp.mean(o * o, axis=0, keepdims=True)
    o_ref[0] = (o * lax.rsqrt(ms + EPS) * gcol_ref[...]).astype(o_ref.dtype)


def _attn_b_fast(qt, k, vt, bias_tiles, far, lam, gcol, shift):
    bsz, _, s_len = qt.shape
    t = bias_tiles.shape[-1]
    return pl.pallas_call(
        _attn_b_fast_kernel,
        grid_spec=pltpu.PrefetchScalarGridSpec(
            num_scalar_prefetch=3,
            grid=(bsz, B_HEADS, s_len // t),
            in_specs=[
                pl.BlockSpec((1, HEAD_DIM, t), lambda b, h, i, *_: (b, h, i)),
                pl.BlockSpec((1, 1, s_len, K_LANES), lambda b, h, i, *_: (b, h, 0, 0)),
                pl.BlockSpec((1, HEAD_DIM, s_len), lambda b, h, i, *_: (b, h, 0)),
                pl.BlockSpec((1, B_TILES, t, t), lambda b, h, i, *_: (h, 0, 0, 0)),
                pl.BlockSpec((HEAD_DIM, 1), lambda b, h, i, *_: (0, 0)),
            ],
            out_specs=pl.BlockSpec((1, HEAD_DIM, t), lambda b, h, i, *_: (b, h, i)),
            scratch_shapes=_fast_scratch(t, 2 * t, 3),
        ),
        out_shape=jax.ShapeDtypeStruct((bsz, B_Q, s_len), BF16),
        compiler_params=_cparams(("arbitrary", "arbitrary", "arbitrary")),
        name="attn_b_fast",
    )(far, lam, shift, qt, k, vt, bias_tiles, gcol)


def _attn_a_kernel(sink_ref, q_ref, kp_ref, kc_ref, kn_ref, vp_ref, vc_ref, vn_ref, bias_ref, o_ref):
    n = pl.program_id(1)
    nb = pl.num_programs(1)
    g = A_HEADS // A_KV
    blk = A_BLK
    row = lax.broadcasted_iota(jnp.int32, (3 * blk, 1), 0)
    valid = ((row >= blk) | (n > 0)) & ((row < 2 * blk) | (n < nb - 1))
    for kv in range(A_KV):
        q3 = q_ref[0, kv * g * HEAD_DIM:(kv + 1) * g * HEAD_DIM]
        qcat = jnp.concatenate([q3[h * HEAD_DIM:(h + 1) * HEAD_DIM] for h in range(g)], axis=1)
        qcat = jnp.concatenate([qcat, jnp.zeros((K_LANES - HEAD_DIM, g * blk), qcat.dtype)], axis=0)
        kwin = jnp.concatenate([kp_ref[0, kv], kc_ref[0, kv], kn_ref[0, kv]], axis=0)
        hs = slice(kv * HEAD_DIM, (kv + 1) * HEAD_DIM)
        vwin = jnp.concatenate([vp_ref[0, hs], vc_ref[0, hs], vn_ref[0, hs]], axis=1)
        bias = jnp.concatenate([bias_ref[kv * g + h] for h in range(g)], axis=1)
        s = jnp.where(valid, _dot(kwin, qcat) + bias, NEG_BIG)
        sink = jnp.concatenate(
            [jnp.full((1, blk), sink_ref[kv * g + h], F32) for h in range(g)], axis=1)
        m = jnp.maximum(jnp.max(s, axis=0, keepdims=True), sink)
        p = jnp.exp2(s - m)
        denom = jnp.sum(p, axis=0, keepdims=True) + jnp.exp2(sink - m)
        o = _dot(vwin, p.astype(BF16)) / denom
        o_ref[0, kv * g * HEAD_DIM:(kv + 1) * g * HEAD_DIM] = jnp.concatenate(
            [o[:, h * blk:(h + 1) * blk] for h in range(g)], axis=0).astype(o_ref.dtype)


def _attn_a(qt, k, vt, bias, sink):
    bsz, _, s_len = qt.shape
    nb = s_len // A_BLK

    def kspec(off):
        return pl.BlockSpec((1, A_KV, A_BLK, K_LANES),
                            lambda b, n, *_: (b, 0, jnp.clip(n + off, 0, nb - 1), 0))

    def vspec(off):
        return pl.BlockSpec((1, A_V, A_BLK), lambda b, n, *_: (b, 0, jnp.clip(n + off, 0, nb - 1)))

    return pl.pallas_call(
        _attn_a_kernel,
        grid_spec=pltpu.PrefetchScalarGridSpec(
            num_scalar_prefetch=1,
            grid=(bsz, nb),
            in_specs=[
                pl.BlockSpec((1, A_Q, A_BLK), lambda b, n, *_: (b, 0, n)),
                kspec(-1), kspec(0), kspec(1), vspec(-1), vspec(0), vspec(1),
                pl.BlockSpec((A_HEADS, 3 * A_BLK, A_BLK), lambda b, n, *_: (0, 0, 0)),
            ],
            out_specs=pl.BlockSpec((1, A_Q, A_BLK), lambda b, n, *_: (b, 0, n)),
        ),
        out_shape=jax.ShapeDtypeStruct((bsz, A_Q, s_len), BF16),
        compiler_params=_cparams(("arbitrary", "arbitrary")),
        name="attn_a",
    )(sink, qt, k, k, k, vt, vt, vt, bias)


def _outproj_kernel(oa_ref, ob_ref, oc_ref, wa_ref, wb_ref, wc_ref, x_ref, mod_ref, g_ref,
                    wrh_ref, wrl_ref, x1_ref, h2_ref, aff_ref):
    out_t = _dot(wa_ref[...], oa_ref[0]) + _dot(wb_ref[...], ob_ref[0]) + _dot(wc_ref[...], oc_ref[0])
    out = out_t.T
    mod = mod_ref[0]
    d = out.shape[1]
    gt_a = mod[:, 2 * d:3 * d]
    sh_f, sc_f = mod[:, 3 * d:4 * d], mod[:, 4 * d:5 * d]
    x1 = x_ref[...] + gt_a * out
    x1_ref[...] = x1
    ms = jnp.mean(x1 * x1, axis=-1, keepdims=True)
    h2 = x1 * lax.rsqrt(ms + EPS) * g_ref[...]
    h2 = h2 * (1.0 + sc_f) + sh_f
    h2_ref[...] = h2
    hi = h2.astype(BF16)
    lo = (h2 - hi.astype(F32)).astype(BF16)
    lt = _dot_nt(wrh_ref[...], hi) + _dot_nt(wrh_ref[...], lo) + _dot_nt(wrl_ref[...], hi)
    lt = lt - jnp.max(lt, axis=0, keepdims=True)
    e = jnp.exp(lt)
    aff_ref[...] = e / jnp.sum(e, axis=0, keepdims=True)


def _outproj(oa, ob, oc, wa, wb, wc, x2d, mod3, g_ffn, wr_hi, wr_lo, s_len):
    n, d = x2d.shape
    tm = min(TM_PROJ, s_len)
    spb = s_len // tm

    def ot_spec(rows_):
        return pl.BlockSpec((1, rows_, tm), lambda i: (i // spb, 0, i % spb))

    def full(shape):
        return pl.BlockSpec(shape, lambda i: tuple(0 for _ in shape))

    return pl.pallas_call(
        _outproj_kernel,
        grid=(n // tm,),
        in_specs=[
            ot_spec(A_Q), ot_spec(B_Q), ot_spec(C_Q),
            full((d, A_Q)), full((d, B_Q)), full((d, C_Q)),
            pl.BlockSpec((tm, d), lambda i: (i, 0)),
            pl.BlockSpec((1, 1, N_MOD * d), lambda i: (i // spb, 0, 0)),
            full((1, d)), full((N_EXPERTS, d)), full((N_EXPERTS, d)),
        ],
        out_specs=[
            pl.BlockSpec((tm, d), lambda i: (i, 0)),
            pl.BlockSpec((tm, d), lambda i: (i, 0)),
            pl.BlockSpec((N_EXPERTS, tm), lambda i: (0, i)),
        ],
        out_shape=[
            jax.ShapeDtypeStruct((n, d), F32),
            jax.ShapeDtypeStruct((n, d), F32),
            jax.ShapeDtypeStruct((N_EXPERTS, n), F32),
        ],
        compiler_params=_cparams(("arbitrary",)),
        name="outproj",
    )(oa, ob, oc, wa, wb, wc, x2d, mod3, g_ffn, wr_hi, wr_lo)


def _routing_kernel(aff_ref, idx_ref, gate_ref, pexc_ref, thr_ref, need_ref, *, cap, pc):
    ne, nblk, _ = aff_ref.shape
    bits = lax.bitcast_convert_type(aff_ref[...], jnp.int32)

    def bis(it, t):
        cand = t | jnp.left_shift(jnp.int32(1), 30 - it)
        cnt = jnp.sum((bits >= cand).astype(F32), axis=(1, 2), keepdims=True)
        return jnp.where(cnt >= cap, cand, t)

    thr = lax.fori_loop(0, 31, bis, jnp.zeros((ne, 1, 1), jnp.int32))
    n_gt = jnp.sum((bits > thr).astype(F32), axis=(1, 2), keepdims=True)
    thr_ref[...] = jnp.broadcast_to(thr, thr_ref.shape)
    need_ref[...] = jnp.broadcast_to(cap - n_gt, need_ref.shape)

    def tri(nn, fn):
        a = lax.broadcasted_iota(jnp.int32, (nn, nn), 0)
        b = lax.broadcasted_iota(jnp.int32, (nn, nn), 1)
        return jnp.where(fn(a, b), 1.0, 0.0).astype(BF16)

    ones_l = jnp.ones((LANES, LANES), BF16)
    su_l = tri(LANES, lambda a, b: a < b)
    li_l = tri(LANES, lambda a, b: b <= a)
    sl_b = tri(nblk, lambda a, b: b < a)
    li_b = tri(nblk, lambda a, b: b <= a)
    blk_col = lax.broadcasted_iota(jnp.int32, (nblk, 1), 0).astype(F32)
    lane_col = lax.broadcasted_iota(jnp.int32, (LANES, 1), 0).astype(F32)

    def per_expert(e, carry):
        a = aff_ref[e]
        be = lax.bitcast_convert_type(a, jnp.int32)
        te = thr_ref[e]
        gt = be > te
        eq = (be == te)
        eqb = jnp.where(eq, 1.0, 0.0).astype(BF16)
        rank = _dot(eqb, su_l) + _dot(sl_b, _dot(eqb, ones_l).astype(BF16))
        sel = gt | (eq & (rank < need_ref[e]))
        mb = jnp.where(sel, 1.0, 0.0).astype(BF16)
        cnt_b = _dot(mb, ones_l)
        pinc = _dot(li_b, cnt_b.astype(BF16))
        pinc_col = pinc[:, 0:1]
        pexc_col = pinc_col - cnt_b[:, 0:1]
        pexc_ref[e] = (pinc - cnt_b).T[0:1].astype(jnp.int32)
        mt = jnp.where(sel, 1.0, 0.0).T.astype(BF16)
        at = a.T
        at_hi = at.astype(BF16)
        at_lo = (at - at_hi.astype(F32)).astype(BF16)
        for c in range(cap // pc):
            p_row = (lax.broadcasted_iota(jnp.int32, (1, pc), 1) + c * pc).astype(F32)
            blk_row = jnp.sum(jnp.where(pinc_col <= p_row, 1.0, 0.0), axis=0, keepdims=True)
            oh = blk_col == blk_row
            ohb = jnp.where(oh, 1.0, 0.0).astype(BF16)
            r_row = p_row - jnp.sum(jnp.where(oh, pexc_col, 0.0), axis=0, keepdims=True)
            gt_rows = _dot(mt, ohb)
            pc_incl = _dot(li_l, gt_rows.astype(BF16))
            lane_row = jnp.sum(jnp.where(pc_incl <= r_row, 1.0, 0.0), axis=0, keepdims=True)
            idx_ref[e, :, c * pc:(c + 1) * pc] = (blk_row * LANES + lane_row).astype(jnp.int32)
            ag = _dot(at_hi, ohb) + _dot(at_lo, ohb)
            gate_ref[e, :, c * pc:(c + 1) * pc] = jnp.sum(
                jnp.where(lane_col == lane_row, ag, 0.0), axis=0, keepdims=True)
        return carry

    lax.fori_loop(0, ne, per_expert, 0)


def _routing(aff3, cap):
    ne, nblk, _ = aff3.shape
    pc = min(ROUTE_PC, cap)
    return pl.pallas_call(
        functools.partial(_routing_kernel, cap=cap, pc=pc),
        grid=(1,),
        in_specs=[pl.BlockSpec((ne, nblk, LANES), lambda i: (0, 0, 0))],
        out_specs=[
            pl.BlockSpec((ne, 1, cap), lambda i: (0, 0, 0)),
            pl.BlockSpec((ne, 1, cap), lambda i: (0, 0, 0)),
            pl.BlockSpec((ne, 1, nblk), lambda i: (0, 0, 0)),
        ],
        out_shape=[
            jax.ShapeDtypeStruct((ne, 1, cap), jnp.int32),
            jax.ShapeDtypeStruct((ne, 1, cap), F32),
            jax.ShapeDtypeStruct((ne, 1, nblk), jnp.int32),
        ],
        scratch_shapes=[pltpu.VMEM((ne, 1, LANES), jnp.int32), pltpu.VMEM((ne, 1, LANES), F32)],
        compiler_params=_cparams(("arbitrary",)),
        name="routing",
    )(aff3)


TOK_LANES = LANES
TOK_RADIX = 256


def _moe_kernel(idx_ref, idxn_ref, gate_ref, h_hbm, wg_ref, wu_ref, wd_ref, y_ref,
                idx_smem, xbuf, sems):
    ch = xbuf.shape[1]
    d = xbuf.shape[2]
    nsteps = pl.num_programs(0) * pl.num_programs(1)
    s = pl.program_id(0) * pl.num_programs(1) + pl.program_id(1)
    slot = s % 2
    nslot = 1 - slot

    def load_idx(src_ref, sl):
        cp = pltpu.make_async_copy(src_ref.at[0, 0], idx_smem.at[sl], sems.at[2])
        cp.start()
        cp.wait()

    def row_copy(sl, r, t):
        return pltpu.make_async_copy(h_hbm.at[pl.ds(t, 1)], xbuf.at[sl, pl.ds(r, 1)], sems.at[sl])

    @pl.when(s == 0)
    def _():
        load_idx(idx_ref, 0)
        for r in range(ch):
            row_copy(0, r, idx_smem[0, r]).start()

    load_idx(idxn_ref, nslot)
    for r in range(ch):
        row_copy(slot, r, 0).wait()
    for r in range(ch):
        row_copy(nslot, r, idx_smem[nslot, r]).start()

    x = xbuf[slot].astype(BF16)
    gg = _dot(x, wg_ref[0])
    uu = _dot(x, wu_ref[0])
    hid = (gg * (1.0 / (1.0 + jnp.exp(-gg))) * uu).astype(BF16)
    y = _dot(hid, wd_ref[0])
    sub = lax.broadcasted_iota(jnp.int32, (LANES, ch), 0)
    cols = jnp.where(sub == 0, gate_ref[0], jnp.where(sub == 1, idx_ref[0].astype(F32), 0.0)).T
    y_ref[:, 0:d] = (y * cols[:, 0:1]).astype(y_ref.dtype)
    tok = cols[:, 1:2]
    hi = jnp.floor(tok * (1.0 / TOK_RADIX))
    lane = lax.broadcasted_iota(jnp.int32, (ch, TOK_LANES), 1)
    y_ref[:, d:d + TOK_LANES] = jnp.where(
        lane == 0, hi, jnp.where(lane == 1, tok - hi * TOK_RADIX, 0.0)).astype(y_ref.dtype)

    @pl.when(s == nsteps - 1)
    def _():
        for r in range(ch):
            row_copy(nslot, r, 0).wait()


def _moe(idx3, gate3, h2, wg, wu, wd, cap):
    n, d = h2.shape
    ne, _, ff = wg.shape
    ch = min(MOE_CH, cap)
    nch = cap // ch
    nsteps = ne * nch
    return pl.pallas_call(
        _moe_kernel,
        grid=(ne, nch),
        in_specs=[
            pl.BlockSpec((1, 1, ch), lambda e, c: (e * nch + c, 0, 0)),
            pl.BlockSpec((1, 1, ch), lambda e, c: (jnp.minimum(e * nch + c + 1, nsteps - 1), 0, 0)),
            pl.BlockSpec((1, 1, ch), lambda e, c: (e * nch + c, 0, 0)),
            pl.BlockSpec(memory_space=pl.ANY),
            pl.BlockSpec((1, d, ff), lambda e, c: (e, 0, 0)),
            pl.BlockSpec((1, d, ff), lambda e, c: (e, 0, 0)),
            pl.BlockSpec((1, ff, d), lambda e, c: (e, 0, 0)),
        ],
        out_specs=pl.BlockSpec((ch, d + TOK_LANES), lambda e, c: (e * nch + c, 0)),
        out_shape=jax.ShapeDtypeStruct((ne * cap, d + TOK_LANES), BF16),
        scratch_shapes=[
            pltpu.SMEM((2, ch), jnp.int32),
            pltpu.VMEM((2, ch, d), F32),
            pltpu.SemaphoreType.DMA((3,)),
        ],
        compiler_params=_cparams(("arbitrary", "arbitrary")),
        name="moe_ffn",
    )(idx3.reshape(nsteps, 1, ch), idx3.reshape(nsteps, 1, ch), gate3.reshape(nsteps, 1, ch), h2, wg, wu, wd)


def _combine_kernel(starts_ref, y_hbm, x_ref, mod_ref, o_ref, ybuf, xtra, sems, *, cap, ne):
    tt, d = x_ref.shape
    wr = ybuf.shape[2]
    nt = pl.num_programs(0)
    t = pl.program_id(0)
    slot = t % 2
    nslot = 1 - slot

    def bounds(e, tile):
        lo = starts_ref[e * (nt + 1) + tile]
        hi = starts_ref[e * (nt + 1) + tile + 1]
        base = jnp.minimum((lo // 16) * 16, cap - wr)
        return lo, hi, base

    def win_copy(e, base, sl):
        return pltpu.make_async_copy(y_hbm.at[pl.ds(pl.multiple_of(e * cap + base, 16), wr)],
                                     ybuf.at[sl, e], sems.at[sl])

    def fetch(tile, sl):
        for e in range(ne):
            win_copy(e, bounds(e, tile)[2], sl).start()

    @pl.when(t == 0)
    def _():
        fetch(0, 0)

    for e in range(ne):
        win_copy(e, 0, slot).wait()
    fetch(jnp.minimum(t + 1, nt - 1), nslot)

    tok0 = (t * tt).astype(F32)
    lane_tok = lax.broadcasted_iota(jnp.int32, (1, tt), 1).astype(F32) + tok0
    row = lax.broadcasted_iota(jnp.int32, (wr, 1), 0)

    def contribution(yb, first, lo_eff, hi):
        tokc = yb[:, d:d + 1].astype(F32) * TOK_RADIX + yb[:, d + 1:d + 2].astype(F32)
        pos = row + first
        oh = jnp.where((pos >= lo_eff) & (pos < hi) & (tokc == lane_tok), 1.0, 0.0).astype(BF16)
        return lax.dot_general(oh, yb[:, 0:d], (((0,), (0,)), ((), ())), preferred_element_type=F32)

    acc = jnp.zeros((tt, d), F32)
    for e in range(ne):
        lo, hi, base = bounds(e, t)
        acc = acc + contribution(ybuf[slot, e], base, lo, hi)
    o_ref[...] = x_ref[...] + mod_ref[0][:, 5 * d:6 * d] * acc

    for e in range(ne):
        lo, hi, base = bounds(e, t)
        n_more = jnp.maximum(hi - (base + wr) + wr - 1, 0) // wr

        def more(k, carry, e=e, hi=hi, base=base):
            done = base + wr * (k + 1)
            first = jnp.minimum(done, cap - wr)
            cp = pltpu.make_async_copy(y_hbm.at[pl.ds(pl.multiple_of(e * cap + first, 16), wr)], xtra, sems.at[2])
            cp.start()
            cp.wait()
            o_ref[...] = o_ref[...] + mod_ref[0][:, 5 * d:6 * d] * contribution(xtra[...], first, done, hi)
            return carry

        lax.fori_loop(0, n_more, more, 0)

    @pl.when(t == nt - 1)
    def _():
        for e in range(ne):
            win_copy(e, 0, nslot).wait()


def _combine(starts, y, x1, mod3, s_len, cap):
    n, d = x1.shape
    ne = y.shape[0] // cap
    tt = min(COMBINE_TT, s_len)
    spb = s_len // tt
    wr = min(COMBINE_WR, cap)
    return pl.pallas_call(
        functools.partial(_combine_kernel, cap=cap, ne=ne),
        grid_spec=pltpu.PrefetchScalarGridSpec(
            num_scalar_prefetch=1,
            grid=(n // tt,),
            in_specs=[
                pl.BlockSpec(memory_space=pl.ANY),
                pl.BlockSpec((tt, d), lambda i, *_: (i, 0)),
                pl.BlockSpec((1, 1, N_MOD * d), lambda i, *_: (i // spb, 0, 0)),
            ],
            out_specs=pl.BlockSpec((tt, d), lambda i, *_: (i, 0)),
            scratch_shapes=[
                pltpu.VMEM((2, ne, wr, d + TOK_LANES), BF16),
                pltpu.VMEM((wr, d + TOK_LANES), BF16),
                pltpu.SemaphoreType.DMA((3,)),
            ],
        ),
        out_shape=jax.ShapeDtypeStruct((n, d), F32),
        compiler_params=_cparams(("arbitrary",)),
        name="combine",
    )(starts, y, x1, mod3)


def _t5_bucket(rel):
    half = NUM_BUCKETS // 2
    max_exact = half // 2
    n = jnp.abs(rel)
    nf = jnp.maximum(n, 1).astype(F32)
    large = max_exact + (jnp.log(nf / max_exact) / math.log(MAX_DISTANCE / max_exact)
                         * (half - max_exact)).astype(jnp.int32)
    large = jnp.minimum(large, half - 1)
    return jnp.where(rel > 0, half, 0) + jnp.where(n < max_exact, n, large)


def _rope_table(s_len):
    quarter = HEAD_DIM // 4
    freqs = ROPE_THETA ** (-jnp.arange(quarter, dtype=F32) / quarter)
    t = jnp.arange(s_len)
    ang_r = freqs[:, None] * (t // GRID_W).astype(F32)[None, :]
    ang_c = freqs[:, None] * (t % GRID_W).astype(F32)[None, :]
    return jnp.concatenate([jnp.cos(ang_r), jnp.sin(ang_r), jnp.cos(ang_c), jnp.sin(ang_c)], axis=0)


def _toeplitz(fn, nj, ni):
    period = nj + ni
    m = jnp.arange(period)
    b = fn(jnp.where(m < nj, m, m - period))
    flat = jnp.tile(b, ni)[..., :ni * (period - 1)]
    rows = flat.reshape(b.shape[:-1] + (ni, period - 1))
    return jnp.swapaxes(rows[..., :nj], -1, -2)


def _bias_tables(rel_bias, t_b):
    def bias_of(rel, heads):
        return rel_bias[_t5_bucket(rel)][:, heads].T * LOG2E

    def fn_a(r):
        rel = r - A_BLK
        return jnp.where(jnp.abs(rel) <= WINDOW, bias_of(rel, slice(0, A_HEADS)), NEG_BIG)

    bias_a = _toeplitz(fn_a, 3 * A_BLK, A_BLK).astype(F32)
    tiles = [
        _toeplitz(lambda r, dd=dd: bias_of(dd * t_b + r, slice(A_HEADS, None)), t_b, t_b)
        for dd in range(-(B_TILES // 2), B_TILES // 2 + 1)
    ]
    bias_b = jnp.stack(tiles, axis=1).astype(F32)
    half = NUM_BUCKETS // 2
    far = jnp.stack([rel_bias[half - 1, A_HEADS:], rel_bias[NUM_BUCKETS - 1, A_HEADS:]], axis=1) * LOG2E
    return bias_a, bias_b, far.astype(F32)


def _gain_column(ga, gb, gc):
    sa = HEAD_DIM ** -0.5 * LOG2E
    sb = B_QK ** -0.5 * LOG2E
    parts = [
        jnp.tile(ga[0] * sa, A_HEADS), jnp.tile(ga[1], A_KV), jnp.ones((A_V,), F32),
        jnp.tile(gb[0] * sb, 2 * B_HEADS), jnp.tile(gb[1], 2 * B_HEADS), jnp.ones((B_V,), F32),
        jnp.tile(gc[0] * sa, C_HEADS), jnp.tile(gc[1], C_KV), jnp.ones((C_V,), F32),
    ]
    return jnp.concatenate(parts).reshape(IN_COLS, 1).astype(F32)


def _trunk(x, mod_l, prep, s_len, bsz):
    n = bsz * s_len
    d = x.shape[-1]
    x2d = x.reshape(n, d)
    t_b = min(T_B, s_len)
    cap = CAPACITY_FACTOR * n // N_EXPERTS
    cs = _rope_table(s_len)
    for l, lp in enumerate(prep["layers"]):
        mod3 = mod_l[l][:, None, :]
        qa, ka, va, qb, kb, vb, qc, kc, vc = _inproj(
            x2d, mod3, lp["g_attn"], lp["w_in_t"], lp["gcol"], cs, bsz, s_len)
        oa = _attn_a(qa, ka, va, prep["bias_a"], lp["sink"])
        bias_b = prep["bias_b"][t_b]
        ob = lax.cond(
            lp["b_bounded"],
            lambda q, k, v: _attn_b_fast(q, k, v, bias_b, prep["far_b"], lp["lam"], lp["subln_col"], lp["b_shift"]),
            lambda q, k, v: _attn_b(q, k, v, bias_b, prep["far_b"], lp["lam"], lp["subln_col"]),
            qb, kb, vb)
        oc = lax.cond(
            lp["c_bounded"],
            lambda q, k, v: _attn_c_fast(q, k, v, lp["c_shift"]),
            _attn_c,
            qc, kc, vc)
        x1, h2, aff = _outproj(oa, ob, oc, lp["wo_a"], lp["wo_b"], lp["wo_c"], x2d, mod3,
                               lp["g_ffn"], lp["wr_hi"], lp["wr_lo"], s_len)
        idx3, gate3, pexc = _routing(aff.reshape(N_EXPERTS, n // LANES, LANES), cap)
        y = _moe(idx3, gate3, h2, lp["wg"], lp["wu"], lp["wd"], cap)
        tt = min(COMBINE_TT, s_len)
        starts = jnp.concatenate(
            [pexc[:, 0, ::tt // LANES], jnp.full((N_EXPERTS, 1), cap, jnp.int32)], axis=1).reshape(-1)
        x2d = _combine(starts, y, x1, mod3, s_len, cap)
    return x2d.reshape(bsz, s_len, d)


def kernel(x_prompt, x_sample, c_prompt, c_sample, w_mod, b_mod, g_attn, g_ffn, w_in, w_out,
           qk_gain_a, qk_gain_b, qk_gain_c, sink_a, lam_b, subln_b, rel_bias,
           w_router, w_gate, w_up, w_down):
    depth = w_mod.shape[0]
    bp, sp, d = x_prompt.shape
    bs, ss, _ = x_sample.shape
    rows = -(-(bp + bs) // 16) * 16
    c_all = jnp.zeros((rows, d), F32).at[:bp].set(c_prompt).at[bp:bp + bs].set(c_sample)
    mod_all = _modulation(c_all, w_mod, b_mod)
    mod_p, mod_s = mod_all[:, :bp], mod_all[:, bp:bp + bs]

    tbs = sorted({min(T_B, sp), min(T_B, ss)})
    tabs = {t: _bias_tables(rel_bias, t) for t in tbs}
    prep = {"bias_a": tabs[tbs[0]][0], "far_b": tabs[tbs[0]][2],
            "bias_b": {t: tabs[t][1] for t in tbs}, "layers": []}
    for l in range(depth):
        lam_init = 0.8 - 0.6 * math.exp(-0.3 * l)
        lp = lam_b[l].astype(F32)
        lam = jnp.exp(jnp.sum(lp[0] * lp[1])) - jnp.exp(jnp.sum(lp[2] * lp[3])) + lam_init
        wo_t = w_out[l].T.astype(BF16)
        wr_t = w_router[l].T
        wr_hi = wr_t.astype(BF16)
        gmax = lambda g_: jnp.max(jnp.abs(g_))
        c_c = 1.02 * HEAD_DIM * gmax(qk_gain_c[l][0]) * gmax(qk_gain_c[l][1]) * (HEAD_DIM ** -0.5 * LOG2E)
        c_b = 1.02 * B_QK * gmax(qk_gain_b[l][0]) * gmax(qk_gain_b[l][1]) * (B_QK ** -0.5 * LOG2E)
        bias_vals = rel_bias[:, A_HEADS:] * LOG2E
        prep["layers"].append({
            "c_shift": c_c.reshape(1).astype(F32),
            "c_bounded": 2.0 * c_c <= SHIFT_LIMIT,
            "b_shift": (c_b + jnp.max(bias_vals, axis=0)).astype(F32),
            "b_bounded": 2.0 * c_b + 2.0 * jnp.max(jnp.abs(bias_vals)) <= SHIFT_LIMIT,
            "g_attn": g_attn[l].reshape(1, d), "g_ffn": g_ffn[l].reshape(1, d),
            "w_in_t": w_in[l].T.astype(BF16),
            "gcol": _gain_column(qk_gain_a[l], qk_gain_b[l], qk_gain_c[l]),
            "sink": (sink_a[l] * LOG2E).astype(F32),
            "lam": lam.reshape(1).astype(F32),
            "subln_col": (subln_b[l] * (1.0 - lam_init)).reshape(HEAD_DIM, 1).astype(F32),
            "wo_a": wo_t[:, 0:A_Q], "wo_b": wo_t[:, A_Q:A_Q + B_Q], "wo_c": wo_t[:, A_Q + B_Q:],
            "wr_hi": wr_hi, "wr_lo": (wr_t - wr_hi.astype(F32)).astype(BF16),
            "wg": w_gate[l].astype(BF16), "wu": w_up[l].astype(BF16), "wd": w_down[l].astype(BF16),
        })
    y_p = _trunk(x_prompt, mod_p, prep, sp, bp)
    y_s = _trunk(x_sample, mod_s, prep, ss, bs)
    return (y_p, y_s)
```

```python
import functools
import math

import jax
import jax.numpy as jnp
from jax import lax
from jax.experimental import pallas as pl
from jax.experimental.pallas import tpu as pltpu

F32 = jnp.float32
BF16 = jnp.bfloat16

HEAD_DIM = 64
A_HEADS, A_KV = 6, 2
B_HEADS, B_QK = 4, 32
C_HEADS, C_KV = 6, 2
WINDOW = 128
A_BLK = 128
NUM_BUCKETS, MAX_DISTANCE = 32, 128
ROPE_THETA = 10000.0
GRID_W = 64
N_EXPERTS = 16
CAPACITY_FACTOR = 2
N_MOD = 6
EPS = 1e-6
NEG_BIG = -1e30
LOG2E = math.log2(math.e)

A_Q, A_K, A_V = A_HEADS * HEAD_DIM, A_KV * HEAD_DIM, A_KV * HEAD_DIM
B_Q, B_K, B_V = B_HEADS * HEAD_DIM, B_HEADS * HEAD_DIM, B_HEADS * HEAD_DIM
C_Q, C_K, C_V = C_HEADS * HEAD_DIM, C_KV * HEAD_DIM, C_KV * HEAD_DIM
A_OFF = 0
B_OFF = A_Q + A_K + A_V
C_OFF = B_OFF + B_Q + B_K + B_V
IN_COLS = C_OFF + C_Q + C_K + C_V

LANES = 128
K_LANES = LANES
VMEM_LIMIT_BYTES = 56 * 1024 * 1024

TM_PROJ = 512
TQ_C, TK_C = 512, 1024
T_B = 512
MOE_CH = 256
ROUTE_PC = 512
COMBINE_TT = 512
COMBINE_WR = 128


def _cparams(sem):
    return pltpu.CompilerParams(dimension_semantics=sem, vmem_limit_bytes=VMEM_LIMIT_BYTES)


def _dot(a, b):
    return jnp.dot(a, b, preferred_element_type=F32)


def _dot_nt(a, b):
    return lax.dot_general(a, b, (((1,), (1,)), ((), ())), preferred_element_type=F32)


def _mod_kernel(c_ref, w_ref, b_ref, o_ref):
    c = c_ref[...]
    a = c * (1.0 / (1.0 + jnp.exp(-c)))
    a_hi = a.astype(BF16)
    a_lo = (a - a_hi.astype(F32)).astype(BF16)
    w = w_ref[0]
    w_hi = w.astype(BF16)
    w_lo = (w - w_hi.astype(F32)).astype(BF16)
    acc = _dot(a_hi, w_hi) + _dot(a_lo, w_hi) + _dot(a_hi, w_lo)
    o_ref[0] = acc + b_ref[0]


def _modulation(c_all, w_mod, b_mod):
    depth, d, nm = w_mod.shape
    rows = c_all.shape[0]
    bn = 1024
    return pl.pallas_call(
        _mod_kernel,
        grid=(depth, nm // bn),
        in_specs=[
            pl.BlockSpec((rows, d), lambda l, j: (0, 0)),
            pl.BlockSpec((1, d, bn), lambda l, j: (l, 0, j)),
            pl.BlockSpec((1, 1, bn), lambda l, j: (l, 0, j)),
        ],
        out_specs=pl.BlockSpec((1, rows, bn), lambda l, j: (l, 0, j)),
        out_shape=jax.ShapeDtypeStruct((depth, rows, nm), F32),
        compiler_params=_cparams(("arbitrary", "arbitrary")),
        name="modulation",
    )(c_all, w_mod, b_mod.reshape(depth, 1, nm))


def _segnorm(rows, seglen):
    r, t = rows.shape
    r3 = rows.reshape(r // seglen, seglen, t)
    ms = jnp.mean(r3 * r3, axis=1, keepdims=True)
    return (r3 * lax.rsqrt(ms + EPS)).reshape(r, t)


def _rope(rows, cs):
    r, t = rows.shape
    x = rows.reshape(r // HEAD_DIM, HEAD_DIM, t)
    cr, sr, cc, sc = cs[None, 0:16], cs[None, 16:32], cs[None, 32:48], cs[None, 48:64]
    ar, br, ac, bc = x[:, 0:16], x[:, 16:32], x[:, 32:48], x[:, 48:64]
    out = jnp.concatenate(
        [ar * cr - br * sr, br * cr + ar * sr, ac * cc - bc * sc, bc * cc + ac * sc], axis=1)
    return out.reshape(r, t)


def _emit_k(k_ref, kt):
    lane = lax.broadcasted_iota(jnp.int32, (1, K_LANES), 1)
    pad = jnp.where(lane == HEAD_DIM, 1.0, 0.0)
    for p in range(kt.shape[0] // LANES):
        tok = kt[p * LANES:(p + 1) * LANES].T
        k_ref[0, 2 * p] = jnp.where(lane < HEAD_DIM, tok, pad).astype(k_ref.dtype)
        k_ref[0, 2 * p + 1] = jnp.where(lane < HEAD_DIM, pltpu.roll(tok, HEAD_DIM, 1), pad).astype(k_ref.dtype)


def _emit_vt(v_ref, vt):
    v_ref[0] = vt.astype(v_ref.dtype)


def _inproj_kernel(x_ref, mod_ref, g_ref, w_ref, gcol_ref, cs_ref,
                   qa_ref, ka_ref, va_ref, qb_ref, kb_ref, vb_ref, qc_ref, kc_ref, vc_ref):
    x = x_ref[...]
    mod = mod_ref[0]
    d = x.shape[1]
    sh, sc = mod[:, 0:d], mod[:, d:2 * d]
    ms = jnp.mean(x * x, axis=-1, keepdims=True)
    h = x * lax.rsqrt(ms + EPS) * g_ref[...]
    h = (h * (1.0 + sc) + sh).astype(BF16)
    pt = _dot_nt(w_ref[...], h)
    gcol = gcol_ref[...]
    cs = cs_ref[...]

    def rows(off, n):
        return pt[off:off + n], gcol[off:off + n]

    q, g = rows(A_OFF, A_Q)
    qa_ref[0] = (_segnorm(q, HEAD_DIM) * g).astype(qa_ref.dtype)
    k, g = rows(A_OFF + A_Q, A_K)
    _emit_k(ka_ref, _segnorm(k, HEAD_DIM) * g)
    _emit_vt(va_ref, pt[A_OFF + A_Q + A_K:A_OFF + A_Q + A_K + A_V])
    q, g = rows(B_OFF, B_Q)
    qb_ref[0] = (_segnorm(q, B_QK) * g).astype(qb_ref.dtype)
    k, g = rows(B_OFF + B_Q, B_K)
    _emit_k(kb_ref, _segnorm(k, B_QK) * g)
    _emit_vt(vb_ref, pt[B_OFF + B_Q + B_K:B_OFF + B_Q + B_K + B_V])
    q, g = rows(C_OFF, C_Q)
    qc_ref[0] = _rope(_segnorm(q, HEAD_DIM) * g, cs).astype(qc_ref.dtype)
    k, g = rows(C_OFF + C_Q, C_K)
    _emit_k(kc_ref, _rope(_segnorm(k, HEAD_DIM) * g, cs))
    _emit_vt(vc_ref, pt[C_OFF + C_Q + C_K:C_OFF + C_Q + C_K + C_V])


def _inproj(x2d, mod3, g_attn, w_in_t, gcol, cs, bsz, s_len):
    n, d = x2d.shape
    tm = min(TM_PROJ, s_len)
    spb = s_len // tm

    def qt_spec(rows_):
        return pl.BlockSpec((1, rows_, tm), lambda i: (i // spb, 0, i % spb))

    def k_spec(nh):
        return pl.BlockSpec((1, nh, tm, K_LANES), lambda i: (i // spb, 0, i % spb, 0))

    def qt_shape(rows_):
        return jax.ShapeDtypeStruct((bsz, rows_, s_len), BF16)

    def k_shape(nh):
        return jax.ShapeDtypeStruct((bsz, nh, s_len, K_LANES), BF16)

    return pl.pallas_call(
        _inproj_kernel,
        grid=(n // tm,),
        in_specs=[
            pl.BlockSpec((tm, d), lambda i: (i, 0)),
            pl.BlockSpec((1, 1, N_MOD * d), lambda i: (i // spb, 0, 0)),
            pl.BlockSpec((1, d), lambda i: (0, 0)),
            pl.BlockSpec((IN_COLS, d), lambda i: (0, 0)),
            pl.BlockSpec((IN_COLS, 1), lambda i: (0, 0)),
            pl.BlockSpec((HEAD_DIM, tm), lambda i: (0, i % spb)),
        ],
        out_specs=[
            qt_spec(A_Q), k_spec(A_KV), qt_spec(A_V),
            qt_spec(B_Q), k_spec(B_HEADS), qt_spec(B_V),
            qt_spec(C_Q), k_spec(C_KV), qt_spec(C_V),
        ],
        out_shape=[
            qt_shape(A_Q), k_shape(A_KV), qt_shape(A_V),
            qt_shape(B_Q), k_shape(B_HEADS), qt_shape(B_V),
            qt_shape(C_Q), k_shape(C_KV), qt_shape(C_V),
        ],
        compiler_params=_cparams(("arbitrary",)),
        name="inproj",
    )(x2d, mod3, g_attn, w_in_t, gcol, cs)


def _pipe_init(m_ref, l_ref, alpha_ref, acc_ref, p_ref):
    m_ref[...] = jnp.full(m_ref.shape, NEG_BIG, F32)
    l_ref[...] = jnp.zeros(l_ref.shape, F32)
    alpha_ref[...] = jnp.ones(alpha_ref.shape, F32)
    acc_ref[...] = jnp.zeros(acc_ref.shape, F32)
    p_ref[...] = jnp.zeros(p_ref.shape, p_ref.dtype)


def _pipe_pv(vt, p_ref, alpha_ref, acc_ref):
    acc_ref[...] = alpha_ref[...] * acc_ref[...] + _dot(vt, p_ref[...])


def _pipe_sm(c, s_ref, bm_ref, m_ref, l_ref, alpha_ref, p_ref):
    m_old = m_ref[...]
    m_new = jnp.maximum(m_old, bm_ref[...] + c)
    alpha = jnp.exp2(m_old - m_new)
    p = jnp.exp2(s_ref[...] - (m_new - c))
    l_ref[...] = alpha * l_ref[...] + jnp.sum(p, axis=0, keepdims=True)
    p_ref[...] = p.astype(p_ref.dtype)
    m_ref[...] = m_new
    alpha_ref[...] = alpha


def _pipe_qk(kc, q_ref, s_ref, bm_ref, tile=None):
    s = _dot(kc, q_ref[...])
    if tile is not None:
        s = s + tile
    s_ref[...] = s
    bm_ref[...] = jnp.max(s, axis=0, keepdims=True)


def _pipe_scratch(tk, n):
    return [
        pltpu.VMEM((K_LANES, n), BF16),
        pltpu.VMEM((tk, n), F32),
        pltpu.VMEM((tk, n), BF16),
        pltpu.VMEM((1, n), F32),
        pltpu.VMEM((1, n), F32),
        pltpu.VMEM((1, n), F32),
        pltpu.VMEM((1, n), F32),
        pltpu.VMEM((HEAD_DIM, n), F32),
    ]


def _attn_c_kernel(q_ref, k_ref, v_ref, o_ref, qs_ref, s_ref, p_ref, bm_ref, m_ref, l_ref, alpha_ref, acc_ref,
                   *, tk):
    g = C_HEADS // C_KV
    tq = q_ref.shape[2]
    nk = k_ref.shape[2] // tk
    q3 = q_ref[0]
    qs_ref[0:HEAD_DIM] = jnp.concatenate([q3[h * HEAD_DIM:(h + 1) * HEAD_DIM] for h in range(g)], axis=1)
    qs_ref[HEAD_DIM:K_LANES] = jnp.zeros((K_LANES - HEAD_DIM, g * tq), qs_ref.dtype)
    _pipe_init(m_ref, l_ref, alpha_ref, acc_ref, p_ref)

    def kchunk(j):
        return k_ref[0, 0, pl.ds(pl.multiple_of(j * tk, tk), tk), :]

    def vchunk(j):
        return v_ref[0, :, pl.ds(pl.multiple_of(j * tk, tk), tk)]

    _pipe_qk(kchunk(0), qs_ref, s_ref, bm_ref)

    def step(j):
        _pipe_pv(vchunk(jnp.maximum(j - 1, 0)), p_ref, alpha_ref, acc_ref)
        _pipe_sm(0.0, s_ref, bm_ref, m_ref, l_ref, alpha_ref, p_ref)
        _pipe_qk(kchunk(j + 1), qs_ref, s_ref, bm_ref)

    def group(jg, carry):
        for u in range(C_UNROLL):
            step(C_UNROLL * jg + u)
        return carry

    n_full = nk - 1
    lax.fori_loop(0, n_full // C_UNROLL, group, 0)
    for j in range(n_full - n_full % C_UNROLL, n_full):
        step(j)
    if nk > 1:
        _pipe_pv(vchunk(nk - 2), p_ref, alpha_ref, acc_ref)
    _pipe_sm(0.0, s_ref, bm_ref, m_ref, l_ref, alpha_ref, p_ref)
    _pipe_pv(vchunk(nk - 1), p_ref, alpha_ref, acc_ref)
    o = acc_ref[...] / l_ref[...]
    o_ref[0] = jnp.concatenate([o[:, h * tq:(h + 1) * tq] for h in range(g)], axis=0).astype(o_ref.dtype)


def _attn_c(qt, k, vt):
    bsz, _, s_len = qt.shape
    g = C_HEADS // C_KV
    tq = min(TQ_C, s_len)
    tk = min(TK_C, s_len)
    return pl.pallas_call(
        functools.partial(_attn_c_kernel, tk=tk),
        grid=(bsz, C_KV, s_len // tq),
        in_specs=[
            pl.BlockSpec((1, g * HEAD_DIM, tq), lambda b, kv, i: (b, kv, i)),
            pl.BlockSpec((1, 1, s_len, K_LANES), lambda b, kv, i: (b, kv, 0, 0)),
            pl.BlockSpec((1, HEAD_DIM, s_len), lambda b, kv, i: (b, kv, 0)),
        ],
        out_specs=pl.BlockSpec((1, g * HEAD_DIM, tq), lambda b, kv, i: (b, kv, i)),
        out_shape=jax.ShapeDtypeStruct((bsz, C_Q, s_len), BF16),
        scratch_shapes=_pipe_scratch(tk, g * tq),
        compiler_params=_cparams(("arbitrary", "arbitrary", "arbitrary")),
        name="attn_c",
    )(qt, k, vt)


B_WIN = 3
B_TILES = 5
B_UNROLL = 2
C_UNROLL = 1


def _attn_b_kernel(far_ref, lam_ref, q_ref, k_ref, v_ref, bias_ref, gcol_ref, o_ref,
                   qs_ref, s_ref, p_ref, bm_ref, m_ref, l_ref, alpha_ref, acc_ref):
    h = pl.program_id(1)
    i = pl.program_id(2)
    t = q_ref.shape[2]
    nk = k_ref.shape[2] // t
    win = min(B_WIN, nk)
    nfar = nk - win
    q = q_ref[0]
    z = jnp.zeros((B_QK, t), q.dtype)
    qs_ref[0:HEAD_DIM] = jnp.concatenate(
        [jnp.concatenate([q[0:B_QK], z], axis=0), jnp.concatenate([z, q[B_QK:HEAD_DIM]], axis=0)], axis=1)
    qs_ref[HEAD_DIM:K_LANES] = jnp.zeros((K_LANES - HEAD_DIM, 2 * t), qs_ref.dtype)
    _pipe_init(m_ref, l_ref, alpha_ref, acc_ref, p_ref)
    c_neg = far_ref[h, 0]
    c_pos = far_ref[h, 1]
    w0 = jnp.clip(i - 1, 0, nk - win)

    def kchunk(j):
        return k_ref[0, 0, pl.ds(pl.multiple_of(j * t, t), t), :]

    def vchunk(j):
        return v_ref[0, :, pl.ds(pl.multiple_of(j * t, t), t)]

    def far_chunk(f):
        return jnp.where(f < w0, f, f + win)

    def qk_win(j):
        b = bias_ref[0, j - i + B_TILES // 2]
        _pipe_qk(kchunk(j), qs_ref, s_ref, bm_ref, jnp.concatenate([b, b], axis=1))

    def sm(c):
        _pipe_sm(c, s_ref, bm_ref, m_ref, l_ref, alpha_ref, p_ref)

    def pv(j):
        _pipe_pv(vchunk(j), p_ref, alpha_ref, acc_ref)

    qk_win(w0)
    for w in range(win):
        if w > 0:
            pv(w0 + w - 1)
        sm(0.0)
        if w + 1 < win:
            qk_win(w0 + w + 1)
        elif nfar > 0:
            _pipe_qk(kchunk(far_chunk(0)), qs_ref, s_ref, bm_ref)

    def far_step(f):
        pv(jnp.where(f == 0, w0 + win - 1, far_chunk(f - 1)))
        sm(jnp.where(f < w0, c_neg, c_pos))
        _pipe_qk(kchunk(far_chunk(f + 1)), qs_ref, s_ref, bm_ref)

    if nfar > 0:
        n_full = nfar - 1

        def group(fg, carry):
            for u in range(B_UNROLL):
                far_step(B_UNROLL * fg + u)
            return carry

        lax.fori_loop(0, n_full // B_UNROLL, group, 0)
        for f in range(n_full - n_full % B_UNROLL, n_full):
            far_step(f)
        pv(far_chunk(nfar - 2) if nfar > 1 else w0 + win - 1)
        sm(jnp.where(nfar - 1 < w0, c_neg, c_pos))
        pv(far_chunk(nfar - 1))
    else:
        pv(w0 + win - 1)

    o = acc_ref[...] / l_ref[...]
    o = o[:, 0:t] - lam_ref[0] * o[:, t:2 * t]
    ms = jnp.mean(o * o, axis=0, keepdims=True)
    o_ref[0] = (o * lax.rsqrt(ms + EPS) * gcol_ref[...]).astype(o_ref.dtype)


def _attn_b(qt, k, vt, bias_tiles, far, lam, gcol):
    bsz, _, s_len = qt.shape
    t = bias_tiles.shape[-1]
    return pl.pallas_call(
        _attn_b_kernel,
        grid_spec=pltpu.PrefetchScalarGridSpec(
            num_scalar_prefetch=2,
            grid=(bsz, B_HEADS, s_len // t),
            in_specs=[
                pl.BlockSpec((1, HEAD_DIM, t), lambda b, h, i, *_: (b, h, i)),
                pl.BlockSpec((1, 1, s_len, K_LANES), lambda b, h, i, *_: (b, h, 0, 0)),
                pl.BlockSpec((1, HEAD_DIM, s_len), lambda b, h, i, *_: (b, h, 0)),
                pl.BlockSpec((1, B_TILES, t, t), lambda b, h, i, *_: (h, 0, 0, 0)),
                pl.BlockSpec((HEAD_DIM, 1), lambda b, h, i, *_: (0, 0)),
            ],
            out_specs=pl.BlockSpec((1, HEAD_DIM, t), lambda b, h, i, *_: (b, h, i)),
            scratch_shapes=_pipe_scratch(t, 2 * t),
        ),
        out_shape=jax.ShapeDtypeStruct((bsz, B_Q, s_len), BF16),
        compiler_params=_cparams(("arbitrary", "arbitrary", "arbitrary")),
        name="attn_b",
    )(far, lam, qt, k, vt, bias_tiles, gcol)


SUM_ROWS = 16
SHIFT_LIMIT = 100.0
FAST_C_UNROLL = 3
FAST_B_UNROLL = 7


def _fast_qk_exp(kc, qaug_ref, p_ref, tile=None):
    s = _dot(kc, qaug_ref[...])
    if tile is not None:
        s = s + tile
    p_ref[...] = jnp.exp2(s).astype(p_ref.dtype)


def _fast_pv(vc, p_ref, acc_ref, cls):
    va = jnp.concatenate([vc, jnp.ones((SUM_ROWS, vc.shape[1]), vc.dtype)], axis=0)
    acc_ref[cls] = acc_ref[cls] + _dot(va, p_ref[...])


def _fast_scratch(tk, n, nacc):
    return [
        pltpu.VMEM((K_LANES, n), BF16),
        pltpu.VMEM((tk, n), BF16),
        pltpu.VMEM((nacc, HEAD_DIM + SUM_ROWS, n), F32),
    ]


def _shift_rows(shift, n):
    row = lax.broadcasted_iota(jnp.int32, (K_LANES - HEAD_DIM, n), 0)
    return jnp.where(row == 0, -shift, 0.0).astype(BF16)


def _attn_c_fast_kernel(shift_ref, q_ref, k_ref, v_ref, o_ref, qaug_ref, p_ref, acc_ref, *, tk):
    g = C_HEADS // C_KV
    tq = q_ref.shape[2]
    nk = k_ref.shape[2] // tk
    q3 = q_ref[0]
    qaug_ref[0:HEAD_DIM] = jnp.concatenate([q3[h * HEAD_DIM:(h + 1) * HEAD_DIM] for h in range(g)], axis=1)
    qaug_ref[HEAD_DIM:K_LANES] = _shift_rows(shift_ref[0], g * tq)
    acc_ref[...] = jnp.zeros(acc_ref.shape, F32)

    def kchunk(j):
        return k_ref[0, 0, pl.ds(pl.multiple_of(j * tk, tk), tk), :]

    def vchunk(j):
        return v_ref[0, :, pl.ds(pl.multiple_of(j * tk, tk), tk)]

    _fast_qk_exp(kchunk(0), qaug_ref, p_ref)

    def step(j):
        _fast_pv(vchunk(j - 1), p_ref, acc_ref, 0)
        _fast_qk_exp(kchunk(j), qaug_ref, p_ref)

    def group(jg, carry):
        for u in range(FAST_C_UNROLL):
            step(1 + FAST_C_UNROLL * jg + u)
        return carry

    n_full = nk - 1
    lax.fori_loop(0, n_full // FAST_C_UNROLL, group, 0)
    for j in range(1 + n_full - n_full % FAST_C_UNROLL, nk):
        step(j)
    _fast_pv(vchunk(nk - 1), p_ref, acc_ref, 0)
    acc = acc_ref[0]
    o = acc[0:HEAD_DIM] / acc[HEAD_DIM:HEAD_DIM + 1]
    o_ref[0] = jnp.concatenate([o[:, h * tq:(h + 1) * tq] for h in range(g)], axis=0).astype(o_ref.dtype)


def _attn_c_fast(qt, k, vt, shift):
    bsz, _, s_len = qt.shape
    g = C_HEADS // C_KV
    tq = min(TQ_C, s_len)
    tk = min(TK_C, s_len)
    return pl.pallas_call(
        functools.partial(_attn_c_fast_kernel, tk=tk),
        grid_spec=pltpu.PrefetchScalarGridSpec(
            num_scalar_prefetch=1,
            grid=(bsz, C_KV, s_len // tq),
            in_specs=[
                pl.BlockSpec((1, g * HEAD_DIM, tq), lambda b, kv, i, *_: (b, kv, i)),
                pl.BlockSpec((1, 1, s_len, K_LANES), lambda b, kv, i, *_: (b, kv, 0, 0)),
                pl.BlockSpec((1, HEAD_DIM, s_len), lambda b, kv, i, *_: (b, kv, 0)),
            ],
            out_specs=pl.BlockSpec((1, g * HEAD_DIM, tq), lambda b, kv, i, *_: (b, kv, i)),
            scratch_shapes=_fast_scratch(tk, g * tq, 1),
        ),
        out_shape=jax.ShapeDtypeStruct((bsz, C_Q, s_len), BF16),
        compiler_params=_cparams(("arbitrary", "arbitrary", "arbitrary")),
        name="attn_c_fast",
    )(shift, qt, k, vt)


def _attn_b_fast_kernel(far_ref, lam_ref, shift_ref, q_ref, k_ref, v_ref, bias_ref, gcol_ref, o_ref,
                        qaug_ref, p_ref, acc_ref):
    h = pl.program_id(1)
    i = pl.program_id(2)
    t = q_ref.shape[2]
    nk = k_ref.shape[2] // t
    win = min(B_WIN, nk)
    nfar = nk - win
    q = q_ref[0]
    z = jnp.zeros((B_QK, t), q.dtype)
    qaug_ref[0:HEAD_DIM] = jnp.concatenate(
        [jnp.concatenate([q[0:B_QK], z], axis=0), jnp.concatenate([z, q[B_QK:HEAD_DIM]], axis=0)], axis=1)
    qaug_ref[HEAD_DIM:K_LANES] = _shift_rows(shift_ref[h], 2 * t)
    acc_ref[...] = jnp.zeros(acc_ref.shape, F32)
    w0 = jnp.clip(i - 1, 0, nk - win)

    def kchunk(j):
        return k_ref[0, 0, pl.ds(pl.multiple_of(j * t, t), t), :]

    def vchunk(j):
        return v_ref[0, :, pl.ds(pl.multiple_of(j * t, t), t)]

    def far_chunk(f):
        return jnp.where(f < w0, f, f + win)

    def far_cls(f):
        return jnp.where(f < w0, 1, 2)

    def qk_win(j):
        b = bias_ref[0, j - i + B_TILES // 2]
        _fast_qk_exp(kchunk(j), qaug_ref, p_ref, jnp.concatenate([b, b], axis=1))

    qk_win(w0)
    for w in range(1, win):
        _fast_pv(vchunk(w0 + w - 1), p_ref, acc_ref, 0)
        qk_win(w0 + w)
    _fast_pv(vchunk(w0 + win - 1), p_ref, acc_ref, 0)

    if nfar > 0:
        _fast_qk_exp(kchunk(far_chunk(0)), qaug_ref, p_ref)

        def far_step(f):
            _fast_pv(vchunk(far_chunk(f - 1)), p_ref, acc_ref, far_cls(f - 1))
            _fast_qk_exp(kchunk(far_chunk(f)), qaug_ref, p_ref)

        n_full = nfar - 1

        def group(fg, carry):
            for u in range(FAST_B_UNROLL):
                far_step(1 + FAST_B_UNROLL * fg + u)
            return carry

        lax.fori_loop(0, n_full // FAST_B_UNROLL, group, 0)
        for f in range(1 + n_full - n_full % FAST_B_UNROLL, nfar):
            far_step(f)
        _fast_pv(vchunk(far_chunk(nfar - 1)), p_ref, acc_ref, far_cls(nfar - 1))

    wneg = jnp.exp2(jnp.full((1, 1), far_ref[h, 0], F32))
    wpos = jnp.exp2(jnp.full((1, 1), far_ref[h, 1], F32))
    acc = acc_ref[0] + wneg * acc_ref[1] + wpos * acc_ref[2]
    o = acc[0:HEAD_DIM] / acc[HEAD_DIM:HEAD_DIM + 1]
    o = o[:, 0:t] - lam_ref[0] * o[:, t:2 * t]
    ms = jnp.mean(o * o, axis=0, keepdims=True)
    o_ref[0] = (o * lax.rsqrt(ms + EPS) * gcol_ref[...]).astype(o_ref.dtype)


def _attn_b_fast(qt, k, vt, bias_tiles, far, lam, gcol, shift):
    bsz, _, s_len = qt.shape
    t = bias_tiles.shape[-1]
    return pl.pallas_call(
        _attn_b_fast_kernel,
        grid_spec=pltpu.PrefetchScalarGridSpec(
            num_scalar_prefetch=3,
            grid=(bsz, B_HEADS, s_len // t),
            in_specs=[
                pl.BlockSpec((1, HEAD_DIM, t), lambda b, h, i, *_: (b, h, i)),
                pl.BlockSpec((1, 1, s_len, K_LANES), lambda b, h, i, *_: (b, h, 0, 0)),
                pl.BlockSpec((1, HEAD_DIM, s_len), lambda b, h, i, *_: (b, h, 0)),
                pl.BlockSpec((1, B_TILES, t, t), lambda b, h, i, *_: (h, 0, 0, 0)),
                pl.BlockSpec((HEAD_DIM, 1), lambda b, h, i, *_: (0, 0)),
            ],
            out_specs=pl.BlockSpec((1, HEAD_DIM, t), lambda b, h, i, *_: (b, h, i)),
            scratch_shapes=_fast_scratch(t, 2 * t, 3),
        ),
        out_shape=jax.ShapeDtypeStruct((bsz, B_Q, s_len), BF16),
        compiler_params=_cparams(("arbitrary", "arbitrary", "arbitrary")),
        name="attn_b_fast",
    )(far, lam, shift, qt, k, vt, bias_tiles, gcol)


def _attn_a_kernel(sink_ref, q_ref, kp_ref, kc_ref, kn_ref, vp_ref, vc_ref, vn_ref, bias_ref, o_ref):
    n = pl.program_id(1)
    nb = pl.num_programs(1)
    g = A_HEADS // A_KV
    blk = A_BLK
    row = lax.broadcasted_iota(jnp.int32, (3 * blk, 1), 0)
    valid = ((row >= blk) | (n > 0)) & ((row < 2 * blk) | (n < nb - 1))
    for kv in range(A_KV):
        q3 = q_ref[0, kv * g * HEAD_DIM:(kv + 1) * g * HEAD_DIM]
        qcat = jnp.concatenate([q3[h * HEAD_DIM:(h + 1) * HEAD_DIM] for h in range(g)], axis=1)
        qcat = jnp.concatenate([qcat, jnp.zeros((K_LANES - HEAD_DIM, g * blk), qcat.dtype)], axis=0)
        kwin = jnp.concatenate([kp_ref[0, kv], kc_ref[0, kv], kn_ref[0, kv]], axis=0)
        hs = slice(kv * HEAD_DIM, (kv + 1) * HEAD_DIM)
        vwin = jnp.concatenate([vp_ref[0, hs], vc_ref[0, hs], vn_ref[0, hs]], axis=1)
        bias = jnp.concatenate([bias_ref[kv * g + h] for h in range(g)], axis=1)
        s = jnp.where(valid, _dot(kwin, qcat) + bias, NEG_BIG)
        sink = jnp.concatenate(
            [jnp.full((1, blk), sink_ref[kv * g + h], F32) for h in range(g)], axis=1)
        m = jnp.maximum(jnp.max(s, axis=0, keepdims=True), sink)
        p = jnp.exp2(s - m)
        denom = jnp.sum(p, axis=0, keepdims=True) + jnp.exp2(sink - m)
        o = _dot(vwin, p.astype(BF16)) / denom
        o_ref[0, kv * g * HEAD_DIM:(kv + 1) * g * HEAD_DIM] = jnp.concatenate(
            [o[:, h * blk:(h + 1) * blk] for h in range(g)], axis=0).astype(o_ref.dtype)


def _attn_a(qt, k, vt, bias, sink):
    bsz, _, s_len = qt.shape
    nb = s_len // A_BLK

    def kspec(off):
        return pl.BlockSpec((1, A_KV, A_BLK, K_LANES),
                            lambda b, n, *_: (b, 0, jnp.clip(n + off, 0, nb - 1), 0))

    def vspec(off):
        return pl.BlockSpec((1, A_V, A_BLK), lambda b, n, *_: (b, 0, jnp.clip(n + off, 0, nb - 1)))

    return pl.pallas_call(
        _attn_a_kernel,
        grid_spec=pltpu.PrefetchScalarGridSpec(
            num_scalar_prefetch=1,
            grid=(bsz, nb),
            in_specs=[
                pl.BlockSpec((1, A_Q, A_BLK), lambda b, n, *_: (b, 0, n)),
                kspec(-1), kspec(0), kspec(1), vspec(-1), vspec(0), vspec(1),
                pl.BlockSpec((A_HEADS, 3 * A_BLK, A_BLK), lambda b, n, *_: (0, 0, 0)),
            ],
            out_specs=pl.BlockSpec((1, A_Q, A_BLK), lambda b, n, *_: (b, 0, n)),
        ),
        out_shape=jax.ShapeDtypeStruct((bsz, A_Q, s_len), BF16),
        compiler_params=_cparams(("arbitrary", "arbitrary")),
        name="attn_a",
    )(sink, qt, k, k, k, vt, vt, vt, bias)


def _outproj_kernel(oa_ref, ob_ref, oc_ref, wa_ref, wb_ref, wc_ref, x_ref, mod_ref, g_ref,
                    wrh_ref, wrl_ref, x1_ref, h2_ref, aff_ref):
    def tn(a, b):
        return lax.dot_general(a, b, (((0,), (0,)), ((), ())), preferred_element_type=F32)

    out = tn(oa_ref[0], wa_ref[...]) + tn(ob_ref[0], wb_ref[...]) + tn(oc_ref[0], wc_ref[...])
    mod = mod_ref[0]
    d = out.shape[1]
    gt_a = mod[:, 2 * d:3 * d]
    sh_f, sc_f = mod[:, 3 * d:4 * d], mod[:, 4 * d:5 * d]
    x1 = x_ref[...] + gt_a * out
    x1_ref[...] = x1
    ms = jnp.mean(x1 * x1, axis=-1, keepdims=True)
    h2 = x1 * lax.rsqrt(ms + EPS) * g_ref[...]
    h2 = h2 * (1.0 + sc_f) + sh_f
    for c in range(d // LANES):
        h2_ref[:, c, :] = h2[:, c * LANES:(c + 1) * LANES]
    hi = h2.astype(BF16)
    lo = (h2 - hi.astype(F32)).astype(BF16)
    lt = _dot_nt(wrh_ref[...], hi) + _dot_nt(wrh_ref[...], lo) + _dot_nt(wrl_ref[...], hi)
    lt = lt - jnp.max(lt, axis=0, keepdims=True)
    e = jnp.exp(lt)
    aff_ref[...] = e / jnp.sum(e, axis=0, keepdims=True)


def _outproj(oa, ob, oc, wa, wb, wc, x2d, mod3, g_ffn, wr_hi, wr_lo, s_len):
    n, d = x2d.shape
    tm = min(TM_PROJ, s_len)
    spb = s_len // tm

    def ot_spec(rows_):
        return pl.BlockSpec((1, rows_, tm), lambda i: (i // spb, 0, i % spb))

    def full(shape):
        return pl.BlockSpec(shape, lambda i: tuple(0 for _ in shape))

    return pl.pallas_call(
        _outproj_kernel,
        grid=(n // tm,),
        in_specs=[
            ot_spec(A_Q), ot_spec(B_Q), ot_spec(C_Q),
            full((A_Q, d)), full((B_Q, d)), full((C_Q, d)),
            pl.BlockSpec((tm, d), lambda i: (i, 0)),
            pl.BlockSpec((1, 1, N_MOD * d), lambda i: (i // spb, 0, 0)),
            full((1, d)), full((N_EXPERTS, d)), full((N_EXPERTS, d)),
        ],
        out_specs=[
            pl.BlockSpec((tm, d), lambda i: (i, 0)),
            pl.BlockSpec((tm, d // LANES, LANES), lambda i: (i, 0, 0)),
            pl.BlockSpec((N_EXPERTS, tm), lambda i: (0, i)),
        ],
        out_shape=[
            jax.ShapeDtypeStruct((n, d), F32),
            jax.ShapeDtypeStruct((n, d // LANES, LANES), F32),
            jax.ShapeDtypeStruct((N_EXPERTS, n), F32),
        ],
        compiler_params=_cparams(("arbitrary",)),
        name="outproj",
    )(oa, ob, oc, wa, wb, wc, x2d, mod3, g_ffn, wr_hi, wr_lo)


def _routing_kernel(aff_ref, idx_ref, gate_ref, pexc_ref, thr_ref, need_ref, *, cap, pc):
    ne, nblk, _ = aff_ref.shape
    bits = lax.bitcast_convert_type(aff_ref[...], jnp.int32)

    def bis(it, t):
        cand = t | jnp.left_shift(jnp.int32(1), 30 - it)
        cnt = jnp.sum((bits >= cand).astype(F32), axis=(1, 2), keepdims=True)
        return jnp.where(cnt >= cap, cand, t)

    thr = lax.fori_loop(0, 31, bis, jnp.zeros((ne, 1, 1), jnp.int32))
    n_gt = jnp.sum((bits > thr).astype(F32), axis=(1, 2), keepdims=True)
    thr_ref[...] = jnp.broadcast_to(thr, thr_ref.shape)
    need_ref[...] = jnp.broadcast_to(cap - n_gt, need_ref.shape)

    def tri(nn, fn):
        a = lax.broadcasted_iota(jnp.int32, (nn, nn), 0)
        b = lax.broadcasted_iota(jnp.int32, (nn, nn), 1)
        return jnp.where(fn(a, b), 1.0, 0.0).astype(BF16)

    ones_l = jnp.ones((LANES, LANES), BF16)
    su_l = tri(LANES, lambda a, b: a < b)
    li_l = tri(LANES, lambda a, b: b <= a)
    sl_b = tri(nblk, lambda a, b: b < a)
    li_b = tri(nblk, lambda a, b: b <= a)
    blk_col = lax.broadcasted_iota(jnp.int32, (nblk, 1), 0).astype(F32)
    lane_col = lax.broadcasted_iota(jnp.int32, (LANES, 1), 0).astype(F32)

    def per_expert(e, carry):
        a = aff_ref[e]
        be = lax.bitcast_convert_type(a, jnp.int32)
        te = thr_ref[e]
        gt = be > te
        eq = (be == te)
        eqb = jnp.where(eq, 1.0, 0.0).astype(BF16)
        rank = _dot(eqb, su_l) + _dot(sl_b, _dot(eqb, ones_l).astype(BF16))
        sel = gt | (eq & (rank < need_ref[e]))
        mb = jnp.where(sel, 1.0, 0.0).astype(BF16)
        cnt_b = _dot(mb, ones_l)
        pinc = _dot(li_b, cnt_b.astype(BF16))
        pinc_col = pinc[:, 0:1]
        pexc_col = pinc_col - cnt_b[:, 0:1]
        pexc_ref[e] = (pinc - cnt_b).T[0:1].astype(jnp.int32)
        mt = jnp.where(sel, 1.0, 0.0).T.astype(BF16)
        at = a.T
        at_hi = at.astype(BF16)
        at_lo = (at - at_hi.astype(F32)).astype(BF16)
        for c in range(cap // pc):
            p_row = (lax.broadcasted_iota(jnp.int32, (1, pc), 1) + c * pc).astype(F32)
            blk_row = jnp.sum(jnp.where(pinc_col <= p_row, 1.0, 0.0), axis=0, keepdims=True)
            oh = blk_col == blk_row
            ohb = jnp.where(oh, 1.0, 0.0).astype(BF16)
            r_row = p_row - jnp.sum(jnp.where(oh, pexc_col, 0.0), axis=0, keepdims=True)
            gt_rows = _dot(mt, ohb)
            pc_incl = _dot(li_l, gt_rows.astype(BF16))
            lane_row = jnp.sum(jnp.where(pc_incl <= r_row, 1.0, 0.0), axis=0, keepdims=True)
            idx_ref[e, :, c * pc:(c + 1) * pc] = (blk_row * LANES + lane_row).astype(jnp.int32)
            ag = _dot(at_hi, ohb) + _dot(at_lo, ohb)
            gate_ref[e, :, c * pc:(c + 1) * pc] = jnp.sum(
                jnp.where(lane_col == lane_row, ag, 0.0), axis=0, keepdims=True)
        return carry

    lax.fori_loop(0, ne, per_expert, 0)


def _routing(aff3, cap):
    ne, nblk, _ = aff3.shape
    pc = min(ROUTE_PC, cap)
    return pl.pallas_call(
        functools.partial(_routing_kernel, cap=cap, pc=pc),
        grid=(1,),
        in_specs=[pl.BlockSpec((ne, nblk, LANES), lambda i: (0, 0, 0))],
        out_specs=[
            pl.BlockSpec((ne, 1, cap), lambda i: (0, 0, 0)),
            pl.BlockSpec((ne, 1, cap), lambda i: (0, 0, 0)),
            pl.BlockSpec((ne, 1, nblk), lambda i: (0, 0, 0)),
        ],
        out_shape=[
            jax.ShapeDtypeStruct((ne, 1, cap), jnp.int32),
            jax.ShapeDtypeStruct((ne, 1, cap), F32),
            jax.ShapeDtypeStruct((ne, 1, nblk), jnp.int32),
        ],
        scratch_shapes=[pltpu.VMEM((ne, 1, LANES), jnp.int32), pltpu.VMEM((ne, 1, LANES), F32)],
        compiler_params=_cparams(("arbitrary",)),
        name="routing",
    )(aff3)


TOK_LANES = LANES
TOK_RADIX = 256


def _moe_kernel(idx_ref, idxn_ref, gate_ref, h_hbm, wg_ref, wu_ref, wd_ref, y_ref,
                idx_smem, xbuf, sems):
    ch = xbuf.shape[1]
    d = xbuf.shape[2] * xbuf.shape[3]
    nsteps = pl.num_programs(0) * pl.num_programs(1)
    s = pl.program_id(0) * pl.num_programs(1) + pl.program_id(1)
    slot = s % 2
    nslot = 1 - slot

    def load_idx(src_ref, sl):
        cp = pltpu.make_async_copy(src_ref.at[0, 0], idx_smem.at[sl], sems.at[2])
        cp.start()
        cp.wait()

    def row_copy(sl, r, t):
        return pltpu.make_async_copy(h_hbm.at[pl.ds(t, 1)], xbuf.at[sl, pl.ds(r, 1)], sems.at[sl])

    @pl.when(s == 0)
    def _():
        load_idx(idx_ref, 0)
        for r in range(ch):
            row_copy(0, r, idx_smem[0, r]).start()

    load_idx(idxn_ref, nslot)
    for r in range(ch):
        row_copy(slot, r, 0).wait()
    for r in range(ch):
        row_copy(nslot, r, idx_smem[nslot, r]).start()

    x = jnp.concatenate([xbuf[slot, :, c, :] for c in range(xbuf.shape[2])], axis=1).astype(BF16)
    gg = _dot(x, wg_ref[0])
    uu = _dot(x, wu_ref[0])
    hid = (gg * (1.0 / (1.0 + jnp.exp(-gg))) * uu).astype(BF16)
    y = _dot(hid, wd_ref[0])
    sub = lax.broadcasted_iota(jnp.int32, (LANES, ch), 0)
    cols = jnp.where(sub == 0, gate_ref[0], jnp.where(sub == 1, idx_ref[0].astype(F32), 0.0)).T
    y_ref[:, 0:d] = (y * cols[:, 0:1]).astype(y_ref.dtype)
    tok = cols[:, 1:2]
    hi = jnp.floor(tok * (1.0 / TOK_RADIX))
    lane = lax.broadcasted_iota(jnp.int32, (ch, TOK_LANES), 1)
    y_ref[:, d:d + TOK_LANES] = jnp.where(
        lane == 0, hi, jnp.where(lane == 1, tok - hi * TOK_RADIX, 0.0)).astype(y_ref.dtype)

    @pl.when(s == nsteps - 1)
    def _():
        for r in range(ch):
            row_copy(nslot, r, 0).wait()


def _moe(idx3, gate3, h2, wg, wu, wd, cap):
    n, dt, _ = h2.shape
    d = dt * LANES
    ne, _, ff = wg.shape
    ch = min(MOE_CH, cap)
    nch = cap // ch
    nsteps = ne * nch
    return pl.pallas_call(
        _moe_kernel,
        grid=(ne, nch),
        in_specs=[
            pl.BlockSpec((1, 1, ch), lambda e, c: (e * nch + c, 0, 0)),
            pl.BlockSpec((1, 1, ch), lambda e, c: (jnp.minimum(e * nch + c + 1, nsteps - 1), 0, 0)),
            pl.BlockSpec((1, 1, ch), lambda e, c: (e * nch + c, 0, 0)),
            pl.BlockSpec(memory_space=pl.ANY),
            pl.BlockSpec((1, d, ff), lambda e, c: (e, 0, 0)),
            pl.BlockSpec((1, d, ff), lambda e, c: (e, 0, 0)),
            pl.BlockSpec((1, ff, d), lambda e, c: (e, 0, 0)),
        ],
        out_specs=pl.BlockSpec((ch, d + TOK_LANES), lambda e, c: (e * nch + c, 0)),
        out_shape=jax.ShapeDtypeStruct((ne * cap, d + TOK_LANES), BF16),
        scratch_shapes=[
            pltpu.SMEM((2, ch), jnp.int32),
            pltpu.VMEM((2, ch, dt, LANES), F32),
            pltpu.SemaphoreType.DMA((3,)),
        ],
        compiler_params=_cparams(("arbitrary", "arbitrary")),
        name="moe_ffn",
    )(idx3.reshape(nsteps, 1, ch), idx3.reshape(nsteps, 1, ch), gate3.reshape(nsteps, 1, ch), h2, wg, wu, wd)


def _combine_kernel(starts_ref, y_hbm, x_ref, mod_ref, o_ref, ybuf, xtra, sems, *, cap, ne):
    tt, d = x_ref.shape
    wr = ybuf.shape[2]
    nt = pl.num_programs(0)
    t = pl.program_id(0)
    slot = t % 2
    nslot = 1 - slot

    def bounds(e, tile):
        lo = starts_ref[e * (nt + 1) + tile]
        hi = starts_ref[e * (nt + 1) + tile + 1]
        base = jnp.minimum((lo // 16) * 16, cap - wr)
        return lo, hi, base

    def win_copy(e, base, sl):
        return pltpu.make_async_copy(y_hbm.at[pl.ds(pl.multiple_of(e * cap + base, 16), wr)],
                                     ybuf.at[sl, e], sems.at[sl])

    def fetch(tile, sl):
        for e in range(ne):
            win_copy(e, bounds(e, tile)[2], sl).start()

    @pl.when(t == 0)
    def _():
        fetch(0, 0)

    for e in range(ne):
        win_copy(e, 0, slot).wait()
    fetch(jnp.minimum(t + 1, nt - 1), nslot)

    tok0 = (t * tt).astype(F32)
    lane_tok = lax.broadcasted_iota(jnp.int32, (1, tt), 1).astype(F32) + tok0
    row = lax.broadcasted_iota(jnp.int32, (wr, 1), 0)

    def contribution(yb, first, lo_eff, hi):
        tokc = yb[:, d:d + 1].astype(F32) * TOK_RADIX + yb[:, d + 1:d + 2].astype(F32)
        pos = row + first
        oh = jnp.where((pos >= lo_eff) & (pos < hi) & (tokc == lane_tok), 1.0, 0.0).astype(BF16)
        return lax.dot_general(oh, yb[:, 0:d], (((0,), (0,)), ((), ())), preferred_element_type=F32)

    acc = jnp.zeros((tt, d), F32)
    for e in range(ne):
        lo, hi, base = bounds(e, t)
        acc = acc + contribution(ybuf[slot, e], base, lo, hi)
    o_ref[...] = x_ref[...] + mod_ref[0][:, 5 * d:6 * d] * acc

    for e in range(ne):
        lo, hi, base = bounds(e, t)
        n_more = jnp.maximum(hi - (base + wr) + wr - 1, 0) // wr

        def more(k, carry, e=e, hi=hi, base=base):
            done = base + wr * (k + 1)
            first = jnp.minimum(done, cap - wr)
            cp = pltpu.make_async_copy(y_hbm.at[pl.ds(pl.multiple_of(e * cap + first, 16), wr)], xtra, sems.at[2])
            cp.start()
            cp.wait()
            o_ref[...] = o_ref[...] + mod_ref[0][:, 5 * d:6 * d] * contribution(xtra[...], first, done, hi)
            return carry

        lax.fori_loop(0, n_more, more, 0)

    @pl.when(t == nt - 1)
    def _():
        for e in range(ne):
            win_copy(e, 0, nslot).wait()


def _combine(starts, y, x1, mod3, s_len, cap):
    n, d = x1.shape
    ne = y.shape[0] // cap
    tt = min(COMBINE_TT, s_len)
    spb = s_len // tt
    wr = min(COMBINE_WR, cap)
    return pl.pallas_call(
        functools.partial(_combine_kernel, cap=cap, ne=ne),
        grid_spec=pltpu.PrefetchScalarGridSpec(
            num_scalar_prefetch=1,
            grid=(n // tt,),
            in_specs=[
                pl.BlockSpec(memory_space=pl.ANY),
                pl.BlockSpec((tt, d), lambda i, *_: (i, 0)),
                pl.BlockSpec((1, 1, N_MOD * d), lambda i, *_: (i // spb, 0, 0)),
            ],
            out_specs=pl.BlockSpec((tt, d), lambda i, *_: (i, 0)),
            scratch_shapes=[
                pltpu.VMEM((2, ne, wr, d + TOK_LANES), BF16),
                pltpu.VMEM((wr, d + TOK_LANES), BF16),
                pltpu.SemaphoreType.DMA((3,)),
            ],
        ),
        out_shape=jax.ShapeDtypeStruct((n, d), F32),
        compiler_params=_cparams(("arbitrary",)),
        name="combine",
    )(starts, y, x1, mod3)


def _t5_bucket(rel):
    half = NUM_BUCKETS // 2
    max_exact = half // 2
    n = jnp.abs(rel)
    nf = jnp.maximum(n, 1).astype(F32)
    large = max_exact + (jnp.log(nf / max_exact) / math.log(MAX_DISTANCE / max_exact)
                         * (half - max_exact)).astype(jnp.int32)
    large = jnp.minimum(large, half - 1)
    return jnp.where(rel > 0, half, 0) + jnp.where(n < max_exact, n, large)


def _rope_table(s_len):
    quarter = HEAD_DIM // 4
    freqs = ROPE_THETA ** (-jnp.arange(quarter, dtype=F32) / quarter)
    t = jnp.arange(s_len)
    ang_r = freqs[:, None] * (t // GRID_W).astype(F32)[None, :]
    ang_c = freqs[:, None] * (t % GRID_W).astype(F32)[None, :]
    return jnp.concatenate([jnp.cos(ang_r), jnp.sin(ang_r), jnp.cos(ang_c), jnp.sin(ang_c)], axis=0)


def _toeplitz(fn, nj, ni):
    period = nj + ni
    m = jnp.arange(period)
    b = fn(jnp.where(m < nj, m, m - period))
    flat = jnp.tile(b, ni)[..., :ni * (period - 1)]
    rows = flat.reshape(b.shape[:-1] + (ni, period - 1))
    return jnp.swapaxes(rows[..., :nj], -1, -2)


def _bias_tables(rel_bias, t_b):
    def bias_of(rel, heads):
        return rel_bias[_t5_bucket(rel)][:, heads].T * LOG2E

    def fn_a(r):
        rel = r - A_BLK
        return jnp.where(jnp.abs(rel) <= WINDOW, bias_of(rel, slice(0, A_HEADS)), NEG_BIG)

    bias_a = _toeplitz(fn_a, 3 * A_BLK, A_BLK).astype(F32)
    tiles = [
        _toeplitz(lambda r, dd=dd: bias_of(dd * t_b + r, slice(A_HEADS, None)), t_b, t_b)
        for dd in range(-(B_TILES // 2), B_TILES // 2 + 1)
    ]
    bias_b = jnp.stack(tiles, axis=1).astype(F32)
    half = NUM_BUCKETS // 2
    far = jnp.stack([rel_bias[half - 1, A_HEADS:], rel_bias[NUM_BUCKETS - 1, A_HEADS:]], axis=1) * LOG2E
    return bias_a, bias_b, far.astype(F32)


def _gain_column(ga, gb, gc):
    sa = HEAD_DIM ** -0.5 * LOG2E
    sb = B_QK ** -0.5 * LOG2E
    parts = [
        jnp.tile(ga[0] * sa, A_HEADS), jnp.tile(ga[1], A_KV), jnp.ones((A_V,), F32),
        jnp.tile(gb[0] * sb, 2 * B_HEADS), jnp.tile(gb[1], 2 * B_HEADS), jnp.ones((B_V,), F32),
        jnp.tile(gc[0] * sa, C_HEADS), jnp.tile(gc[1], C_KV), jnp.ones((C_V,), F32),
    ]
    return jnp.concatenate(parts).reshape(IN_COLS, 1).astype(F32)


def _trunk(x, mod_l, prep, s_len, bsz):
    n = bsz * s_len
    d = x.shape[-1]
    x2d = x.reshape(n, d)
    t_b = min(T_B, s_len)
    cap = CAPACITY_FACTOR * n // N_EXPERTS
    cs = _rope_table(s_len)
    for l, lp in enumerate(prep["layers"]):
        mod3 = mod_l[l][:, None, :]
        qa, ka, va, qb, kb, vb, qc, kc, vc = _inproj(
            x2d, mod3, lp["g_attn"], lp["w_in_t"], lp["gcol"], cs, bsz, s_len)
        oa = _attn_a(qa, ka, va, prep["bias_a"], lp["sink"])
        bias_b = prep["bias_b"][t_b]
        ob = lax.cond(
            lp["b_bounded"],
            lambda q, k, v: _attn_b_fast(q, k, v, bias_b, prep["far_b"], lp["lam"], lp["subln_col"], lp["b_shift"]),
            lambda q, k, v: _attn_b(q, k, v, bias_b, prep["far_b"], lp["lam"], lp["subln_col"]),
            qb, kb, vb)
        oc = lax.cond(
            lp["c_bounded"],
            lambda q, k, v: _attn_c_fast(q, k, v, lp["c_shift"]),
            _attn_c,
            qc, kc, vc)
        x1, h2, aff = _outproj(oa, ob, oc, lp["wo_a"], lp["wo_b"], lp["wo_c"], x2d, mod3,
                               lp["g_ffn"], lp["wr_hi"], lp["wr_lo"], s_len)
        idx3, gate3, pexc = _routing(aff.reshape(N_EXPERTS, n // LANES, LANES), cap)
        y = _moe(idx3, gate3, h2, lp["wg"], lp["wu"], lp["wd"], cap)
        tt = min(COMBINE_TT, s_len)
        starts = jnp.concatenate(
            [pexc[:, 0, ::tt // LANES], jnp.full((N_EXPERTS, 1), cap, jnp.int32)], axis=1).reshape(-1)
        x2d = _combine(starts, y, x1, mod3, s_len, cap)
    return x2d.reshape(bsz, s_len, d)


def kernel(x_prompt, x_sample, c_prompt, c_sample, w_mod, b_mod, g_attn, g_ffn, w_in, w_out,
           qk_gain_a, qk_gain_b, qk_gain_c, sink_a, lam_b, subln_b, rel_bias,
           w_router, w_gate, w_up, w_down):
    depth = w_mod.shape[0]
    bp, sp, d = x_prompt.shape
    bs, ss, _ = x_sample.shape
    rows = -(-(bp + bs) // 16) * 16
    c_all = jnp.zeros((rows, d), F32).at[:bp].set(c_prompt).at[bp:bp + bs].set(c_sample)
    mod_all = _modulation(c_all, w_mod, b_mod)
    mod_p, mod_s = mod_all[:, :bp], mod_all[:, bp:bp + bs]

    tbs = sorted({min(T_B, sp), min(T_B, ss)})
    tabs = {t: _bias_tables(rel_bias, t) for t in tbs}
    prep = {"bias_a": tabs[tbs[0]][0], "far_b": tabs[tbs[0]][2],
            "bias_b": {t: tabs[t][1] for t in tbs}, "layers": []}
    for l in range(depth):
        lam_init = 0.8 - 0.6 * math.exp(-0.3 * l)
        lp = lam_b[l].astype(F32)
        lam = jnp.exp(jnp.sum(lp[0] * lp[1])) - jnp.exp(jnp.sum(lp[2] * lp[3])) + lam_init
        wo = w_out[l].astype(BF16)
        wr_t = w_router[l].T
        wr_hi = wr_t.astype(BF16)
        gmax = lambda g_: jnp.max(jnp.abs(g_))
        c_c = 1.02 * HEAD_DIM * gmax(qk_gain_c[l][0]) * gmax(qk_gain_c[l][1]) * (HEAD_DIM ** -0.5 * LOG2E)
        c_b = 1.02 * B_QK * gmax(qk_gain_b[l][0]) * gmax(qk_gain_b[l][1]) * (B_QK ** -0.5 * LOG2E)
        bias_vals = rel_bias[:, A_HEADS:] * LOG2E
        prep["layers"].append({
            "c_shift": c_c.reshape(1).astype(F32),
            "c_bounded": 2.0 * c_c <= SHIFT_LIMIT,
            "b_shift": (c_b + jnp.max(bias_vals, axis=0)).astype(F32),
            "b_bounded": 2.0 * c_b + 2.0 * jnp.max(jnp.abs(bias_vals)) <= SHIFT_LIMIT,
            "g_attn": g_attn[l].reshape(1, d), "g_ffn": g_ffn[l].reshape(1, d),
            "w_in_t": w_in[l].T.astype(BF16),
            "gcol": _gain_column(qk_gain_a[l], qk_gain_b[l], qk_gain_c[l]),
            "sink": (sink_a[l] * LOG2E).astype(F32),
            "lam": lam.reshape(1).astype(F32),
            "subln_col": (subln_b[l] * (1.0 - lam_init)).reshape(HEAD_DIM, 1).astype(F32),
            "wo_a": wo[0:A_Q], "wo_b": wo[A_Q:A_Q + B_Q], "wo_c": wo[A_Q + B_Q:],
            "wr_hi": wr_hi, "wr_lo": (wr_t - wr_hi.astype(F32)).astype(BF16),
            "wg": w_gate[l].astype(BF16), "wu": w_up[l].astype(BF16), "wd": w_down[l].astype(BF16),
        })
    y_p = _trunk(x_prompt, mod_p, prep, sp, bp)
    y_s = _trunk(x_sample, mod_s, prep, ss, bs)
    return (y_p, y_s)
```

```python
import functools
import math

import jax
import jax.numpy as jnp
from jax import lax
from jax.experimental import pallas as pl
from jax.experimental.pallas import tpu as pltpu

F32 = jnp.float32
BF16 = jnp.bfloat16

HEAD_DIM = 64
A_HEADS, A_KV = 6, 2
B_HEADS, B_QK = 4, 32
C_HEADS, C_KV = 6, 2
WINDOW = 128
A_BLK = 128
NUM_BUCKETS, MAX_DISTANCE = 32, 128
ROPE_THETA = 10000.0
GRID_W = 64
N_EXPERTS = 16
CAPACITY_FACTOR = 2
N_MOD = 6
EPS = 1e-6
NEG_BIG = -1e30
LOG2E = math.log2(math.e)

A_Q, A_K, A_V = A_HEADS * HEAD_DIM, A_KV * HEAD_DIM, A_KV * HEAD_DIM
B_Q, B_K, B_V = B_HEADS * HEAD_DIM, B_HEADS * HEAD_DIM, B_HEADS * HEAD_DIM
C_Q, C_K, C_V = C_HEADS * HEAD_DIM, C_KV * HEAD_DIM, C_KV * HEAD_DIM
A_OFF = 0
B_OFF = A_Q + A_K + A_V
C_OFF = B_OFF + B_Q + B_K + B_V
IN_COLS = C_OFF + C_Q + C_K + C_V

LANES = 128
K_LANES = LANES
VMEM_LIMIT_BYTES = 56 * 1024 * 1024

TM_PROJ = 512
TQ_C, TK_C = 512, 1024
T_B = 512
MOE_CH = 256
ROUTE_PC = 512
COMBINE_TT = 512
COMBINE_WR = 128


def _cparams(sem):
    return pltpu.CompilerParams(dimension_semantics=sem, vmem_limit_bytes=VMEM_LIMIT_BYTES)


def _dot(a, b):
    return jnp.dot(a, b, preferred_element_type=F32)


def _dot_nt(a, b):
    return lax.dot_general(a, b, (((1,), (1,)), ((), ())), preferred_element_type=F32)


def _mod_kernel(c_ref, w_ref, b_ref, o_ref):
    c = c_ref[...]
    a = c * (1.0 / (1.0 + jnp.exp(-c)))
    a_hi = a.astype(BF16)
    a_lo = (a - a_hi.astype(F32)).astype(BF16)
    w = w_ref[0]
    w_hi = w.astype(BF16)
    w_lo = (w - w_hi.astype(F32)).astype(BF16)
    acc = _dot(a_hi, w_hi) + _dot(a_lo, w_hi) + _dot(a_hi, w_lo)
    o_ref[0] = acc + b_ref[0]


def _modulation(c_all, w_mod, b_mod):
    depth, d, nm = w_mod.shape
    rows = c_all.shape[0]
    bn = 1024
    return pl.pallas_call(
        _mod_kernel,
        grid=(depth, nm // bn),
        in_specs=[
            pl.BlockSpec((rows, d), lambda l, j: (0, 0)),
            pl.BlockSpec((1, d, bn), lambda l, j: (l, 0, j)),
            pl.BlockSpec((1, 1, bn), lambda l, j: (l, 0, j)),
        ],
        out_specs=pl.BlockSpec((1, rows, bn), lambda l, j: (l, 0, j)),
        out_shape=jax.ShapeDtypeStruct((depth, rows, nm), F32),
        compiler_params=_cparams(("arbitrary", "arbitrary")),
        name="modulation",
    )(c_all, w_mod, b_mod.reshape(depth, 1, nm))


def _segnorm(rows, seglen):
    r, t = rows.shape
    r3 = rows.reshape(r // seglen, seglen, t)
    ms = jnp.mean(r3 * r3, axis=1, keepdims=True)
    return (r3 * lax.rsqrt(ms + EPS)).reshape(r, t)


def _rope(rows, cs):
    r, t = rows.shape
    x = rows.reshape(r // HEAD_DIM, HEAD_DIM, t)
    cr, sr, cc, sc = cs[None, 0:16], cs[None, 16:32], cs[None, 32:48], cs[None, 48:64]
    ar, br, ac, bc = x[:, 0:16], x[:, 16:32], x[:, 32:48], x[:, 48:64]
    out = jnp.concatenate(
        [ar * cr - br * sr, br * cr + ar * sr, ac * cc - bc * sc, bc * cc + ac * sc], axis=1)
    return out.reshape(r, t)


def _emit_k(k_ref, kt):
    lane = lax.broadcasted_iota(jnp.int32, (1, K_LANES), 1)
    pad = jnp.where(lane == HEAD_DIM, 1.0, 0.0)
    for p in range(kt.shape[0] // LANES):
        tok = kt[p * LANES:(p + 1) * LANES].T
        k_ref[0, 2 * p] = jnp.where(lane < HEAD_DIM, tok, pad).astype(k_ref.dtype)
        k_ref[0, 2 * p + 1] = jnp.where(lane < HEAD_DIM, pltpu.roll(tok, HEAD_DIM, 1), pad).astype(k_ref.dtype)


def _emit_vt(v_ref, vt):
    v_ref[0] = vt.astype(v_ref.dtype)


def _inproj_kernel(x_ref, mod_ref, g_ref, w_ref, gcol_ref, cs_ref,
                   qa_ref, ka_ref, va_ref, qb_ref, kb_ref, vb_ref, qc_ref, kc_ref, vc_ref):
    x = x_ref[...]
    mod = mod_ref[0]
    d = x.shape[1]
    sh, sc = mod[:, 0:d], mod[:, d:2 * d]
    ms = jnp.mean(x * x, axis=-1, keepdims=True)
    h = x * lax.rsqrt(ms + EPS) * g_ref[...]
    h = (h * (1.0 + sc) + sh).astype(BF16)
    pt = _dot_nt(w_ref[...], h)
    gcol = gcol_ref[...]
    cs = cs_ref[...]

    def rows(off, n):
        return pt[off:off + n], gcol[off:off + n]

    q, g = rows(A_OFF, A_Q)
    qa_ref[0] = (_segnorm(q, HEAD_DIM) * g).astype(qa_ref.dtype)
    k, g = rows(A_OFF + A_Q, A_K)
    _emit_k(ka_ref, _segnorm(k, HEAD_DIM) * g)
    _emit_vt(va_ref, pt[A_OFF + A_Q + A_K:A_OFF + A_Q + A_K + A_V])
    q, g = rows(B_OFF, B_Q)
    qb_ref[0] = (_segnorm(q, B_QK) * g).astype(qb_ref.dtype)
    k, g = rows(B_OFF + B_Q, B_K)
    _emit_k(kb_ref, _segnorm(k, B_QK) * g)
    _emit_vt(vb_ref, pt[B_OFF + B_Q + B_K:B_OFF + B_Q + B_K + B_V])
    q, g = rows(C_OFF, C_Q)
    qc_ref[0] = _rope(_segnorm(q, HEAD_DIM) * g, cs).astype(qc_ref.dtype)
    k, g = rows(C_OFF + C_Q, C_K)
    _emit_k(kc_ref, _rope(_segnorm(k, HEAD_DIM) * g, cs))
    _emit_vt(vc_ref, pt[C_OFF + C_Q + C_K:C_OFF + C_Q + C_K + C_V])


def _inproj(x2d, mod3, g_attn, w_in_t, gcol, cs, bsz, s_len):
    n, d = x2d.shape
    tm = min(TM_PROJ, s_len)
    spb = s_len // tm

    def qt_spec(rows_):
        return pl.BlockSpec((1, rows_, tm), lambda i: (i // spb, 0, i % spb))

    def k_spec(nh):
        return pl.BlockSpec((1, nh, tm, K_LANES), lambda i: (i // spb, 0, i % spb, 0))

    def qt_shape(rows_):
        return jax.ShapeDtypeStruct((bsz, rows_, s_len), BF16)

    def k_shape(nh):
        return jax.ShapeDtypeStruct((bsz, nh, s_len, K_LANES), BF16)

    return pl.pallas_call(
        _inproj_kernel,
        grid=(n // tm,),
        in_specs=[
            pl.BlockSpec((tm, d), lambda i: (i, 0)),
            pl.BlockSpec((1, 1, N_MOD * d), lambda i: (i // spb, 0, 0)),
            pl.BlockSpec((1, d), lambda i: (0, 0)),
            pl.BlockSpec((IN_COLS, d), lambda i: (0, 0)),
            pl.BlockSpec((IN_COLS, 1), lambda i: (0, 0)),
            pl.BlockSpec((HEAD_DIM, tm), lambda i: (0, i % spb)),
        ],
        out_specs=[
            qt_spec(A_Q), k_spec(A_KV), qt_spec(A_V),
            qt_spec(B_Q), k_spec(B_HEADS), qt_spec(B_V),
            qt_spec(C_Q), k_spec(C_KV), qt_spec(C_V),
        ],
        out_shape=[
            qt_shape(A_Q), k_shape(A_KV), qt_shape(A_V),
            qt_shape(B_Q), k_shape(B_HEADS), qt_shape(B_V),
            qt_shape(C_Q), k_shape(C_KV), qt_shape(C_V),
        ],
        compiler_params=_cparams(("arbitrary",)),
        name="inproj",
    )(x2d, mod3, g_attn, w_in_t, gcol, cs)


def _pipe_init(m_ref, l_ref, alpha_ref, acc_ref, p_ref):
    m_ref[...] = jnp.full(m_ref.shape, NEG_BIG, F32)
    l_ref[...] = jnp.zeros(l_ref.shape, F32)
    alpha_ref[...] = jnp.ones(alpha_ref.shape, F32)
    acc_ref[...] = jnp.zeros(acc_ref.shape, F32)
    p_ref[...] = jnp.zeros(p_ref.shape, p_ref.dtype)


def _pipe_pv(vt, p_ref, alpha_ref, acc_ref):
    acc_ref[...] = alpha_ref[...] * acc_ref[...] + _dot(vt, p_ref[...])


def _pipe_sm(c, s_ref, bm_ref, m_ref, l_ref, alpha_ref, p_ref):
    m_old = m_ref[...]
    m_new = jnp.maximum(m_old, bm_ref[...] + c)
    alpha = jnp.exp2(m_old - m_new)
    p = jnp.exp2(s_ref[...] - (m_new - c))
    l_ref[...] = alpha * l_ref[...] + jnp.sum(p, axis=0, keepdims=True)
    p_ref[...] = p.astype(p_ref.dtype)
    m_ref[...] = m_new
    alpha_ref[...] = alpha


def _pipe_qk(kc, q_ref, s_ref, bm_ref, tile=None):
    s = _dot(kc, q_ref[...])
    if tile is not None:
        s = s + tile
    s_ref[...] = s
    bm_ref[...] = jnp.max(s, axis=0, keepdims=True)


def _pipe_scratch(tk, n):
    return [
        pltpu.VMEM((K_LANES, n), BF16),
        pltpu.VMEM((tk, n), F32),
        pltpu.VMEM((tk, n), BF16),
        pltpu.VMEM((1, n), F32),
        pltpu.VMEM((1, n), F32),
        pltpu.VMEM((1, n), F32),
        pltpu.VMEM((1, n), F32),
        pltpu.VMEM((HEAD_DIM, n), F32),
    ]


def _attn_c_kernel(q_ref, k_ref, v_ref, o_ref, qs_ref, s_ref, p_ref, bm_ref, m_ref, l_ref, alpha_ref, acc_ref,
                   *, tk):
    g = C_HEADS // C_KV
    tq = q_ref.shape[2]
    nk = k_ref.shape[2] // tk
    q3 = q_ref[0]
    qs_ref[0:HEAD_DIM] = jnp.concatenate([q3[h * HEAD_DIM:(h + 1) * HEAD_DIM] for h in range(g)], axis=1)
    qs_ref[HEAD_DIM:K_LANES] = jnp.zeros((K_LANES - HEAD_DIM, g * tq), qs_ref.dtype)
    _pipe_init(m_ref, l_ref, alpha_ref, acc_ref, p_ref)

    def kchunk(j):
        return k_ref[0, 0, pl.ds(pl.multiple_of(j * tk, tk), tk), :]

    def vchunk(j):
        return v_ref[0, :, pl.ds(pl.multiple_of(j * tk, tk), tk)]

    _pipe_qk(kchunk(0), qs_ref, s_ref, bm_ref)

    def step(j):
        _pipe_pv(vchunk(jnp.maximum(j - 1, 0)), p_ref, alpha_ref, acc_ref)
        _pipe_sm(0.0, s_ref, bm_ref, m_ref, l_ref, alpha_ref, p_ref)
        _pipe_qk(kchunk(j + 1), qs_ref, s_ref, bm_ref)

    def group(jg, carry):
        for u in range(C_UNROLL):
            step(C_UNROLL * jg + u)
        return carry

    n_full = nk - 1
    lax.fori_loop(0, n_full // C_UNROLL, group, 0)
    for j in range(n_full - n_full % C_UNROLL, n_full):
        step(j)
    if nk > 1:
        _pipe_pv(vchunk(nk - 2), p_ref, alpha_ref, acc_ref)
    _pipe_sm(0.0, s_ref, bm_ref, m_ref, l_ref, alpha_ref, p_ref)
    _pipe_pv(vchunk(nk - 1), p_ref, alpha_ref, acc_ref)
    o = acc_ref[...] / l_ref[...]
    o_ref[0] = jnp.concatenate([o[:, h * tq:(h + 1) * tq] for h in range(g)], axis=0).astype(o_ref.dtype)


def _attn_c(qt, k, vt):
    bsz, _, s_len = qt.shape
    g = C_HEADS // C_KV
    tq = min(TQ_C, s_len)
    tk = min(TK_C, s_len)
    return pl.pallas_call(
        functools.partial(_attn_c_kernel, tk=tk),
        grid=(bsz, C_KV, s_len // tq),
        in_specs=[
            pl.BlockSpec((1, g * HEAD_DIM, tq), lambda b, kv, i: (b, kv, i)),
            pl.BlockSpec((1, 1, s_len, K_LANES), lambda b, kv, i: (b, kv, 0, 0)),
            pl.BlockSpec((1, HEAD_DIM, s_len), lambda b, kv, i: (b, kv, 0)),
        ],
        out_specs=pl.BlockSpec((1, g * HEAD_DIM, tq), lambda b, kv, i: (b, kv, i)),
        out_shape=jax.ShapeDtypeStruct((bsz, C_Q, s_len), BF16),
        scratch_shapes=_pipe_scratch(tk, g * tq),
        compiler_params=_cparams(("arbitrary", "arbitrary", "arbitrary")),
        name="attn_c",
    )(qt, k, vt)


B_WIN = 3
B_TILES = 5
B_UNROLL = 2
C_UNROLL = 1


def _attn_b_kernel(far_ref, lam_ref, q_ref, k_ref, v_ref, bias_ref, gcol_ref, o_ref,
                   qs_ref, s_ref, p_ref, bm_ref, m_ref, l_ref, alpha_ref, acc_ref):
    h = pl.program_id(1)
    i = pl.program_id(2)
    t = q_ref.shape[2]
    nk = k_ref.shape[2] // t
    win = min(B_WIN, nk)
    nfar = nk - win
    q = q_ref[0]
    z = jnp.zeros((B_QK, t), q.dtype)
    qs_ref[0:HEAD_DIM] = jnp.concatenate(
        [jnp.concatenate([q[0:B_QK], z], axis=0), jnp.concatenate([z, q[B_QK:HEAD_DIM]], axis=0)], axis=1)
    qs_ref[HEAD_DIM:K_LANES] = jnp.zeros((K_LANES - HEAD_DIM, 2 * t), qs_ref.dtype)
    _pipe_init(m_ref, l_ref, alpha_ref, acc_ref, p_ref)
    c_neg = far_ref[h, 0]
    c_pos = far_ref[h, 1]
    w0 = jnp.clip(i - 1, 0, nk - win)

    def kchunk(j):
        return k_ref[0, 0, pl.ds(pl.multiple_of(j * t, t), t), :]

    def vchunk(j):
        return v_ref[0, :, pl.ds(pl.multiple_of(j * t, t), t)]

    def far_chunk(f):
        return jnp.where(f < w0, f, f + win)

    def qk_win(j):
        b = bias_ref[0, j - i + B_TILES // 2]
        _pipe_qk(kchunk(j), qs_ref, s_ref, bm_ref, jnp.concatenate([b, b], axis=1))

    def sm(c):
        _pipe_sm(c, s_ref, bm_ref, m_ref, l_ref, alpha_ref, p_ref)

    def pv(j):
        _pipe_pv(vchunk(j), p_ref, alpha_ref, acc_ref)

    qk_win(w0)
    for w in range(win):
        if w > 0:
            pv(w0 + w - 1)
        sm(0.0)
        if w + 1 < win:
            qk_win(w0 + w + 1)
        elif nfar > 0:
            _pipe_qk(kchunk(far_chunk(0)), qs_ref, s_ref, bm_ref)

    def far_step(f):
        pv(jnp.where(f == 0, w0 + win - 1, far_chunk(f - 1)))
        sm(jnp.where(f < w0, c_neg, c_pos))
        _pipe_qk(kchunk(far_chunk(f + 1)), qs_ref, s_ref, bm_ref)

    if nfar > 0:
        n_full = nfar - 1

        def group(fg, carry):
            for u in range(B_UNROLL):
                far_step(B_UNROLL * fg + u)
            return carry

        lax.fori_loop(0, n_full // B_UNROLL, group, 0)
        for f in range(n_full - n_full % B_UNROLL, n_full):
            far_step(f)
        pv(far_chunk(nfar - 2) if nfar > 1 else w0 + win - 1)
        sm(jnp.where(nfar - 1 < w0, c_neg, c_pos))
        pv(far_chunk(nfar - 1))
    else:
        pv(w0 + win - 1)

    o = acc_ref[...] / l_ref[...]
    o = o[:, 0:t] - lam_ref[0] * o[:, t:2 * t]
    ms = jnp.mean(o * o, axis=0, keepdims=True)
    o_ref[0] = (o * lax.rsqrt(ms + EPS) * gcol_ref[...]).astype(o_ref.dtype)


def _attn_b(qt, k, vt, bias_tiles, far, lam, gcol):
    bsz, _, s_len = qt.shape
    t = bias_tiles.shape[-1]
    return pl.pallas_call(
        _attn_b_kernel,
        grid_spec=pltpu.PrefetchScalarGridSpec(
            num_scalar_prefetch=2,
            grid=(bsz, B_HEADS, s_len // t),
            in_specs=[
                pl.BlockSpec((1, HEAD_DIM, t), lambda b, h, i, *_: (b, h, i)),
                pl.BlockSpec((1, 1, s_len, K_LANES), lambda b, h, i, *_: (b, h, 0, 0)),
                pl.BlockSpec((1, HEAD_DIM, s_len), lambda b, h, i, *_: (b, h, 0)),
                pl.BlockSpec((1, B_TILES, t, t), lambda b, h, i, *_: (h, 0, 0, 0)),
                pl.BlockSpec((HEAD_DIM, 1), lambda b, h, i, *_: (0, 0)),
            ],
            out_specs=pl.BlockSpec((1, HEAD_DIM, t), lambda b, h, i, *_: (b, h, i)),
            scratch_shapes=_pipe_scratch(t, 2 * t),
        ),
        out_shape=jax.ShapeDtypeStruct((bsz, B_Q, s_len), BF16),
        compiler_params=_cparams(("arbitrary", "arbitrary", "arbitrary")),
        name="attn_b",
    )(far, lam, qt, k, vt, bias_tiles, gcol)


SUM_ROWS = 16
SHIFT_LIMIT = 100.0
FAST_C_UNROLL = 3
FAST_B_UNROLL = 7


def _fast_qk_exp(kc, qaug_ref, p_ref, tile=None):
    s = _dot(kc, qaug_ref[...])
    if tile is not None:
        s = s + tile
    p_ref[...] = jnp.exp2(s).astype(p_ref.dtype)


def _fast_pv(vc, p_ref, acc_ref, cls):
    va = jnp.concatenate([vc, jnp.ones((SUM_ROWS, vc.shape[1]), vc.dtype)], axis=0)
    acc_ref[cls] = acc_ref[cls] + _dot(va, p_ref[...])


def _fast_scratch(tk, n, nacc):
    return [
        pltpu.VMEM((K_LANES, n), BF16),
        pltpu.VMEM((tk, n), BF16),
        pltpu.VMEM((nacc, HEAD_DIM + SUM_ROWS, n), F32),
    ]


def _shift_rows(shift, n):
    row = lax.broadcasted_iota(jnp.int32, (K_LANES - HEAD_DIM, n), 0)
    return jnp.where(row == 0, -shift, 0.0).astype(BF16)


def _attn_c_fast_kernel(shift_ref, q_ref, k_ref, v_ref, o_ref, qaug_ref, p_ref, acc_ref, *, tk):
    g = C_HEADS // C_KV
    tq = q_ref.shape[2]
    nk = k_ref.shape[2] // tk
    q3 = q_ref[0]
    qaug_ref[0:HEAD_DIM] = jnp.concatenate([q3[h * HEAD_DIM:(h + 1) * HEAD_DIM] for h in range(g)], axis=1)
    qaug_ref[HEAD_DIM:K_LANES] = _shift_rows(shift_ref[0], g * tq)
    acc_ref[...] = jnp.zeros(acc_ref.shape, F32)

    def kchunk(j):
        return k_ref[0, 0, pl.ds(pl.multiple_of(j * tk, tk), tk), :]

    def vchunk(j):
        return v_ref[0, :, pl.ds(pl.multiple_of(j * tk, tk), tk)]

    _fast_qk_exp(kchunk(0), qaug_ref, p_ref)

    def step(j):
        _fast_pv(vchunk(j - 1), p_ref, acc_ref, 0)
        _fast_qk_exp(kchunk(j), qaug_ref, p_ref)

    def group(jg, carry):
        for u in range(FAST_C_UNROLL):
            step(1 + FAST_C_UNROLL * jg + u)
        return carry

    n_full = nk - 1
    lax.fori_loop(0, n_full // FAST_C_UNROLL, group, 0)
    for j in range(1 + n_full - n_full % FAST_C_UNROLL, nk):
        step(j)
    _fast_pv(vchunk(nk - 1), p_ref, acc_ref, 0)
    acc = acc_ref[0]
    o = acc[0:HEAD_DIM] / acc[HEAD_DIM:HEAD_DIM + 1]
    o_ref[0] = jnp.concatenate([o[:, h * tq:(h + 1) * tq] for h in range(g)], axis=0).astype(o_ref.dtype)


def _attn_c_fast(qt, k, vt, shift):
    bsz, _, s_len = qt.shape
    g = C_HEADS // C_KV
    tq = min(TQ_C, s_len)
    tk = min(TK_C, s_len)
    return pl.pallas_call(
        functools.partial(_attn_c_fast_kernel, tk=tk),
        grid_spec=pltpu.PrefetchScalarGridSpec(
            num_scalar_prefetch=1,
            grid=(bsz, C_KV, s_len // tq),
            in_specs=[
                pl.BlockSpec((1, g * HEAD_DIM, tq), lambda b, kv, i, *_: (b, kv, i)),
                pl.BlockSpec((1, 1, s_len, K_LANES), lambda b, kv, i, *_: (b, kv, 0, 0)),
                pl.BlockSpec((1, HEAD_DIM, s_len), lambda b, kv, i, *_: (b, kv, 0)),
            ],
            out_specs=pl.BlockSpec((1, g * HEAD_DIM, tq), lambda b, kv, i, *_: (b, kv, i)),
            scratch_shapes=_fast_scratch(tk, g * tq, 1),
        ),
        out_shape=jax.ShapeDtypeStruct((bsz, C_Q, s_len), BF16),
        compiler_params=_cparams(("arbitrary", "arbitrary", "arbitrary")),
        name="attn_c_fast",
    )(shift, qt, k, vt)


def _attn_b_fast_kernel(far_ref, lam_ref, shift_ref, q_ref, k_ref, v_ref, bias_ref, gcol_ref, o_ref,
                        qaug_ref, p_ref, acc_ref):
    h = pl.program_id(1)
    i = pl.program_id(2)
    t = q_ref.shape[2]
    nk = k_ref.shape[2] // t
    win = min(B_WIN, nk)
    nfar = nk - win
    q = q_ref[0]
    z = jnp.zeros((B_QK, t), q.dtype)
    qaug_ref[0:HEAD_DIM] = jnp.concatenate(
        [jnp.concatenate([q[0:B_QK], z], axis=0), jnp.concatenate([z, q[B_QK:HEAD_DIM]], axis=0)], axis=1)
    qaug_ref[HEAD_DIM:K_LANES] = _shift_rows(shift_ref[h], 2 * t)
    acc_ref[...] = jnp.zeros(acc_ref.shape, F32)
    w0 = jnp.clip(i - 1, 0, nk - win)

    def kchunk(j):
        return k_ref[0, 0, pl.ds(pl.multiple_of(j * t, t), t), :]

    def vchunk(j):
        return v_ref[0, :, pl.ds(pl.multiple_of(j * t, t), t)]

    def far_chunk(f):
        return jnp.where(f < w0, f, f + win)

    def far_cls(f):
        return jnp.where(f < w0, 1, 2)

    def qk_win(j):
        b = bias_ref[0, j - i + B_TILES // 2]
        _fast_qk_exp(kchunk(j), qaug_ref, p_ref, jnp.concatenate([b, b], axis=1))

    qk_win(w0)
    for w in range(1, win):
        _fast_pv(vchunk(w0 + w - 1), p_ref, acc_ref, 0)
        qk_win(w0 + w)
    _fast_pv(vchunk(w0 + win - 1), p_ref, acc_ref, 0)

    if nfar > 0:
        _fast_qk_exp(kchunk(far_chunk(0)), qaug_ref, p_ref)

        def far_step(f):
            _fast_pv(vchunk(far_chunk(f - 1)), p_ref, acc_ref, far_cls(f - 1))
            _fast_qk_exp(kchunk(far_chunk(f)), qaug_ref, p_ref)

        n_full = nfar - 1

        def group(fg, carry):
            for u in range(FAST_B_UNROLL):
                far_step(1 + FAST_B_UNROLL * fg + u)
            return carry

        lax.fori_loop(0, n_full // FAST_B_UNROLL, group, 0)
        for f in range(1 + n_full - n_full % FAST_B_UNROLL, nfar):
            far_step(f)
        _fast_pv(vchunk(far_chunk(nfar - 1)), p_ref, acc_ref, far_cls(nfar - 1))

    wneg = jnp.exp2(jnp.full((1, 1), far_ref[h, 0], F32))
    wpos = jnp.exp2(jnp.full((1, 1), far_ref[h, 1], F32))
    acc = acc_ref[0] + wneg * acc_ref[1] + wpos * acc_ref[2]
    o = acc[0:HEAD_DIM] / acc[HEAD_DIM:HEAD_DIM + 1]
    o = o[:, 0:t] - lam_ref[0] * o[:, t:2 * t]
    ms = jnp.mean(o * o, axis=0, keepdims=True)
    o_ref[0] = (o * lax.rsqrt(ms + EPS) * gcol_ref[...]).astype(o_ref.dtype)


def _attn_b_fast(qt, k, vt, bias_tiles, far, lam, gcol, shift):
    bsz, _, s_len = qt.shape
    t = bias_tiles.shape[-1]
    return pl.pallas_call(
        _attn_b_fast_kernel,
        grid_spec=pltpu.PrefetchScalarGridSpec(
            num_scalar_prefetch=3,
            grid=(bsz, B_HEADS, s_len // t),
            in_specs=[
                pl.BlockSpec((1, HEAD_DIM, t), lambda b, h, i, *_: (b, h, i)),
                pl.BlockSpec((1, 1, s_len, K_LANES), lambda b, h, i, *_: (b, h, 0, 0)),
                pl.BlockSpec((1, HEAD_DIM, s_len), lambda b, h, i, *_: (b, h, 0)),
                pl.BlockSpec((1, B_TILES, t, t), lambda b, h, i, *_: (h, 0, 0, 0)),
                pl.BlockSpec((HEAD_DIM, 1), lambda b, h, i, *_: (0, 0)),
            ],
            out_specs=pl.BlockSpec((1, HEAD_DIM, t), lambda b, h, i, *_: (b, h, i)),
            scratch_shapes=_fast_scratch(t, 2 * t, 3),
        ),
        out_shape=jax.ShapeDtypeStruct((bsz, B_Q, s_len), BF16),
        compiler_params=_cparams(("arbitrary", "arbitrary", "arbitrary")),
        name="attn_b_fast",
    )(far, lam, shift, qt, k, vt, bias_tiles, gcol)


def _attn_a_kernel(sink_ref, q_ref, kp_ref, kc_ref, kn_ref, vp_ref, vc_ref, vn_ref, bias_ref, o_ref):
    n = pl.program_id(1)
    nb = pl.num_programs(1)
    g = A_HEADS // A_KV
    blk = A_BLK
    row = lax.broadcasted_iota(jnp.int32, (3 * blk, 1), 0)
    valid = ((row >= blk) | (n > 0)) & ((row < 2 * blk) | (n < nb - 1))
    gw = g * blk
    zero_q = jnp.zeros((K_LANES, gw), q_ref.dtype)
    qrows = []
    for kv in range(A_KV):
        q3 = q_ref[0, kv * g * HEAD_DIM:(kv + 1) * g * HEAD_DIM]
        qcat = jnp.concatenate([q3[h * HEAD_DIM:(h + 1) * HEAD_DIM] for h in range(g)], axis=1)
        qaug = jnp.concatenate([qcat, jnp.zeros((K_LANES - HEAD_DIM, gw), qcat.dtype)], axis=0)
        qrows.append(jnp.concatenate([qaug if c == kv else zero_q for c in range(A_KV)], axis=1))
    qbd = jnp.concatenate(qrows, axis=0)
    kwin = jnp.concatenate(
        [jnp.concatenate([r[0, kv] for r in (kp_ref, kc_ref, kn_ref)], axis=0) for kv in range(A_KV)], axis=1)
    vwin = jnp.concatenate([vp_ref[0], vc_ref[0], vn_ref[0]], axis=1)
    bias = jnp.concatenate([bias_ref[h] for h in range(A_HEADS)], axis=1)
    s = jnp.where(valid, _dot(kwin, qbd) + bias, NEG_BIG)
    sink = jnp.concatenate([jnp.full((1, blk), sink_ref[h], F32) for h in range(A_HEADS)], axis=1)
    m = jnp.maximum(jnp.max(s, axis=0, keepdims=True), sink)
    p = jnp.exp2(s - m)
    denom = jnp.sum(p, axis=0, keepdims=True) + jnp.exp2(sink - m)
    o_all = _dot(vwin, p.astype(BF16)) / denom
    for kv in range(A_KV):
        o = o_all[kv * HEAD_DIM:(kv + 1) * HEAD_DIM, kv * gw:(kv + 1) * gw]
        o_ref[0, kv * g * HEAD_DIM:(kv + 1) * g * HEAD_DIM] = jnp.concatenate(
            [o[:, h * blk:(h + 1) * blk] for h in range(g)], axis=0).astype(o_ref.dtype)


def _attn_a(qt, k, vt, bias, sink):
    bsz, _, s_len = qt.shape
    nb = s_len // A_BLK

    def kspec(off):
        return pl.BlockSpec((1, A_KV, A_BLK, K_LANES),
                            lambda b, n, *_: (b, 0, jnp.clip(n + off, 0, nb - 1), 0))

    def vspec(off):
        return pl.BlockSpec((1, A_V, A_BLK), lambda b, n, *_: (b, 0, jnp.clip(n + off, 0, nb - 1)))

    return pl.pallas_call(
        _attn_a_kernel,
        grid_spec=pltpu.PrefetchScalarGridSpec(
            num_scalar_prefetch=1,
            grid=(bsz, nb),
            in_specs=[
                pl.BlockSpec((1, A_Q, A_BLK), lambda b, n, *_: (b, 0, n)),
                kspec(-1), kspec(0), kspec(1), vspec(-1), vspec(0), vspec(1),
                pl.BlockSpec((A_HEADS, 3 * A_BLK, A_BLK), lambda b, n, *_: (0, 0, 0)),
            ],
            out_specs=pl.BlockSpec((1, A_Q, A_BLK), lambda b, n, *_: (b, 0, n)),
        ),
        out_shape=jax.ShapeDtypeStruct((bsz, A_Q, s_len), BF16),
        compiler_params=_cparams(("arbitrary", "arbitrary")),
        name="attn_a",
    )(sink, qt, k, k, k, vt, vt, vt, bias)


def _outproj_kernel(oa_ref, ob_ref, oc_ref, wa_ref, wb_ref, wc_ref, x_ref, mod_ref, g_ref,
                    wrh_ref, wrl_ref, x1_ref, h2_ref, aff_ref):
    def tn(a, b):
        return lax.dot_general(a, b, (((0,), (0,)), ((), ())), preferred_element_type=F32)

    out = tn(oa_ref[0], wa_ref[...]) + tn(ob_ref[0], wb_ref[...]) + tn(oc_ref[0], wc_ref[...])
    mod = mod_ref[0]
    d = out.shape[1]
    gt_a = mod[:, 2 * d:3 * d]
    sh_f, sc_f = mod[:, 3 * d:4 * d], mod[:, 4 * d:5 * d]
    x1 = x_ref[...] + gt_a * out
    x1_ref[...] = x1
    ms = jnp.mean(x1 * x1, axis=-1, keepdims=True)
    h2 = x1 * lax.rsqrt(ms + EPS) * g_ref[...]
    h2 = h2 * (1.0 + sc_f) + sh_f
    for c in range(d // LANES):
        h2_ref[:, c, :] = h2[:, c * LANES:(c + 1) * LANES]
    hi = h2.astype(BF16)
    lo = (h2 - hi.astype(F32)).astype(BF16)
    lt = _dot_nt(wrh_ref[...], hi) + _dot_nt(wrh_ref[...], lo) + _dot_nt(wrl_ref[...], hi)
    lt = lt - jnp.max(lt, axis=0, keepdims=True)
    e = jnp.exp(lt)
    aff_ref[...] = e / jnp.sum(e, axis=0, keepdims=True)


def _outproj(oa, ob, oc, wa, wb, wc, x2d, mod3, g_ffn, wr_hi, wr_lo, s_len):
    n, d = x2d.shape
    tm = min(TM_PROJ, s_len)
    spb = s_len // tm

    def ot_spec(rows_):
        return pl.BlockSpec((1, rows_, tm), lambda i: (i // spb, 0, i % spb))

    def full(shape):
        return pl.BlockSpec(shape, lambda i: tuple(0 for _ in shape))

    return pl.pallas_call(
        _outproj_kernel,
        grid=(n // tm,),
        in_specs=[
            ot_spec(A_Q), ot_spec(B_Q), ot_spec(C_Q),
            full((A_Q, d)), full((B_Q, d)), full((C_Q, d)),
            pl.BlockSpec((tm, d), lambda i: (i, 0)),
            pl.BlockSpec((1, 1, N_MOD * d), lambda i: (i // spb, 0, 0)),
            full((1, d)), full((N_EXPERTS, d)), full((N_EXPERTS, d)),
        ],
        out_specs=[
            pl.BlockSpec((tm, d), lambda i: (i, 0)),
            pl.BlockSpec((tm, d // LANES, LANES), lambda i: (i, 0, 0)),
            pl.BlockSpec((N_EXPERTS, tm), lambda i: (0, i)),
        ],
        out_shape=[
            jax.ShapeDtypeStruct((n, d), F32),
            jax.ShapeDtypeStruct((n, d // LANES, LANES), F32),
            jax.ShapeDtypeStruct((N_EXPERTS, n), F32),
        ],
        compiler_params=_cparams(("arbitrary",)),
        name="outproj",
    )(oa, ob, oc, wa, wb, wc, x2d, mod3, g_ffn, wr_hi, wr_lo)


def _routing_kernel(aff_ref, idx_ref, gate_ref, pexc_ref, thr_ref, need_ref, *, cap, pc):
    ne, nblk, _ = aff_ref.shape
    bits = lax.bitcast_convert_type(aff_ref[...], jnp.int32)

    def bis(it, t):
        cand = t | jnp.left_shift(jnp.int32(1), 30 - it)
        cnt = jnp.sum((bits >= cand).astype(F32), axis=(1, 2), keepdims=True)
        return jnp.where(cnt >= cap, cand, t)

    thr = lax.fori_loop(0, 31, bis, jnp.zeros((ne, 1, 1), jnp.int32))
    n_gt = jnp.sum((bits > thr).astype(F32), axis=(1, 2), keepdims=True)
    thr_ref[...] = jnp.broadcast_to(thr, thr_ref.shape)
    need_ref[...] = jnp.broadcast_to(cap - n_gt, need_ref.shape)

    def tri(nn, fn):
        a = lax.broadcasted_iota(jnp.int32, (nn, nn), 0)
        b = lax.broadcasted_iota(jnp.int32, (nn, nn), 1)
        return jnp.where(fn(a, b), 1.0, 0.0).astype(BF16)

    ones_l = jnp.ones((LANES, LANES), BF16)
    su_l = tri(LANES, lambda a, b: a < b)
    li_l = tri(LANES, lambda a, b: b <= a)
    sl_b = tri(nblk, lambda a, b: b < a)
    li_b = tri(nblk, lambda a, b: b <= a)
    blk_col = lax.broadcasted_iota(jnp.int32, (nblk, 1), 0).astype(F32)
    lane_col = lax.broadcasted_iota(jnp.int32, (LANES, 1), 0).astype(F32)

    def per_expert(e, carry):
        a = aff_ref[e]
        be = lax.bitcast_convert_type(a, jnp.int32)
        te = thr_ref[e]
        gt = be > te
        eq = (be == te)
        eqb = jnp.where(eq, 1.0, 0.0).astype(BF16)
        rank = _dot(eqb, su_l) + _dot(sl_b, _dot(eqb, ones_l).astype(BF16))
        sel = gt | (eq & (rank < need_ref[e]))
        mb = jnp.where(sel, 1.0, 0.0).astype(BF16)
        cnt_b = _dot(mb, ones_l)
        pinc = _dot(li_b, cnt_b.astype(BF16))
        pinc_col = pinc[:, 0:1]
        pexc_col = pinc_col - cnt_b[:, 0:1]
        pexc_ref[e] = (pinc - cnt_b).T[0:1].astype(jnp.int32)
        mt = jnp.where(sel, 1.0, 0.0).T.astype(BF16)
        at = a.T
        at_hi = at.astype(BF16)
        at_lo = (at - at_hi.astype(F32)).astype(BF16)
        for c in range(cap // pc):
            p_row = (lax.broadcasted_iota(jnp.int32, (1, pc), 1) + c * pc).astype(F32)
            blk_row = jnp.sum(jnp.where(pinc_col <= p_row, 1.0, 0.0), axis=0, keepdims=True)
            oh = blk_col == blk_row
            ohb = jnp.where(oh, 1.0, 0.0).astype(BF16)
            r_row = p_row - jnp.sum(jnp.where(oh, pexc_col, 0.0), axis=0, keepdims=True)
            gt_rows = _dot(mt, ohb)
            pc_incl = _dot(li_l, gt_rows.astype(BF16))
            lane_row = jnp.sum(jnp.where(pc_incl <= r_row, 1.0, 0.0), axis=0, keepdims=True)
            idx_ref[e, :, c * pc:(c + 1) * pc] = (blk_row * LANES + lane_row).astype(jnp.int32)
            ag = _dot(at_hi, ohb) + _dot(at_lo, ohb)
            gate_ref[e, :, c * pc:(c + 1) * pc] = jnp.sum(
                jnp.where(lane_col == lane_row, ag, 0.0), axis=0, keepdims=True)
        return carry

    lax.fori_loop(0, ne, per_expert, 0)


def _routing(aff3, cap):
    ne, nblk, _ = aff3.shape
    pc = min(ROUTE_PC, cap)
    return pl.pallas_call(
        functools.partial(_routing_kernel, cap=cap, pc=pc),
        grid=(1,),
        in_specs=[pl.BlockSpec((ne, nblk, LANES), lambda i: (0, 0, 0))],
        out_specs=[
            pl.BlockSpec((ne, 1, cap), lambda i: (0, 0, 0)),
            pl.BlockSpec((ne, 1, cap), lambda i: (0, 0, 0)),
            pl.BlockSpec((ne, 1, nblk), lambda i: (0, 0, 0)),
        ],
        out_shape=[
            jax.ShapeDtypeStruct((ne, 1, cap), jnp.int32),
            jax.ShapeDtypeStruct((ne, 1, cap), F32),
            jax.ShapeDtypeStruct((ne, 1, nblk), jnp.int32),
        ],
        scratch_shapes=[pltpu.VMEM((ne, 1, LANES), jnp.int32), pltpu.VMEM((ne, 1, LANES), F32)],
        compiler_params=_cparams(("arbitrary",)),
        name="routing",
    )(aff3)


TOK_LANES = LANES
TOK_RADIX = 256
IDX_SLOTS = 3


def _moe_kernel(idx_ref, idxn_ref, idxnn_ref, gate_ref, h_hbm, wg_ref, wu_ref, wd_ref, y_ref,
                idx_smem, xbuf, sems):
    ch = xbuf.shape[1]
    d = xbuf.shape[2] * xbuf.shape[3]
    nsteps = pl.num_programs(0) * pl.num_programs(1)
    s = pl.program_id(0) * pl.num_programs(1) + pl.program_id(1)
    slot = s % 2
    nslot = 1 - slot
    k1 = (s + 1) % IDX_SLOTS
    k2 = (s + 2) % IDX_SLOTS

    def idx_copy(src_ref, sl):
        return pltpu.make_async_copy(src_ref.at[0, 0], idx_smem.at[sl], sems.at[2 + sl])

    def row_copy(sl, r, t):
        return pltpu.make_async_copy(h_hbm.at[pl.ds(t, 1)], xbuf.at[sl, pl.ds(r, 1)], sems.at[sl])

    @pl.when(s == 0)
    def _():
        c0 = idx_copy(idx_ref, 0)
        c0.start()
        c0.wait()
        for r in range(ch):
            row_copy(0, r, idx_smem[0, r]).start()
        c1 = idx_copy(idxn_ref, 1)
        c1.start()
        c1.wait()

    idx_copy(idxnn_ref, k2).start()
    for r in range(ch):
        row_copy(slot, r, 0).wait()
    for r in range(ch):
        row_copy(nslot, r, idx_smem[k1, r]).start()

    x = jnp.concatenate([xbuf[slot, :, c, :] for c in range(xbuf.shape[2])], axis=1).astype(BF16)
    gg = _dot(x, wg_ref[0])
    uu = _dot(x, wu_ref[0])
    hid = (gg * (1.0 / (1.0 + jnp.exp(-gg))) * uu).astype(BF16)
    y = _dot(hid, wd_ref[0])
    sub = lax.broadcasted_iota(jnp.int32, (LANES, ch), 0)
    cols = jnp.where(sub == 0, gate_ref[0], jnp.where(sub == 1, idx_ref[0].astype(F32), 0.0)).T
    y_ref[:, 0:d] = (y * cols[:, 0:1]).astype(y_ref.dtype)
    tok = cols[:, 1:2]
    hi = jnp.floor(tok * (1.0 / TOK_RADIX))
    lane = lax.broadcasted_iota(jnp.int32, (ch, TOK_LANES), 1)
    y_ref[:, d:d + TOK_LANES] = jnp.where(
        lane == 0, hi, jnp.where(lane == 1, tok - hi * TOK_RADIX, 0.0)).astype(y_ref.dtype)

    idx_copy(idxnn_ref, k2).wait()

    @pl.when(s == nsteps - 1)
    def _():
        for r in range(ch):
            row_copy(nslot, r, 0).wait()


def _moe(idx3, gate3, h2, wg, wu, wd, cap):
    n, dt, _ = h2.shape
    d = dt * LANES
    ne, _, ff = wg.shape
    ch = min(MOE_CH, cap)
    nch = cap // ch
    nsteps = ne * nch
    return pl.pallas_call(
        _moe_kernel,
        grid=(ne, nch),
        in_specs=[
            pl.BlockSpec((1, 1, ch), lambda e, c: (e * nch + c, 0, 0)),
            pl.BlockSpec((1, 1, ch), lambda e, c: (jnp.minimum(e * nch + c + 1, nsteps - 1), 0, 0)),
            pl.BlockSpec((1, 1, ch), lambda e, c: (jnp.minimum(e * nch + c + 2, nsteps - 1), 0, 0)),
            pl.BlockSpec((1, 1, ch), lambda e, c: (e * nch + c, 0, 0)),
            pl.BlockSpec(memory_space=pl.ANY),
            pl.BlockSpec((1, d, ff), lambda e, c: (e, 0, 0)),
            pl.BlockSpec((1, d, ff), lambda e, c: (e, 0, 0)),
            pl.BlockSpec((1, ff, d), lambda e, c: (e, 0, 0)),
        ],
        out_specs=pl.BlockSpec((ch, d + TOK_LANES), lambda e, c: (e * nch + c, 0)),
        out_shape=jax.ShapeDtypeStruct((ne * cap, d + TOK_LANES), BF16),
        scratch_shapes=[
            pltpu.SMEM((IDX_SLOTS, ch), jnp.int32),
            pltpu.VMEM((2, ch, dt, LANES), F32),
            pltpu.SemaphoreType.DMA((2 + IDX_SLOTS,)),
        ],
        compiler_params=_cparams(("arbitrary", "arbitrary")),
        name="moe_ffn",
    )(idx3.reshape(nsteps, 1, ch), idx3.reshape(nsteps, 1, ch), idx3.reshape(nsteps, 1, ch),
      gate3.reshape(nsteps, 1, ch), h2, wg, wu, wd)


def _combine_kernel(starts_ref, y_hbm, x_ref, mod_ref, o_ref, ybuf, xtra, sems, *, cap, ne):
    tt, d = x_ref.shape
    wr = ybuf.shape[2]
    nt = pl.num_programs(0)
    t = pl.program_id(0)
    slot = t % 2
    nslot = 1 - slot

    def bounds(e, tile):
        lo = starts_ref[e * (nt + 1) + tile]
        hi = starts_ref[e * (nt + 1) + tile + 1]
        base = jnp.minimum((lo // 16) * 16, cap - wr)
        return lo, hi, base

    def win_copy(e, base, sl):
        return pltpu.make_async_copy(y_hbm.at[pl.ds(pl.multiple_of(e * cap + base, 16), wr)],
                                     ybuf.at[sl, e], sems.at[sl])

    def fetch(tile, sl):
        for e in range(ne):
            win_copy(e, bounds(e, tile)[2], sl).start()

    @pl.when(t == 0)
    def _():
        fetch(0, 0)

    for e in range(ne):
        win_copy(e, 0, slot).wait()
    fetch(jnp.minimum(t + 1, nt - 1), nslot)

    tok0 = (t * tt).astype(F32)
    lane_tok = lax.broadcasted_iota(jnp.int32, (1, tt), 1).astype(F32) + tok0
    row = lax.broadcasted_iota(jnp.int32, (wr, 1), 0)

    def one_hot(yb, first, lo_eff, hi):
        tokc = yb[:, d:d + 1].astype(F32) * TOK_RADIX + yb[:, d + 1:d + 2].astype(F32)
        pos = row + first
        return jnp.where((pos >= lo_eff) & (pos < hi) & (tokc == lane_tok), 1.0, 0.0).astype(BF16)

    def scatter_rows(oh, rows):
        return lax.dot_general(oh, rows, (((0,), (0,)), ((), ())), preferred_element_type=F32)

    ohs = []
    for e in range(ne):
        lo, hi, base = bounds(e, t)
        ohs.append(one_hot(ybuf[slot, e], base, lo, hi))
    rows_all = ybuf[slot].reshape(ne * wr, d + TOK_LANES)[:, 0:d]
    acc = scatter_rows(jnp.concatenate(ohs, axis=0), rows_all)
    o_ref[...] = x_ref[...] + mod_ref[0][:, 5 * d:6 * d] * acc

    for e in range(ne):
        lo, hi, base = bounds(e, t)
        n_more = jnp.maximum(hi - (base + wr) + wr - 1, 0) // wr

        def more(k, carry, e=e, hi=hi, base=base):
            done = base + wr * (k + 1)
            first = jnp.minimum(done, cap - wr)
            cp = pltpu.make_async_copy(y_hbm.at[pl.ds(pl.multiple_of(e * cap + first, 16), wr)], xtra, sems.at[2])
            cp.start()
            cp.wait()
            yb = xtra[...]
            o_ref[...] = o_ref[...] + mod_ref[0][:, 5 * d:6 * d] * scatter_rows(
                one_hot(yb, first, done, hi), yb[:, 0:d])
            return carry

        lax.fori_loop(0, n_more, more, 0)

    @pl.when(t == nt - 1)
    def _():
        for e in range(ne):
            win_copy(e, 0, nslot).wait()


def _combine(starts, y, x1, mod3, s_len, cap):
    n, d = x1.shape
    ne = y.shape[0] // cap
    tt = min(COMBINE_TT, s_len)
    spb = s_len // tt
    wr = min(COMBINE_WR, cap)
    return pl.pallas_call(
        functools.partial(_combine_kernel, cap=cap, ne=ne),
        grid_spec=pltpu.PrefetchScalarGridSpec(
            num_scalar_prefetch=1,
            grid=(n // tt,),
            in_specs=[
                pl.BlockSpec(memory_space=pl.ANY),
                pl.BlockSpec((tt, d), lambda i, *_: (i, 0)),
                pl.BlockSpec((1, 1, N_MOD * d), lambda i, *_: (i // spb, 0, 0)),
            ],
            out_specs=pl.BlockSpec((tt, d), lambda i, *_: (i, 0)),
            scratch_shapes=[
                pltpu.VMEM((2, ne, wr, d + TOK_LANES), BF16),
                pltpu.VMEM((wr, d + TOK_LANES), BF16),
                pltpu.SemaphoreType.DMA((3,)),
            ],
        ),
        out_shape=jax.ShapeDtypeStruct((n, d), F32),
        compiler_params=_cparams(("arbitrary",)),
        name="combine",
    )(starts, y, x1, mod3)


def _t5_bucket(rel):
    half = NUM_BUCKETS // 2
    max_exact = half // 2
    n = jnp.abs(rel)
    nf = jnp.maximum(n, 1).astype(F32)
    large = max_exact + (jnp.log(nf / max_exact) / math.log(MAX_DISTANCE / max_exact)
                         * (half - max_exact)).astype(jnp.int32)
    large = jnp.minimum(large, half - 1)
    return jnp.where(rel > 0, half, 0) + jnp.where(n < max_exact, n, large)


def _rope_table(s_len):
    quarter = HEAD_DIM // 4
    freqs = ROPE_THETA ** (-jnp.arange(quarter, dtype=F32) / quarter)
    t = jnp.arange(s_len)
    ang_r = freqs[:, None] * (t // GRID_W).astype(F32)[None, :]
    ang_c = freqs[:, None] * (t % GRID_W).astype(F32)[None, :]
    return jnp.concatenate([jnp.cos(ang_r), jnp.sin(ang_r), jnp.cos(ang_c), jnp.sin(ang_c)], axis=0)


def _toeplitz(fn, nj, ni):
    period = nj + ni
    m = jnp.arange(period)
    b = fn(jnp.where(m < nj, m, m - period))
    flat = jnp.tile(b, ni)[..., :ni * (period - 1)]
    rows = flat.reshape(b.shape[:-1] + (ni, period - 1))
    return jnp.swapaxes(rows[..., :nj], -1, -2)


def _bias_tables(rel_bias, t_b):
    def bias_of(rel, heads):
        return rel_bias[_t5_bucket(rel)][:, heads].T * LOG2E

    def fn_a(r):
        rel = r - A_BLK
        return jnp.where(jnp.abs(rel) <= WINDOW, bias_of(rel, slice(0, A_HEADS)), NEG_BIG)

    bias_a = _toeplitz(fn_a, 3 * A_BLK, A_BLK).astype(F32)
    tiles = [
        _toeplitz(lambda r, dd=dd: bias_of(dd * t_b + r, slice(A_HEADS, None)), t_b, t_b)
        for dd in range(-(B_TILES // 2), B_TILES // 2 + 1)
    ]
    bias_b = jnp.stack(tiles, axis=1).astype(F32)
    half = NUM_BUCKETS // 2
    far = jnp.stack([rel_bias[half - 1, A_HEADS:], rel_bias[NUM_BUCKETS - 1, A_HEADS:]], axis=1) * LOG2E
    return bias_a, bias_b, far.astype(F32)


def _gain_column(ga, gb, gc):
    sa = HEAD_DIM ** -0.5 * LOG2E
    sb = B_QK ** -0.5 * LOG2E
    parts = [
        jnp.tile(ga[0] * sa, A_HEADS), jnp.tile(ga[1], A_KV), jnp.ones((A_V,), F32),
        jnp.tile(gb[0] * sb, 2 * B_HEADS), jnp.tile(gb[1], 2 * B_HEADS), jnp.ones((B_V,), F32),
        jnp.tile(gc[0] * sa, C_HEADS), jnp.tile(gc[1], C_KV), jnp.ones((C_V,), F32),
    ]
    return jnp.concatenate(parts).reshape(IN_COLS, 1).astype(F32)


def _trunk(x, mod_l, prep, s_len, bsz):
    n = bsz * s_len
    d = x.shape[-1]
    x2d = x.reshape(n, d)
    t_b = min(T_B, s_len)
    cap = CAPACITY_FACTOR * n // N_EXPERTS
    cs = _rope_table(s_len)
    for l, lp in enumerate(prep["layers"]):
        mod3 = mod_l[l][:, None, :]
        qa, ka, va, qb, kb, vb, qc, kc, vc = _inproj(
            x2d, mod3, lp["g_attn"], lp["w_in_t"], lp["gcol"], cs, bsz, s_len)
        oa = _attn_a(qa, ka, va, prep["bias_a"], lp["sink"])
        bias_b = prep["bias_b"][t_b]
        ob = lax.cond(
            lp["b_bounded"],
            lambda q, k, v: _attn_b_fast(q, k, v, bias_b, prep["far_b"], lp["lam"], lp["subln_col"], lp["b_shift"]),
            lambda q, k, v: _attn_b(q, k, v, bias_b, prep["far_b"], lp["lam"], lp["subln_col"]),
            qb, kb, vb)
        oc = lax.cond(
            lp["c_bounded"],
            lambda q, k, v: _attn_c_fast(q, k, v, lp["c_shift"]),
            _attn_c,
            qc, kc, vc)
        x1, h2, aff = _outproj(oa, ob, oc, lp["wo_a"], lp["wo_b"], lp["wo_c"], x2d, mod3,
                               lp["g_ffn"], lp["wr_hi"], lp["wr_lo"], s_len)
        idx3, gate3, pexc = _routing(aff.reshape(N_EXPERTS, n // LANES, LANES), cap)
        y = _moe(idx3, gate3, h2, lp["wg"], lp["wu"], lp["wd"], cap)
        tt = min(COMBINE_TT, s_len)
        starts = jnp.concatenate(
            [pexc[:, 0, ::tt // LANES], jnp.full((N_EXPERTS, 1), cap, jnp.int32)], axis=1).reshape(-1)
        x2d = _combine(starts, y, x1, mod3, s_len, cap)
    return x2d.reshape(bsz, s_len, d)


def kernel(x_prompt, x_sample, c_prompt, c_sample, w_mod, b_mod, g_attn, g_ffn, w_in, w_out,
           qk_gain_a, qk_gain_b, qk_gain_c, sink_a, lam_b, subln_b, rel_bias,
           w_router, w_gate, w_up, w_down):
    depth = w_mod.shape[0]
    bp, sp, d = x_prompt.shape
    bs, ss, _ = x_sample.shape
    rows = -(-(bp + bs) // 16) * 16
    c_all = jnp.zeros((rows, d), F32).at[:bp].set(c_prompt).at[bp:bp + bs].set(c_sample)
    mod_all = _modulation(c_all, w_mod, b_mod)
    mod_p, mod_s = mod_all[:, :bp], mod_all[:, bp:bp + bs]

    tbs = sorted({min(T_B, sp), min(T_B, ss)})
    tabs = {t: _bias_tables(rel_bias, t) for t in tbs}
    prep = {"bias_a": tabs[tbs[0]][0], "far_b": tabs[tbs[0]][2],
            "bias_b": {t: tabs[t][1] for t in tbs}, "layers": []}
    for l in range(depth):
        lam_init = 0.8 - 0.6 * math.exp(-0.3 * l)
        lp = lam_b[l].astype(F32)
        lam = jnp.exp(jnp.sum(lp[0] * lp[1])) - jnp.exp(jnp.sum(lp[2] * lp[3])) + lam_init
        wo = w_out[l].astype(BF16)
        wr_t = w_router[l].T
        wr_hi = wr_t.astype(BF16)
        gmax = lambda g_: jnp.max(jnp.abs(g_))
        c_c = 1.02 * HEAD_DIM * gmax(qk_gain_c[l][0]) * gmax(qk_gain_c[l][1]) * (HEAD_DIM ** -0.5 * LOG2E)
        c_b = 1.02 * B_QK * gmax(qk_gain_b[l][0]) * gmax(qk_gain_b[l][1]) * (B_QK ** -0.5 * LOG2E)
        bias_vals = rel_bias[:, A_HEADS:] * LOG2E
        prep["layers"].append({
            "c_shift": c_c.reshape(1).astype(F32),
            "c_bounded": 2.0 * c_c <= SHIFT_LIMIT,
            "b_shift": (c_b + jnp.max(bias_vals, axis=0)).astype(F32),
            "b_bounded": 2.0 * c_b + 2.0 * jnp.max(jnp.abs(bias_vals)) <= SHIFT_LIMIT,
            "g_attn": g_attn[l].reshape(1, d), "g_ffn": g_ffn[l].reshape(1, d),
            "w_in_t": w_in[l].T.astype(BF16),
            "gcol": _gain_column(qk_gain_a[l], qk_gain_b[l], qk_gain_c[l]),
            "sink": (sink_a[l] * LOG2E).astype(F32),
            "lam": lam.reshape(1).astype(F32),
            "subln_col": (subln_b[l] * (1.0 - lam_init)).reshape(HEAD_DIM, 1).astype(F32),
            "wo_a": wo[0:A_Q], "wo_b": wo[A_Q:A_Q + B_Q], "wo_c": wo[A_Q + B_Q:],
            "wr_hi": wr_hi, "wr_lo": (wr_t - wr_hi.astype(F32)).astype(BF16),
            "wg": w_gate[l].astype(BF16), "wu": w_up[l].astype(BF16), "wd": w_down[l].astype(BF16),
        })
    y_p = _trunk(x_prompt, mod_p, prep, sp, bp)
    y_s = _trunk(x_sample, mod_s, prep, ss, bs)
    return (y_p, y_s)
```

```python
import functools
import math

import jax
import jax.numpy as jnp
from jax import lax
from jax.experimental import pallas as pl
from jax.experimental.pallas import tpu as pltpu

F32 = jnp.float32
BF16 = jnp.bfloat16

HEAD_DIM = 64
A_HEADS, A_KV = 6, 2
B_HEADS, B_QK = 4, 32
C_HEADS, C_KV = 6, 2
WINDOW = 128
A_BLK = 128
NUM_BUCKETS, MAX_DISTANCE = 32, 128
ROPE_THETA = 10000.0
GRID_W = 64
N_EXPERTS = 16
CAPACITY_FACTOR = 2
N_MOD = 6
EPS = 1e-6
NEG_BIG = -1e30
LOG2E = math.log2(math.e)

A_Q, A_K, A_V = A_HEADS * HEAD_DIM, A_KV * HEAD_DIM, A_KV * HEAD_DIM
B_Q, B_K, B_V = B_HEADS * HEAD_DIM, B_HEADS * HEAD_DIM, B_HEADS * HEAD_DIM
C_Q, C_K, C_V = C_HEADS * HEAD_DIM, C_KV * HEAD_DIM, C_KV * HEAD_DIM
A_OFF = 0
B_OFF = A_Q + A_K + A_V
C_OFF = B_OFF + B_Q + B_K + B_V
IN_COLS = C_OFF + C_Q + C_K + C_V

LANES = 128
K_LANES = LANES
VMEM_LIMIT_BYTES = 56 * 1024 * 1024

TM_PROJ = 512
TQ_C, TK_C = 512, 1024
T_B = 512
MOE_CH = 256
ROUTE_PC = 512
COMBINE_TT = 512
COMBINE_WR = 128


def _cparams(sem):
    return pltpu.CompilerParams(dimension_semantics=sem, vmem_limit_bytes=VMEM_LIMIT_BYTES)


def _dot(a, b):
    return jnp.dot(a, b, preferred_element_type=F32)


def _dot_nt(a, b):
    return lax.dot_general(a, b, (((1,), (1,)), ((), ())), preferred_element_type=F32)


def _mod_kernel(c_ref, w_ref, b_ref, o_ref):
    c = c_ref[...]
    a = c * (1.0 / (1.0 + jnp.exp(-c)))
    a_hi = a.astype(BF16)
    a_lo = (a - a_hi.astype(F32)).astype(BF16)
    w = w_ref[0]
    w_hi = w.astype(BF16)
    w_lo = (w - w_hi.astype(F32)).astype(BF16)
    acc = _dot(a_hi, w_hi) + _dot(a_lo, w_hi) + _dot(a_hi, w_lo)
    o_ref[0] = acc + b_ref[0]


def _modulation(c_all, w_mod, b_mod):
    depth, d, nm = w_mod.shape
    rows = c_all.shape[0]
    bn = 1024
    return pl.pallas_call(
        _mod_kernel,
        grid=(depth, nm // bn),
        in_specs=[
            pl.BlockSpec((rows, d), lambda l, j: (0, 0)),
            pl.BlockSpec((1, d, bn), lambda l, j: (l, 0, j)),
            pl.BlockSpec((1, 1, bn), lambda l, j: (l, 0, j)),
        ],
        out_specs=pl.BlockSpec((1, rows, bn), lambda l, j: (l, 0, j)),
        out_shape=jax.ShapeDtypeStruct((depth, rows, nm), F32),
        compiler_params=_cparams(("arbitrary", "arbitrary")),
        name="modulation",
    )(c_all, w_mod, b_mod.reshape(depth, 1, nm))


def _segnorm(rows, seglen):
    r, t = rows.shape
    r3 = rows.reshape(r // seglen, seglen, t)
    ms = jnp.mean(r3 * r3, axis=1, keepdims=True)
    return (r3 * lax.rsqrt(ms + EPS)).reshape(r, t)


def _rope(rows, cs):
    r, t = rows.shape
    x = rows.reshape(r // HEAD_DIM, HEAD_DIM, t)
    cr, sr, cc, sc = cs[None, 0:16], cs[None, 16:32], cs[None, 32:48], cs[None, 48:64]
    ar, br, ac, bc = x[:, 0:16], x[:, 16:32], x[:, 32:48], x[:, 48:64]
    out = jnp.concatenate(
        [ar * cr - br * sr, br * cr + ar * sr, ac * cc - bc * sc, bc * cc + ac * sc], axis=1)
    return out.reshape(r, t)


def _emit_k(k_ref, kt):
    lane = lax.broadcasted_iota(jnp.int32, (1, K_LANES), 1)
    pad = jnp.where(lane == HEAD_DIM, 1.0, 0.0)
    for p in range(kt.shape[0] // LANES):
        tok = kt[p * LANES:(p + 1) * LANES].T
        k_ref[0, 2 * p] = jnp.where(lane < HEAD_DIM, tok, pad).astype(k_ref.dtype)
        k_ref[0, 2 * p + 1] = jnp.where(lane < HEAD_DIM, pltpu.roll(tok, HEAD_DIM, 1), pad).astype(k_ref.dtype)


def _emit_vt(v_ref, vt):
    v_ref[0] = vt.astype(v_ref.dtype)


def _inproj_kernel(x_ref, mod_ref, g_ref, w_ref, gcol_ref, cs_ref,
                   qa_ref, ka_ref, va_ref, qb_ref, kb_ref, vb_ref, qc_ref, kc_ref, vc_ref):
    x = x_ref[...]
    mod = mod_ref[0]
    d = x.shape[1]
    sh, sc = mod[:, 0:d], mod[:, d:2 * d]
    ms = jnp.mean(x * x, axis=-1, keepdims=True)
    h = x * lax.rsqrt(ms + EPS) * g_ref[...]
    h = (h * (1.0 + sc) + sh).astype(BF16)
    pt = _dot_nt(w_ref[...], h)
    gcol = gcol_ref[...]
    cs = cs_ref[...]

    def rows(off, n):
        return pt[off:off + n], gcol[off:off + n]

    q, g = rows(A_OFF, A_Q)
    qa_ref[0] = (_segnorm(q, HEAD_DIM) * g).astype(qa_ref.dtype)
    k, g = rows(A_OFF + A_Q, A_K)
    _emit_k(ka_ref, _segnorm(k, HEAD_DIM) * g)
    _emit_vt(va_ref, pt[A_OFF + A_Q + A_K:A_OFF + A_Q + A_K + A_V])
    q, g = rows(B_OFF, B_Q)
    qb_ref[0] = (_segnorm(q, B_QK) * g).astype(qb_ref.dtype)
    k, g = rows(B_OFF + B_Q, B_K)
    _emit_k(kb_ref, _segnorm(k, B_QK) * g)
    _emit_vt(vb_ref, pt[B_OFF + B_Q + B_K:B_OFF + B_Q + B_K + B_V])
    q, g = rows(C_OFF, C_Q)
    qc_ref[0] = _rope(_segnorm(q, HEAD_DIM) * g, cs).astype(qc_ref.dtype)
    k, g = rows(C_OFF + C_Q, C_K)
    _emit_k(kc_ref, _rope(_segnorm(k, HEAD_DIM) * g, cs))
    _emit_vt(vc_ref, pt[C_OFF + C_Q + C_K:C_OFF + C_Q + C_K + C_V])


def _inproj(x2d, mod3, g_attn, w_in_t, gcol, cs, bsz, s_len):
    n, d = x2d.shape
    tm = min(TM_PROJ, s_len)
    spb = s_len // tm

    def qt_spec(rows_):
        return pl.BlockSpec((1, rows_, tm), lambda i: (i // spb, 0, i % spb))

    def k_spec(nh):
        return pl.BlockSpec((1, nh, tm, K_LANES), lambda i: (i // spb, 0, i % spb, 0))

    def qt_shape(rows_):
        return jax.ShapeDtypeStruct((bsz, rows_, s_len), BF16)

    def k_shape(nh):
        return jax.ShapeDtypeStruct((bsz, nh, s_len, K_LANES), BF16)

    return pl.pallas_call(
        _inproj_kernel,
        grid=(n // tm,),
        in_specs=[
            pl.BlockSpec((tm, d), lambda i: (i, 0)),
            pl.BlockSpec((1, 1, N_MOD * d), lambda i: (i // spb, 0, 0)),
            pl.BlockSpec((1, d), lambda i: (0, 0)),
            pl.BlockSpec((IN_COLS, d), lambda i: (0, 0)),
            pl.BlockSpec((IN_COLS, 1), lambda i: (0, 0)),
            pl.BlockSpec((HEAD_DIM, tm), lambda i: (0, i % spb)),
        ],
        out_specs=[
            qt_spec(A_Q), k_spec(A_KV), qt_spec(A_V),
            qt_spec(B_Q), k_spec(B_HEADS), qt_spec(B_V),
            qt_spec(C_Q), k_spec(C_KV), qt_spec(C_V),
        ],
        out_shape=[
            qt_shape(A_Q), k_shape(A_KV), qt_shape(A_V),
            qt_shape(B_Q), k_shape(B_HEADS), qt_shape(B_V),
            qt_shape(C_Q), k_shape(C_KV), qt_shape(C_V),
        ],
        compiler_params=_cparams(("arbitrary",)),
        name="inproj",
    )(x2d, mod3, g_attn, w_in_t, gcol, cs)


def _pipe_init(m_ref, l_ref, alpha_ref, acc_ref, p_ref):
    m_ref[...] = jnp.full(m_ref.shape, NEG_BIG, F32)
    l_ref[...] = jnp.zeros(l_ref.shape, F32)
    alpha_ref[...] = jnp.ones(alpha_ref.shape, F32)
    acc_ref[...] = jnp.zeros(acc_ref.shape, F32)
    p_ref[...] = jnp.zeros(p_ref.shape, p_ref.dtype)


def _pipe_pv(vt, p_ref, alpha_ref, acc_ref):
    acc_ref[...] = alpha_ref[...] * acc_ref[...] + _dot(vt, p_ref[...])


def _pipe_sm(c, s_ref, bm_ref, m_ref, l_ref, alpha_ref, p_ref):
    m_old = m_ref[...]
    m_new = jnp.maximum(m_old, bm_ref[...] + c)
    alpha = jnp.exp2(m_old - m_new)
    p = jnp.exp2(s_ref[...] - (m_new - c))
    l_ref[...] = alpha * l_ref[...] + jnp.sum(p, axis=0, keepdims=True)
    p_ref[...] = p.astype(p_ref.dtype)
    m_ref[...] = m_new
    alpha_ref[...] = alpha


def _pipe_qk(kc, q_ref, s_ref, bm_ref, tile=None):
    s = _dot(kc, q_ref[...])
    if tile is not None:
        s = s + tile
    s_ref[...] = s
    bm_ref[...] = jnp.max(s, axis=0, keepdims=True)


def _pipe_scratch(tk, n):
    return [
        pltpu.VMEM((K_LANES, n), BF16),
        pltpu.VMEM((tk, n), F32),
        pltpu.VMEM((tk, n), BF16),
        pltpu.VMEM((1, n), F32),
        pltpu.VMEM((1, n), F32),
        pltpu.VMEM((1, n), F32),
        pltpu.VMEM((1, n), F32),
        pltpu.VMEM((HEAD_DIM, n), F32),
    ]


def _attn_c_kernel(q_ref, k_ref, v_ref, o_ref, qs_ref, s_ref, p_ref, bm_ref, m_ref, l_ref, alpha_ref, acc_ref,
                   *, tk):
    g = C_HEADS // C_KV
    tq = q_ref.shape[2]
    nk = k_ref.shape[2] // tk
    q3 = q_ref[0]
    qs_ref[0:HEAD_DIM] = jnp.concatenate([q3[h * HEAD_DIM:(h + 1) * HEAD_DIM] for h in range(g)], axis=1)
    qs_ref[HEAD_DIM:K_LANES] = jnp.zeros((K_LANES - HEAD_DIM, g * tq), qs_ref.dtype)
    _pipe_init(m_ref, l_ref, alpha_ref, acc_ref, p_ref)

    def kchunk(j):
        return k_ref[0, 0, pl.ds(pl.multiple_of(j * tk, tk), tk), :]

    def vchunk(j):
        return v_ref[0, :, pl.ds(pl.multiple_of(j * tk, tk), tk)]

    _pipe_qk(kchunk(0), qs_ref, s_ref, bm_ref)

    def step(j):
        _pipe_pv(vchunk(jnp.maximum(j - 1, 0)), p_ref, alpha_ref, acc_ref)
        _pipe_sm(0.0, s_ref, bm_ref, m_ref, l_ref, alpha_ref, p_ref)
        _pipe_qk(kchunk(j + 1), qs_ref, s_ref, bm_ref)

    def group(jg, carry):
        for u in range(C_UNROLL):
            step(C_UNROLL * jg + u)
        return carry

    n_full = nk - 1
    lax.fori_loop(0, n_full // C_UNROLL, group, 0)
    for j in range(n_full - n_full % C_UNROLL, n_full):
        step(j)
    if nk > 1:
        _pipe_pv(vchunk(nk - 2), p_ref, alpha_ref, acc_ref)
    _pipe_sm(0.0, s_ref, bm_ref, m_ref, l_ref, alpha_ref, p_ref)
    _pipe_pv(vchunk(nk - 1), p_ref, alpha_ref, acc_ref)
    o = acc_ref[...] / l_ref[...]
    o_ref[0] = jnp.concatenate([o[:, h * tq:(h + 1) * tq] for h in range(g)], axis=0).astype(o_ref.dtype)


def _attn_c(qt, k, vt):
    bsz, _, s_len = qt.shape
    g = C_HEADS // C_KV
    tq = min(TQ_C, s_len)
    tk = min(TK_C, s_len)
    return pl.pallas_call(
        functools.partial(_attn_c_kernel, tk=tk),
        grid=(bsz, C_KV, s_len // tq),
        in_specs=[
            pl.BlockSpec((1, g * HEAD_DIM, tq), lambda b, kv, i: (b, kv, i)),
            pl.BlockSpec((1, 1, s_len, K_LANES), lambda b, kv, i: (b, kv, 0, 0)),
            pl.BlockSpec((1, HEAD_DIM, s_len), lambda b, kv, i: (b, kv, 0)),
        ],
        out_specs=pl.BlockSpec((1, g * HEAD_DIM, tq), lambda b, kv, i: (b, kv, i)),
        out_shape=jax.ShapeDtypeStruct((bsz, C_Q, s_len), BF16),
        scratch_shapes=_pipe_scratch(tk, g * tq),
        compiler_params=_cparams(("arbitrary", "arbitrary", "arbitrary")),
        name="attn_c",
    )(qt, k, vt)


B_WIN = 3
B_TILES = 5
B_UNROLL = 2
C_UNROLL = 1


def _attn_b_kernel(far_ref, lam_ref, q_ref, k_ref, v_ref, bias_ref, gcol_ref, o_ref,
                   qs_ref, s_ref, p_ref, bm_ref, m_ref, l_ref, alpha_ref, acc_ref):
    h = pl.program_id(1)
    i = pl.program_id(2)
    t = q_ref.shape[2]
    nk = k_ref.shape[2] // t
    win = min(B_WIN, nk)
    nfar = nk - win
    q = q_ref[0]
    z = jnp.zeros((B_QK, t), q.dtype)
    qs_ref[0:HEAD_DIM] = jnp.concatenate(
        [jnp.concatenate([q[0:B_QK], z], axis=0), jnp.concatenate([z, q[B_QK:HEAD_DIM]], axis=0)], axis=1)
    qs_ref[HEAD_DIM:K_LANES] = jnp.zeros((K_LANES - HEAD_DIM, 2 * t), qs_ref.dtype)
    _pipe_init(m_ref, l_ref, alpha_ref, acc_ref, p_ref)
    c_neg = far_ref[h, 0]
    c_pos = far_ref[h, 1]
    w0 = jnp.clip(i - 1, 0, nk - win)

    def kchunk(j):
        return k_ref[0, 0, pl.ds(pl.multiple_of(j * t, t), t), :]

    def vchunk(j):
        return v_ref[0, :, pl.ds(pl.multiple_of(j * t, t), t)]

    def far_chunk(f):
        return jnp.where(f < w0, f, f + win)

    def qk_win(j):
        b = bias_ref[0, j - i + B_TILES // 2]
        _pipe_qk(kchunk(j), qs_ref, s_ref, bm_ref, jnp.concatenate([b, b], axis=1))

    def sm(c):
        _pipe_sm(c, s_ref, bm_ref, m_ref, l_ref, alpha_ref, p_ref)

    def pv(j):
        _pipe_pv(vchunk(j), p_ref, alpha_ref, acc_ref)

    qk_win(w0)
    for w in range(win):
        if w > 0:
            pv(w0 + w - 1)
        sm(0.0)
        if w + 1 < win:
            qk_win(w0 + w + 1)
        elif nfar > 0:
            _pipe_qk(kchunk(far_chunk(0)), qs_ref, s_ref, bm_ref)

    def far_step(f):
        pv(jnp.where(f == 0, w0 + win - 1, far_chunk(f - 1)))
        sm(jnp.where(f < w0, c_neg, c_pos))
        _pipe_qk(kchunk(far_chunk(f + 1)), qs_ref, s_ref, bm_ref)

    if nfar > 0:
        n_full = nfar - 1

        def group(fg, carry):
            for u in range(B_UNROLL):
                far_step(B_UNROLL * fg + u)
            return carry

        lax.fori_loop(0, n_full // B_UNROLL, group, 0)
        for f in range(n_full - n_full % B_UNROLL, n_full):
            far_step(f)
        pv(far_chunk(nfar - 2) if nfar > 1 else w0 + win - 1)
        sm(jnp.where(nfar - 1 < w0, c_neg, c_pos))
        pv(far_chunk(nfar - 1))
    else:
        pv(w0 + win - 1)

    o = acc_ref[...] / l_ref[...]
    o = o[:, 0:t] - lam_ref[0] * o[:, t:2 * t]
    ms = jnp.mean(o * o, axis=0, keepdims=True)
    o_ref[0] = (o * lax.rsqrt(ms + EPS) * gcol_ref[...]).astype(o_ref.dtype)


def _attn_b(qt, k, vt, bias_tiles, far, lam, gcol):
    bsz, _, s_len = qt.shape
    t = bias_tiles.shape[-1]
    return pl.pallas_call(
        _attn_b_kernel,
        grid_spec=pltpu.PrefetchScalarGridSpec(
            num_scalar_prefetch=2,
            grid=(bsz, B_HEADS, s_len // t),
            in_specs=[
                pl.BlockSpec((1, HEAD_DIM, t), lambda b, h, i, *_: (b, h, i)),
                pl.BlockSpec((1, 1, s_len, K_LANES), lambda b, h, i, *_: (b, h, 0, 0)),
                pl.BlockSpec((1, HEAD_DIM, s_len), lambda b, h, i, *_: (b, h, 0)),
                pl.BlockSpec((1, B_TILES, t, t), lambda b, h, i, *_: (h, 0, 0, 0)),
                pl.BlockSpec((HEAD_DIM, 1), lambda b, h, i, *_: (0, 0)),
            ],
            out_specs=pl.BlockSpec((1, HEAD_DIM, t), lambda b, h, i, *_: (b, h, i)),
            scratch_shapes=_pipe_scratch(t, 2 * t),
        ),
        out_shape=jax.ShapeDtypeStruct((bsz, B_Q, s_len), BF16),
        compiler_params=_cparams(("arbitrary", "arbitrary", "arbitrary")),
        name="attn_b",
    )(far, lam, qt, k, vt, bias_tiles, gcol)


SUM_ROWS = 16
SHIFT_LIMIT = 100.0
FAST_C_UNROLL = 5
FAST_B_UNROLL = 14


def _fast_qk_exp(kc, qaug_ref, p_ref, tile=None):
    s = _dot(kc, qaug_ref[...])
    if tile is not None:
        s = s + tile
    p_ref[...] = jnp.exp2(s).astype(p_ref.dtype)


def _fast_pv(vc, p_ref, acc_ref, cls):
    va = jnp.concatenate([vc, jnp.ones((SUM_ROWS, vc.shape[1]), vc.dtype)], axis=0)
    acc_ref[cls] = acc_ref[cls] + _dot(va, p_ref[...])


def _fast_scratch(tk, n, nacc):
    return [
        pltpu.VMEM((K_LANES, n), BF16),
        pltpu.VMEM((tk, n), BF16),
        pltpu.VMEM((nacc, HEAD_DIM + SUM_ROWS, n), F32),
    ]


def _shift_rows(shift, n):
    row = lax.broadcasted_iota(jnp.int32, (K_LANES - HEAD_DIM, n), 0)
    return jnp.where(row == 0, -shift, 0.0).astype(BF16)


def _attn_c_fast_kernel(shift_ref, q_ref, k_ref, v_ref, o_ref, qaug_ref, p_ref, acc_ref, *, tk):
    g = C_HEADS // C_KV
    tq = q_ref.shape[2]
    nk = k_ref.shape[2] // tk
    q3 = q_ref[0]
    qaug_ref[0:HEAD_DIM] = jnp.concatenate([q3[h * HEAD_DIM:(h + 1) * HEAD_DIM] for h in range(g)], axis=1)
    qaug_ref[HEAD_DIM:K_LANES] = _shift_rows(shift_ref[0], g * tq)
    acc_ref[...] = jnp.zeros(acc_ref.shape, F32)

    def kchunk(j):
        return k_ref[0, 0, pl.ds(pl.multiple_of(j * tk, tk), tk), :]

    def vchunk(j):
        return v_ref[0, :, pl.ds(pl.multiple_of(j * tk, tk), tk)]

    _fast_qk_exp(kchunk(0), qaug_ref, p_ref)

    def step(j):
        _fast_pv(vchunk(j - 1), p_ref, acc_ref, 0)
        _fast_qk_exp(kchunk(j), qaug_ref, p_ref)

    def group(jg, carry):
        for u in range(FAST_C_UNROLL):
            step(1 + FAST_C_UNROLL * jg + u)
        return carry

    n_full = nk - 1
    lax.fori_loop(0, n_full // FAST_C_UNROLL, group, 0)
    for j in range(1 + n_full - n_full % FAST_C_UNROLL, nk):
        step(j)
    _fast_pv(vchunk(nk - 1), p_ref, acc_ref, 0)
    acc = acc_ref[0]
    o = acc[0:HEAD_DIM] / acc[HEAD_DIM:HEAD_DIM + 1]
    o_ref[0] = jnp.concatenate([o[:, h * tq:(h + 1) * tq] for h in range(g)], axis=0).astype(o_ref.dtype)


def _attn_c_fast(qt, k, vt, shift):
    bsz, _, s_len = qt.shape
    g = C_HEADS // C_KV
    tq = min(TQ_C, s_len)
    tk = min(TK_C, s_len)
    return pl.pallas_call(
        functools.partial(_attn_c_fast_kernel, tk=tk),
        grid_spec=pltpu.PrefetchScalarGridSpec(
            num_scalar_prefetch=1,
            grid=(bsz, C_KV, s_len // tq),
            in_specs=[
                pl.BlockSpec((1, g * HEAD_DIM, tq), lambda b, kv, i, *_: (b, kv, i)),
                pl.BlockSpec((1, 1, s_len, K_LANES), lambda b, kv, i, *_: (b, kv, 0, 0)),
                pl.BlockSpec((1, HEAD_DIM, s_len), lambda b, kv, i, *_: (b, kv, 0)),
            ],
            out_specs=pl.BlockSpec((1, g * HEAD_DIM, tq), lambda b, kv, i, *_: (b, kv, i)),
            scratch_shapes=_fast_scratch(tk, g * tq, 1),
        ),
        out_shape=jax.ShapeDtypeStruct((bsz, C_Q, s_len), BF16),
        compiler_params=_cparams(("arbitrary", "arbitrary", "arbitrary")),
        name="attn_c_fast",
    )(shift, qt, k, vt)


def _attn_b_fast_kernel(far_ref, lam_ref, shift_ref, q_ref, k_ref, v_ref, bias_ref, gcol_ref, o_ref,
                        qaug_ref, p_ref, acc_ref):
    h = pl.program_id(1)
    i = pl.program_id(2)
    t = q_ref.shape[2]
    nk = k_ref.shape[2] // t
    win = min(B_WIN, nk)
    nfar = nk - win
    q = q_ref[0]
    z = jnp.zeros((B_QK, t), q.dtype)
    qaug_ref[0:HEAD_DIM] = jnp.concatenate(
        [jnp.concatenate([q[0:B_QK], z], axis=0), jnp.concatenate([z, q[B_QK:HEAD_DIM]], axis=0)], axis=1)
    qaug_ref[HEAD_DIM:K_LANES] = _shift_rows(shift_ref[h], 2 * t)
    acc_ref[...] = jnp.zeros(acc_ref.shape, F32)
    w0 = jnp.clip(i - 1, 0, nk - win)

    def kchunk(j):
        return k_ref[0, 0, pl.ds(pl.multiple_of(j * t, t), t), :]

    def vchunk(j):
        return v_ref[0, :, pl.ds(pl.multiple_of(j * t, t), t)]

    def far_chunk(f):
        return jnp.where(f < w0, f, f + win)

    def far_cls(f):
        return jnp.where(f < w0, 1, 2)

    def qk_win(j):
        b = bias_ref[0, j - i + B_TILES // 2]
        _fast_qk_exp(kchunk(j), qaug_ref, p_ref, jnp.concatenate([b, b], axis=1))

    qk_win(w0)
    for w in range(1, win):
        _fast_pv(vchunk(w0 + w - 1), p_ref, acc_ref, 0)
        qk_win(w0 + w)
    _fast_pv(vchunk(w0 + win - 1), p_ref, acc_ref, 0)

    if nfar > 0:
        _fast_qk_exp(kchunk(far_chunk(0)), qaug_ref, p_ref)

        def far_step(f):
            _fast_pv(vchunk(far_chunk(f - 1)), p_ref, acc_ref, far_cls(f - 1))
            _fast_qk_exp(kchunk(far_chunk(f)), qaug_ref, p_ref)

        n_full = nfar - 1

        def group(fg, carry):
            for u in range(FAST_B_UNROLL):
                far_step(1 + FAST_B_UNROLL * fg + u)
            return carry

        lax.fori_loop(0, n_full // FAST_B_UNROLL, group, 0)
        for f in range(1 + n_full - n_full % FAST_B_UNROLL, nfar):
            far_step(f)
        _fast_pv(vchunk(far_chunk(nfar - 1)), p_ref, acc_ref, far_cls(nfar - 1))

    wneg = jnp.exp2(jnp.full((1, 1), far_ref[h, 0], F32))
    wpos = jnp.exp2(jnp.full((1, 1), far_ref[h, 1], F32))
    acc = acc_ref[0] + wneg * acc_ref[1] + wpos * acc_ref[2]
    o = acc[0:HEAD_DIM] / acc[HEAD_DIM:HEAD_DIM + 1]
    o = o[:, 0:t] - lam_ref[0] * o[:, t:2 * t]
    ms = jnp.mean(o * o, axis=0, keepdims=True)
    o_ref[0] = (o * lax.rsqrt(ms + EPS) * gcol_ref[...]).astype(o_ref.dtype)


def _attn_b_fast(qt, k, vt, bias_tiles, far, lam, gcol, shift):
    bsz, _, s_len = qt.shape
    t = bias_tiles.shape[-1]
    return pl.pallas_call(
        _attn_b_fast_kernel,
        grid_spec=pltpu.PrefetchScalarGridSpec(
            num_scalar_prefetch=3,
            grid=(bsz, B_HEADS, s_len // t),
            in_specs=[
                pl.BlockSpec((1, HEAD_DIM, t), lambda b, h, i, *_: (b, h, i)),
                pl.BlockSpec((1, 1, s_len, K_LANES), lambda b, h, i, *_: (b, h, 0, 0)),
                pl.BlockSpec((1, HEAD_DIM, s_len), lambda b, h, i, *_: (b, h, 0)),
                pl.BlockSpec((1, B_TILES, t, t), lambda b, h, i, *_: (h, 0, 0, 0)),
                pl.BlockSpec((HEAD_DIM, 1), lambda b, h, i, *_: (0, 0)),
            ],
            out_specs=pl.BlockSpec((1, HEAD_DIM, t), lambda b, h, i, *_: (b, h, i)),
            scratch_shapes=_fast_scratch(t, 2 * t, 3),
        ),
        out_shape=jax.ShapeDtypeStruct((bsz, B_Q, s_len), BF16),
        compiler_params=_cparams(("arbitrary", "arbitrary", "arbitrary")),
        name="attn_b_fast",
    )(far, lam, shift, qt, k, vt, bias_tiles, gcol)


def _attn_a_kernel(sink_ref, q_ref, kp_ref, kc_ref, kn_ref, vp_ref, vc_ref, vn_ref, bias_ref, o_ref):
    n = pl.program_id(1)
    nb = pl.num_programs(1)
    g = A_HEADS // A_KV
    blk = A_BLK
    row = lax.broadcasted_iota(jnp.int32, (3 * blk, 1), 0)
    valid = ((row >= blk) | (n > 0)) & ((row < 2 * blk) | (n < nb - 1))
    gw = g * blk
    zero_q = jnp.zeros((K_LANES, gw), q_ref.dtype)
    qrows = []
    for kv in range(A_KV):
        q3 = q_ref[0, kv * g * HEAD_DIM:(kv + 1) * g * HEAD_DIM]
        qcat = jnp.concatenate([q3[h * HEAD_DIM:(h + 1) * HEAD_DIM] for h in range(g)], axis=1)
        qaug = jnp.concatenate([qcat, jnp.zeros((K_LANES - HEAD_DIM, gw), qcat.dtype)], axis=0)
        qrows.append(jnp.concatenate([qaug if c == kv else zero_q for c in range(A_KV)], axis=1))
    qbd = jnp.concatenate(qrows, axis=0)
    kwin = jnp.concatenate(
        [jnp.concatenate([r[0, kv] for r in (kp_ref, kc_ref, kn_ref)], axis=0) for kv in range(A_KV)], axis=1)
    vwin = jnp.concatenate([vp_ref[0], vc_ref[0], vn_ref[0]], axis=1)
    bias = jnp.concatenate([bias_ref[h] for h in range(A_HEADS)], axis=1)
    s = jnp.where(valid, _dot(kwin, qbd) + bias, NEG_BIG)
    sink = jnp.concatenate([jnp.full((1, blk), sink_ref[h], F32) for h in range(A_HEADS)], axis=1)
    m = jnp.maximum(jnp.max(s, axis=0, keepdims=True), sink)
    p = jnp.exp2(s - m)
    denom = jnp.sum(p, axis=0, keepdims=True) + jnp.exp2(sink - m)
    o_all = _dot(vwin, p.astype(BF16)) / denom
    for kv in range(A_KV):
        o = o_all[kv * HEAD_DIM:(kv + 1) * HEAD_DIM, kv * gw:(kv + 1) * gw]
        o_ref[0, kv * g * HEAD_DIM:(kv + 1) * g * HEAD_DIM] = jnp.concatenate(
            [o[:, h * blk:(h + 1) * blk] for h in range(g)], axis=0).astype(o_ref.dtype)


def _attn_a(qt, k, vt, bias, sink):
    bsz, _, s_len = qt.shape
    nb = s_len // A_BLK

    def kspec(off):
        return pl.BlockSpec((1, A_KV, A_BLK, K_LANES),
                            lambda b, n, *_: (b, 0, jnp.clip(n + off, 0, nb - 1), 0))

    def vspec(off):
        return pl.BlockSpec((1, A_V, A_BLK), lambda b, n, *_: (b, 0, jnp.clip(n + off, 0, nb - 1)))

    return pl.pallas_call(
        _attn_a_kernel,
        grid_spec=pltpu.PrefetchScalarGridSpec(
            num_scalar_prefetch=1,
            grid=(bsz, nb),
            in_specs=[
                pl.BlockSpec((1, A_Q, A_BLK), lambda b, n, *_: (b, 0, n)),
                kspec(-1), kspec(0), kspec(1), vspec(-1), vspec(0), vspec(1),
                pl.BlockSpec((A_HEADS, 3 * A_BLK, A_BLK), lambda b, n, *_: (0, 0, 0)),
            ],
            out_specs=pl.BlockSpec((1, A_Q, A_BLK), lambda b, n, *_: (b, 0, n)),
        ),
        out_shape=jax.ShapeDtypeStruct((bsz, A_Q, s_len), BF16),
        compiler_params=_cparams(("arbitrary", "arbitrary")),
        name="attn_a",
    )(sink, qt, k, k, k, vt, vt, vt, bias)


def _outproj_kernel(oa_ref, ob_ref, oc_ref, wa_ref, wb_ref, wc_ref, x_ref, mod_ref, g_ref,
                    wrh_ref, wrl_ref, x1_ref, h2_ref, aff_ref):
    def tn(a, b):
        return lax.dot_general(a, b, (((0,), (0,)), ((), ())), preferred_element_type=F32)

    out = tn(oa_ref[0], wa_ref[...]) + tn(ob_ref[0], wb_ref[...]) + tn(oc_ref[0], wc_ref[...])
    mod = mod_ref[0]
    d = out.shape[1]
    gt_a = mod[:, 2 * d:3 * d]
    sh_f, sc_f = mod[:, 3 * d:4 * d], mod[:, 4 * d:5 * d]
    x1 = x_ref[...] + gt_a * out
    x1_ref[...] = x1
    ms = jnp.mean(x1 * x1, axis=-1, keepdims=True)
    h2 = x1 * lax.rsqrt(ms + EPS) * g_ref[...]
    h2 = h2 * (1.0 + sc_f) + sh_f
    for c in range(d // LANES):
        h2_ref[:, c, :] = h2[:, c * LANES:(c + 1) * LANES]
    hi = h2.astype(BF16)
    lo = (h2 - hi.astype(F32)).astype(BF16)
    lt = _dot_nt(wrh_ref[...], hi) + _dot_nt(wrh_ref[...], lo) + _dot_nt(wrl_ref[...], hi)
    lt = lt - jnp.max(lt, axis=0, keepdims=True)
    e = jnp.exp(lt)
    aff_ref[...] = e / jnp.sum(e, axis=0, keepdims=True)


def _outproj(oa, ob, oc, wa, wb, wc, x2d, mod3, g_ffn, wr_hi, wr_lo, s_len):
    n, d = x2d.shape
    tm = min(TM_PROJ, s_len)
    spb = s_len // tm

    def ot_spec(rows_):
        return pl.BlockSpec((1, rows_, tm), lambda i: (i // spb, 0, i % spb))

    def full(shape):
        return pl.BlockSpec(shape, lambda i: tuple(0 for _ in shape))

    return pl.pallas_call(
        _outproj_kernel,
        grid=(n // tm,),
        in_specs=[
            ot_spec(A_Q), ot_spec(B_Q), ot_spec(C_Q),
            full((A_Q, d)), full((B_Q, d)), full((C_Q, d)),
            pl.BlockSpec((tm, d), lambda i: (i, 0)),
            pl.BlockSpec((1, 1, N_MOD * d), lambda i: (i // spb, 0, 0)),
            full((1, d)), full((N_EXPERTS, d)), full((N_EXPERTS, d)),
        ],
        out_specs=[
            pl.BlockSpec((tm, d), lambda i: (i, 0)),
            pl.BlockSpec((tm, d // LANES, LANES), lambda i: (i, 0, 0)),
            pl.BlockSpec((N_EXPERTS, tm), lambda i: (0, i)),
        ],
        out_shape=[
            jax.ShapeDtypeStruct((n, d), F32),
            jax.ShapeDtypeStruct((n, d // LANES, LANES), F32),
            jax.ShapeDtypeStruct((N_EXPERTS, n), F32),
        ],
        compiler_params=_cparams(("arbitrary",)),
        name="outproj",
    )(oa, ob, oc, wa, wb, wc, x2d, mod3, g_ffn, wr_hi, wr_lo)


def _routing_kernel(aff_ref, idx_ref, gate_ref, pexc_ref, thr_ref, need_ref, *, cap, pc):
    ne, nblk, _ = aff_ref.shape
    bits = lax.bitcast_convert_type(aff_ref[...], jnp.int32)

    def bis(it, t):
        cand = t | jnp.left_shift(jnp.int32(1), 30 - it)
        cnt = jnp.sum((bits >= cand).astype(F32), axis=(1, 2), keepdims=True)
        return jnp.where(cnt >= cap, cand, t)

    thr = lax.fori_loop(0, 31, bis, jnp.zeros((ne, 1, 1), jnp.int32))
    n_gt = jnp.sum((bits > thr).astype(F32), axis=(1, 2), keepdims=True)
    thr_ref[...] = jnp.broadcast_to(thr, thr_ref.shape)
    need_ref[...] = jnp.broadcast_to(cap - n_gt, need_ref.shape)

    def tri(nn, fn):
        a = lax.broadcasted_iota(jnp.int32, (nn, nn), 0)
        b = lax.broadcasted_iota(jnp.int32, (nn, nn), 1)
        return jnp.where(fn(a, b), 1.0, 0.0).astype(BF16)

    ones_l = jnp.ones((LANES, LANES), BF16)
    su_l = tri(LANES, lambda a, b: a < b)
    li_l = tri(LANES, lambda a, b: b <= a)
    sl_b = tri(nblk, lambda a, b: b < a)
    li_b = tri(nblk, lambda a, b: b <= a)
    blk_col = lax.broadcasted_iota(jnp.int32, (nblk, 1), 0).astype(F32)
    lane_col = lax.broadcasted_iota(jnp.int32, (LANES, 1), 0).astype(F32)

    def per_expert(e, carry):
        a = aff_ref[e]
        be = lax.bitcast_convert_type(a, jnp.int32)
        te = thr_ref[e]
        gt = be > te
        eq = (be == te)
        eqb = jnp.where(eq, 1.0, 0.0).astype(BF16)
        rank = _dot(eqb, su_l) + _dot(sl_b, _dot(eqb, ones_l).astype(BF16))
        sel = gt | (eq & (rank < need_ref[e]))
        mb = jnp.where(sel, 1.0, 0.0).astype(BF16)
        cnt_b = _dot(mb, ones_l)
        pinc = _dot(li_b, cnt_b.astype(BF16))
        pinc_col = pinc[:, 0:1]
        pexc_col = pinc_col - cnt_b[:, 0:1]
        pexc_ref[e] = (pinc - cnt_b).T[0:1].astype(jnp.int32)
        mt = jnp.where(sel, 1.0, 0.0).T.astype(BF16)
        at = a.T
        at_hi = at.astype(BF16)
        at_lo = (at - at_hi.astype(F32)).astype(BF16)
        for c in range(cap // pc):
            p_row = (lax.broadcasted_iota(jnp.int32, (1, pc), 1) + c * pc).astype(F32)
            blk_row = jnp.sum(jnp.where(pinc_col <= p_row, 1.0, 0.0), axis=0, keepdims=True)
            oh = blk_col == blk_row
            ohb = jnp.where(oh, 1.0, 0.0).astype(BF16)
            r_row = p_row - jnp.sum(jnp.where(oh, pexc_col, 0.0), axis=0, keepdims=True)
            gt_rows = _dot(mt, ohb)
            pc_incl = _dot(li_l, gt_rows.astype(BF16))
            lane_row = jnp.sum(jnp.where(pc_incl <= r_row, 1.0, 0.0), axis=0, keepdims=True)
            idx_ref[e, :, c * pc:(c + 1) * pc] = (blk_row * LANES + lane_row).astype(jnp.int32)
            ag = _dot(at_hi, ohb) + _dot(at_lo, ohb)
            gate_ref[e, :, c * pc:(c + 1) * pc] = jnp.sum(
                jnp.where(lane_col == lane_row, ag, 0.0), axis=0, keepdims=True)
        return carry

    lax.fori_loop(0, ne, per_expert, 0)


def _routing(aff3, cap):
    ne, nblk, _ = aff3.shape
    pc = min(ROUTE_PC, cap)
    return pl.pallas_call(
        functools.partial(_routing_kernel, cap=cap, pc=pc),
        grid=(1,),
        in_specs=[pl.BlockSpec((ne, nblk, LANES), lambda i: (0, 0, 0))],
        out_specs=[
            pl.BlockSpec((ne, 1, cap), lambda i: (0, 0, 0)),
            pl.BlockSpec((ne, 1, cap), lambda i: (0, 0, 0)),
            pl.BlockSpec((ne, 1, nblk), lambda i: (0, 0, 0)),
        ],
        out_shape=[
            jax.ShapeDtypeStruct((ne, 1, cap), jnp.int32),
            jax.ShapeDtypeStruct((ne, 1, cap), F32),
            jax.ShapeDtypeStruct((ne, 1, nblk), jnp.int32),
        ],
        scratch_shapes=[pltpu.VMEM((ne, 1, LANES), jnp.int32), pltpu.VMEM((ne, 1, LANES), F32)],
        compiler_params=_cparams(("arbitrary",)),
        name="routing",
    )(aff3)


TOK_LANES = LANES
TOK_RADIX = 256
IDX_SLOTS = 3


def _moe_kernel(idx_ref, idxn_ref, idxnn_ref, gate_ref, h_hbm, wg_ref, wu_ref, wd_ref, y_ref,
                idx_smem, xbuf, sems):
    ch = xbuf.shape[1]
    d = xbuf.shape[2] * xbuf.shape[3]
    nsteps = pl.num_programs(0) * pl.num_programs(1)
    s = pl.program_id(0) * pl.num_programs(1) + pl.program_id(1)
    slot = s % 2
    nslot = 1 - slot
    k1 = (s + 1) % IDX_SLOTS
    k2 = (s + 2) % IDX_SLOTS

    def idx_copy(src_ref, sl):
        return pltpu.make_async_copy(src_ref.at[0, 0], idx_smem.at[sl], sems.at[2 + sl])

    def row_copy(sl, r, t):
        return pltpu.make_async_copy(h_hbm.at[pl.ds(t, 1)], xbuf.at[sl, pl.ds(r, 1)], sems.at[sl])

    @pl.when(s == 0)
    def _():
        c0 = idx_copy(idx_ref, 0)
        c0.start()
        c0.wait()
        for r in range(ch):
            row_copy(0, r, idx_smem[0, r]).start()
        c1 = idx_copy(idxn_ref, 1)
        c1.start()
        c1.wait()

    idx_copy(idxnn_ref, k2).start()
    for r in range(ch):
        row_copy(slot, r, 0).wait()
    for r in range(ch):
        row_copy(nslot, r, idx_smem[k1, r]).start()

    x = jnp.concatenate([xbuf[slot, :, c, :] for c in range(xbuf.shape[2])], axis=1).astype(BF16)
    gg = _dot(x, wg_ref[0])
    uu = _dot(x, wu_ref[0])
    hid = (gg * (1.0 / (1.0 + jnp.exp(-gg))) * uu).astype(BF16)
    y = _dot(hid, wd_ref[0])
    sub = lax.broadcasted_iota(jnp.int32, (LANES, ch), 0)
    cols = jnp.where(sub == 0, gate_ref[0], jnp.where(sub == 1, idx_ref[0].astype(F32), 0.0)).T
    y_ref[:, 0:d] = (y * cols[:, 0:1]).astype(y_ref.dtype)
    tok = cols[:, 1:2]
    hi = jnp.floor(tok * (1.0 / TOK_RADIX))
    lane = lax.broadcasted_iota(jnp.int32, (ch, TOK_LANES), 1)
    y_ref[:, d:d + TOK_LANES] = jnp.where(
        lane == 0, hi, jnp.where(lane == 1, tok - hi * TOK_RADIX, 0.0)).astype(y_ref.dtype)

    idx_copy(idxnn_ref, k2).wait()

    @pl.when(s == nsteps - 1)
    def _():
        for r in range(ch):
            row_copy(nslot, r, 0).wait()


def _moe(idx3, gate3, h2, wg, wu, wd, cap):
    n, dt, _ = h2.shape
    d = dt * LANES
    ne, _, ff = wg.shape
    ch = min(MOE_CH, cap)
    nch = cap // ch
    nsteps = ne * nch
    return pl.pallas_call(
        _moe_kernel,
        grid=(ne, nch),
        in_specs=[
            pl.BlockSpec((1, 1, ch), lambda e, c: (e * nch + c, 0, 0)),
            pl.BlockSpec((1, 1, ch), lambda e, c: (jnp.minimum(e * nch + c + 1, nsteps - 1), 0, 0)),
            pl.BlockSpec((1, 1, ch), lambda e, c: (jnp.minimum(e * nch + c + 2, nsteps - 1), 0, 0)),
            pl.BlockSpec((1, 1, ch), lambda e, c: (e * nch + c, 0, 0)),
            pl.BlockSpec(memory_space=pl.ANY),
            pl.BlockSpec((1, d, ff), lambda e, c: (e, 0, 0)),
            pl.BlockSpec((1, d, ff), lambda e, c: (e, 0, 0)),
            pl.BlockSpec((1, ff, d), lambda e, c: (e, 0, 0)),
        ],
        out_specs=pl.BlockSpec((ch, d + TOK_LANES), lambda e, c: (e * nch + c, 0)),
        out_shape=jax.ShapeDtypeStruct((ne * cap, d + TOK_LANES), BF16),
        scratch_shapes=[
            pltpu.SMEM((IDX_SLOTS, ch), jnp.int32),
            pltpu.VMEM((2, ch, dt, LANES), F32),
            pltpu.SemaphoreType.DMA((2 + IDX_SLOTS,)),
        ],
        compiler_params=_cparams(("arbitrary", "arbitrary")),
        name="moe_ffn",
    )(idx3.reshape(nsteps, 1, ch), idx3.reshape(nsteps, 1, ch), idx3.reshape(nsteps, 1, ch),
      gate3.reshape(nsteps, 1, ch), h2, wg, wu, wd)


def _combine_kernel(starts_ref, y_hbm, x_ref, mod_ref, o_ref, ybuf, xtra, sems, *, cap, ne):
    tt, d = x_ref.shape
    wr = ybuf.shape[2]
    nt = pl.num_programs(0)
    t = pl.program_id(0)
    slot = t % 2
    nslot = 1 - slot

    def bounds(e, tile):
        lo = starts_ref[e * (nt + 1) + tile]
        hi = starts_ref[e * (nt + 1) + tile + 1]
        base = jnp.minimum((lo // 16) * 16, cap - wr)
        return lo, hi, base

    def win_copy(e, base, sl):
        return pltpu.make_async_copy(y_hbm.at[pl.ds(pl.multiple_of(e * cap + base, 16), wr)],
                                     ybuf.at[sl, e], sems.at[sl])

    def fetch(tile, sl):
        for e in range(ne):
            win_copy(e, bounds(e, tile)[2], sl).start()

    @pl.when(t == 0)
    def _():
        fetch(0, 0)

    for e in range(ne):
        win_copy(e, 0, slot).wait()
    fetch(jnp.minimum(t + 1, nt - 1), nslot)

    tok0 = (t * tt).astype(F32)
    lane_tok = lax.broadcasted_iota(jnp.int32, (1, tt), 1).astype(F32) + tok0
    row = lax.broadcasted_iota(jnp.int32, (wr, 1), 0)

    def one_hot(yb, first, lo_eff, hi):
        tokc = yb[:, d:d + 1].astype(F32) * TOK_RADIX + yb[:, d + 1:d + 2].astype(F32)
        pos = row + first
        return jnp.where((pos >= lo_eff) & (pos < hi) & (tokc == lane_tok), 1.0, 0.0).astype(BF16)

    def scatter_rows(oh, rows):
        return lax.dot_general(oh, rows, (((0,), (0,)), ((), ())), preferred_element_type=F32)

    ohs = []
    for e in range(ne):
        lo, hi, base = bounds(e, t)
        ohs.append(one_hot(ybuf[slot, e], base, lo, hi))
    rows_all = ybuf[slot].reshape(ne * wr, d + TOK_LANES)[:, 0:d]
    acc = scatter_rows(jnp.concatenate(ohs, axis=0), rows_all)
    o_ref[...] = x_ref[...] + mod_ref[0][:, 5 * d:6 * d] * acc

    for e in range(ne):
        lo, hi, base = bounds(e, t)
        n_more = jnp.maximum(hi - (base + wr) + wr - 1, 0) // wr

        def more(k, carry, e=e, hi=hi, base=base):
            done = base + wr * (k + 1)
            first = jnp.minimum(done, cap - wr)
            cp = pltpu.make_async_copy(y_hbm.at[pl.ds(pl.multiple_of(e * cap + first, 16), wr)], xtra, sems.at[2])
            cp.start()
            cp.wait()
            yb = xtra[...]
            o_ref[...] = o_ref[...] + mod_ref[0][:, 5 * d:6 * d] * scatter_rows(
                one_hot(yb, first, done, hi), yb[:, 0:d])
            return carry

        lax.fori_loop(0, n_more, more, 0)

    @pl.when(t == nt - 1)
    def _():
        for e in range(ne):
            win_copy(e, 0, nslot).wait()


def _combine(starts, y, x1, mod3, s_len, cap):
    n, d = x1.shape
    ne = y.shape[0] // cap
    tt = min(COMBINE_TT, s_len)
    spb = s_len // tt
    wr = min(COMBINE_WR, cap)
    return pl.pallas_call(
        functools.partial(_combine_kernel, cap=cap, ne=ne),
        grid_spec=pltpu.PrefetchScalarGridSpec(
            num_scalar_prefetch=1,
            grid=(n // tt,),
            in_specs=[
                pl.BlockSpec(memory_space=pl.ANY),
                pl.BlockSpec((tt, d), lambda i, *_: (i, 0)),
                pl.BlockSpec((1, 1, N_MOD * d), lambda i, *_: (i // spb, 0, 0)),
            ],
            out_specs=pl.BlockSpec((tt, d), lambda i, *_: (i, 0)),
            scratch_shapes=[
                pltpu.VMEM((2, ne, wr, d + TOK_LANES), BF16),
                pltpu.VMEM((wr, d + TOK_LANES), BF16),
                pltpu.SemaphoreType.DMA((3,)),
            ],
        ),
        out_shape=jax.ShapeDtypeStruct((n, d), F32),
        compiler_params=_cparams(("arbitrary",)),
        name="combine",
    )(starts, y, x1, mod3)


def _t5_bucket(rel):
    half = NUM_BUCKETS // 2
    max_exact = half // 2
    n = jnp.abs(rel)
    nf = jnp.maximum(n, 1).astype(F32)
    large = max_exact + (jnp.log(nf / max_exact) / math.log(MAX_DISTANCE / max_exact)
                         * (half - max_exact)).astype(jnp.int32)
    large = jnp.minimum(large, half - 1)
    return jnp.where(rel > 0, half, 0) + jnp.where(n < max_exact, n, large)


def _rope_table(s_len):
    quarter = HEAD_DIM // 4
    freqs = ROPE_THETA ** (-jnp.arange(quarter, dtype=F32) / quarter)
    t = jnp.arange(s_len)
    ang_r = freqs[:, None] * (t // GRID_W).astype(F32)[None, :]
    ang_c = freqs[:, None] * (t % GRID_W).astype(F32)[None, :]
    return jnp.concatenate([jnp.cos(ang_r), jnp.sin(ang_r), jnp.cos(ang_c), jnp.sin(ang_c)], axis=0)


def _toeplitz(fn, nj, ni):
    period = nj + ni
    m = jnp.arange(period)
    b = fn(jnp.where(m < nj, m, m - period))
    flat = jnp.tile(b, ni)[..., :ni * (period - 1)]
    rows = flat.reshape(b.shape[:-1] + (ni, period - 1))
    return jnp.swapaxes(rows[..., :nj], -1, -2)


def _bias_tables(rel_bias, t_b):
    def bias_of(rel, heads):
        return rel_bias[_t5_bucket(rel)][:, heads].T * LOG2E

    def fn_a(r):
        rel = r - A_BLK
        return jnp.where(jnp.abs(rel) <= WINDOW, bias_of(rel, slice(0, A_HEADS)), NEG_BIG)

    bias_a = _toeplitz(fn_a, 3 * A_BLK, A_BLK).astype(F32)
    tiles = [
        _toeplitz(lambda r, dd=dd: bias_of(dd * t_b + r, slice(A_HEADS, None)), t_b, t_b)
        for dd in range(-(B_TILES // 2), B_TILES // 2 + 1)
    ]
    bias_b = jnp.stack(tiles, axis=1).astype(F32)
    half = NUM_BUCKETS // 2
    far = jnp.stack([rel_bias[half - 1, A_HEADS:], rel_bias[NUM_BUCKETS - 1, A_HEADS:]], axis=1) * LOG2E
    return bias_a, bias_b, far.astype(F32)


def _gain_column(ga, gb, gc):
    sa = HEAD_DIM ** -0.5 * LOG2E
    sb = B_QK ** -0.5 * LOG2E
    parts = [
        jnp.tile(ga[0] * sa, A_HEADS), jnp.tile(ga[1], A_KV), jnp.ones((A_V,), F32),
        jnp.tile(gb[0] * sb, 2 * B_HEADS), jnp.tile(gb[1], 2 * B_HEADS), jnp.ones((B_V,), F32),
        jnp.tile(gc[0] * sa, C_HEADS), jnp.tile(gc[1], C_KV), jnp.ones((C_V,), F32),
    ]
    return jnp.concatenate(parts).reshape(IN_COLS, 1).astype(F32)


def _trunk(x, mod_l, prep, s_len, bsz):
    n = bsz * s_len
    d = x.shape[-1]
    x2d = x.reshape(n, d)
    t_b = min(T_B, s_len)
    cap = CAPACITY_FACTOR * n // N_EXPERTS
    cs = _rope_table(s_len)
    for l, lp in enumerate(prep["layers"]):
        mod3 = mod_l[l][:, None, :]
        qa, ka, va, qb, kb, vb, qc, kc, vc = _inproj(
            x2d, mod3, lp["g_attn"], lp["w_in_t"], lp["gcol"], cs, bsz, s_len)
        oa = _attn_a(qa, ka, va, prep["bias_a"], lp["sink"])
        bias_b = prep["bias_b"][t_b]
        ob = lax.cond(
            lp["b_bounded"],
            lambda q, k, v: _attn_b_fast(q, k, v, bias_b, prep["far_b"], lp["lam"], lp["subln_col"], lp["b_shift"]),
            lambda q, k, v: _attn_b(q, k, v, bias_b, prep["far_b"], lp["lam"], lp["subln_col"]),
            qb, kb, vb)
        oc = lax.cond(
            lp["c_bounded"],
            lambda q, k, v: _attn_c_fast(q, k, v, lp["c_shift"]),
            _attn_c,
            qc, kc, vc)
        x1, h2, aff = _outproj(oa, ob, oc, lp["wo_a"], lp["wo_b"], lp["wo_c"], x2d, mod3,
                               lp["g_ffn"], lp["wr_hi"], lp["wr_lo"], s_len)
        idx3, gate3, pexc = _routing(aff.reshape(N_EXPERTS, n // LANES, LANES), cap)
        y = _moe(idx3, gate3, h2, lp["wg"], lp["wu"], lp["wd"], cap)
        tt = min(COMBINE_TT, s_len)
        starts = jnp.concatenate(
            [pexc[:, 0, ::tt // LANES], jnp.full((N_EXPERTS, 1), cap, jnp.int32)], axis=1).reshape(-1)
        x2d = _combine(starts, y, x1, mod3, s_len, cap)
    return x2d.reshape(bsz, s_len, d)


def kernel(x_prompt, x_sample, c_prompt, c_sample, w_mod, b_mod, g_attn, g_ffn, w_in, w_out,
           qk_gain_a, qk_gain_b, qk_gain_c, sink_a, lam_b, subln_b, rel_bias,
           w_router, w_gate, w_up, w_down):
    depth = w_mod.shape[0]
    bp, sp, d = x_prompt.shape
    bs, ss, _ = x_sample.shape
    rows = -(-(bp + bs) // 16) * 16
    c_all = jnp.zeros((rows, d), F32).at[:bp].set(c_prompt).at[bp:bp + bs].set(c_sample)
    mod_all = _modulation(c_all, w_mod, b_mod)
    mod_p, mod_s = mod_all[:, :bp], mod_all[:, bp:bp + bs]

    tbs = sorted({min(T_B, sp), min(T_B, ss)})
    tabs = {t: _bias_tables(rel_bias, t) for t in tbs}
    prep = {"bias_a": tabs[tbs[0]][0], "far_b": tabs[tbs[0]][2],
            "bias_b": {t: tabs[t][1] for t in tbs}, "layers": []}
    for l in range(depth):
        lam_init = 0.8 - 0.6 * math.exp(-0.3 * l)
        lp = lam_b[l].astype(F32)
        lam = jnp.exp(jnp.sum(lp[0] * lp[1])) - jnp.exp(jnp.sum(lp[2] * lp[3])) + lam_init
        wo = w_out[l].astype(BF16)
        wr_t = w_router[l].T
        wr_hi = wr_t.astype(BF16)
        gmax = lambda g_: jnp.max(jnp.abs(g_))
        c_c = 1.02 * HEAD_DIM * gmax(qk_gain_c[l][0]) * gmax(qk_gain_c[l][1]) * (HEAD_DIM ** -0.5 * LOG2E)
        c_b = 1.02 * B_QK * gmax(qk_gain_b[l][0]) * gmax(qk_gain_b[l][1]) * (B_QK ** -0.5 * LOG2E)
        bias_vals = rel_bias[:, A_HEADS:] * LOG2E
        prep["layers"].append({
            "c_shift": c_c.reshape(1).astype(F32),
            "c_bounded": 2.0 * c_c <= SHIFT_LIMIT,
            "b_shift": (c_b + jnp.max(bias_vals, axis=0)).astype(F32),
            "b_bounded": 2.0 * c_b + 2.0 * jnp.max(jnp.abs(bias_vals)) <= SHIFT_LIMIT,
            "g_attn": g_attn[l].reshape(1, d), "g_ffn": g_ffn[l].reshape(1, d),
            "w_in_t": w_in[l].T.astype(BF16),
            "gcol": _gain_column(qk_gain_a[l], qk_gain_b[l], qk_gain_c[l]),
            "sink": (sink_a[l] * LOG2E).astype(F32),
            "lam": lam.reshape(1).astype(F32),
            "subln_col": (subln_b[l] * (1.0 - lam_init)).reshape(HEAD_DIM, 1).astype(F32),
            "wo_a": wo[0:A_Q], "wo_b": wo[A_Q:A_Q + B_Q], "wo_c": wo[A_Q + B_Q:],
            "wr_hi": wr_hi, "wr_lo": (wr_t - wr_hi.astype(F32)).astype(BF16),
            "wg": w_gate[l].astype(BF16), "wu": w_up[l].astype(BF16), "wd": w_down[l].astype(BF16),
        })
    y_p = _trunk(x_prompt, mod_p, prep, sp, bp)
    y_s = _trunk(x_sample, mod_s, prep, ss, bs)
    return (y_p, y_s)
```

```python
import functools
import math

import jax
import jax.numpy as jnp
from jax import lax
from jax.experimental import pallas as pl
from jax.experimental.pallas import tpu as pltpu

F32 = jnp.float32
BF16 = jnp.bfloat16

HEAD_DIM = 64
A_HEADS, A_KV = 6, 2
B_HEADS, B_QK = 4, 32
C_HEADS, C_KV = 6, 2
WINDOW = 128
A_BLK = 128
NUM_BUCKETS, MAX_DISTANCE = 32, 128
ROPE_THETA = 10000.0
GRID_W = 64
N_EXPERTS = 16
CAPACITY_FACTOR = 2
N_MOD = 6
EPS = 1e-6
NEG_BIG = -1e30
LOG2E = math.log2(math.e)

A_Q, A_K, A_V = A_HEADS * HEAD_DIM, A_KV * HEAD_DIM, A_KV * HEAD_DIM
B_Q, B_K, B_V = B_HEADS * HEAD_DIM, B_HEADS * HEAD_DIM, B_HEADS * HEAD_DIM
C_Q, C_K, C_V = C_HEADS * HEAD_DIM, C_KV * HEAD_DIM, C_KV * HEAD_DIM
A_OFF = 0
B_OFF = A_Q + A_K + A_V
C_OFF = B_OFF + B_Q + B_K + B_V
IN_COLS = C_OFF + C_Q + C_K + C_V

LANES = 128
K_LANES = LANES
VMEM_LIMIT_BYTES = 56 * 1024 * 1024

TM_PROJ = 512
TQ_C, TK_C = 512, 1024
T_B = 512
MOE_CH = 256
ROUTE_PC = 512
COMBINE_TT = 512
COMBINE_WR = 128


def _cparams(sem):
    return pltpu.CompilerParams(dimension_semantics=sem, vmem_limit_bytes=VMEM_LIMIT_BYTES)


def _dot(a, b):
    return jnp.dot(a, b, preferred_element_type=F32)


def _dot_nt(a, b):
    return lax.dot_general(a, b, (((1,), (1,)), ((), ())), preferred_element_type=F32)


def _mod_kernel(c_ref, w_ref, b_ref, o_ref):
    c = c_ref[...]
    a = c * (1.0 / (1.0 + jnp.exp(-c)))
    a_hi = a.astype(BF16)
    a_lo = (a - a_hi.astype(F32)).astype(BF16)
    w = w_ref[0]
    w_hi = w.astype(BF16)
    w_lo = (w - w_hi.astype(F32)).astype(BF16)
    acc = _dot(a_hi, w_hi) + _dot(a_lo, w_hi) + _dot(a_hi, w_lo)
    o_ref[0] = acc + b_ref[0]


def _modulation(c_all, w_mod, b_mod):
    depth, d, nm = w_mod.shape
    rows = c_all.shape[0]
    bn = 1024
    return pl.pallas_call(
        _mod_kernel,
        grid=(depth, nm // bn),
        in_specs=[
            pl.BlockSpec((rows, d), lambda l, j: (0, 0)),
            pl.BlockSpec((1, d, bn), lambda l, j: (l, 0, j)),
            pl.BlockSpec((1, 1, bn), lambda l, j: (l, 0, j)),
        ],
        out_specs=pl.BlockSpec((1, rows, bn), lambda l, j: (l, 0, j)),
        out_shape=jax.ShapeDtypeStruct((depth, rows, nm), F32),
        compiler_params=_cparams(("arbitrary", "arbitrary")),
        name="modulation",
    )(c_all, w_mod, b_mod.reshape(depth, 1, nm))


def _segnorm(rows, seglen):
    r, t = rows.shape
    r3 = rows.reshape(r // seglen, seglen, t)
    ms = jnp.mean(r3 * r3, axis=1, keepdims=True)
    return (r3 * lax.rsqrt(ms + EPS)).reshape(r, t)


def _rope(rows, cs):
    r, t = rows.shape
    x = rows.reshape(r // HEAD_DIM, HEAD_DIM, t)
    cr, sr, cc, sc = cs[None, 0:16], cs[None, 16:32], cs[None, 32:48], cs[None, 48:64]
    ar, br, ac, bc = x[:, 0:16], x[:, 16:32], x[:, 32:48], x[:, 48:64]
    out = jnp.concatenate(
        [ar * cr - br * sr, br * cr + ar * sr, ac * cc - bc * sc, bc * cc + ac * sc], axis=1)
    return out.reshape(r, t)


def _emit_k(k_ref, kt):
    lane = lax.broadcasted_iota(jnp.int32, (1, K_LANES), 1)
    pad = jnp.where(lane == HEAD_DIM, 1.0, 0.0)
    for p in range(kt.shape[0] // LANES):
        tok = kt[p * LANES:(p + 1) * LANES].T
        k_ref[0, 2 * p] = jnp.where(lane < HEAD_DIM, tok, pad).astype(k_ref.dtype)
        k_ref[0, 2 * p + 1] = jnp.where(lane < HEAD_DIM, pltpu.roll(tok, HEAD_DIM, 1), pad).astype(k_ref.dtype)


def _emit_vt(v_ref, vt):
    v_ref[0] = vt.astype(v_ref.dtype)


def _inproj_kernel(x_ref, mod_ref, g_ref, w_ref, gcol_ref, cs_ref,
                   qa_ref, ka_ref, va_ref, qb_ref, kb_ref, vb_ref, qc_ref, kc_ref, vc_ref):
    x = x_ref[...]
    mod = mod_ref[0]
    d = x.shape[1]
    sh, sc = mod[:, 0:d], mod[:, d:2 * d]
    ms = jnp.mean(x * x, axis=-1, keepdims=True)
    h = x * lax.rsqrt(ms + EPS) * g_ref[...]
    h = (h * (1.0 + sc) + sh).astype(BF16)
    pt = _dot_nt(w_ref[...], h)
    gcol = gcol_ref[...]
    cs = cs_ref[...]

    def rows(off, n):
        return pt[off:off + n], gcol[off:off + n]

    q, g = rows(A_OFF, A_Q)
    qa_ref[0] = (_segnorm(q, HEAD_DIM) * g).astype(qa_ref.dtype)
    k, g = rows(A_OFF + A_Q, A_K)
    _emit_k(ka_ref, _segnorm(k, HEAD_DIM) * g)
    _emit_vt(va_ref, pt[A_OFF + A_Q + A_K:A_OFF + A_Q + A_K + A_V])
    q, g = rows(B_OFF, B_Q)
    qb_ref[0] = (_segnorm(q, B_QK) * g).astype(qb_ref.dtype)
    k, g = rows(B_OFF + B_Q, B_K)
    _emit_k(kb_ref, _segnorm(k, B_QK) * g)
    _emit_vt(vb_ref, pt[B_OFF + B_Q + B_K:B_OFF + B_Q + B_K + B_V])
    q, g = rows(C_OFF, C_Q)
    qc_ref[0] = _rope(_segnorm(q, HEAD_DIM) * g, cs).astype(qc_ref.dtype)
    k, g = rows(C_OFF + C_Q, C_K)
    _emit_k(kc_ref, _rope(_segnorm(k, HEAD_DIM) * g, cs))
    _emit_vt(vc_ref, pt[C_OFF + C_Q + C_K:C_OFF + C_Q + C_K + C_V])


def _inproj(x2d, mod3, g_attn, w_in_t, gcol, cs, bsz, s_len):
    n, d = x2d.shape
    tm = min(TM_PROJ, s_len)
    spb = s_len // tm

    def qt_spec(rows_):
        return pl.BlockSpec((1, rows_, tm), lambda i: (i // spb, 0, i % spb))

    def k_spec(nh):
        return pl.BlockSpec((1, nh, tm, K_LANES), lambda i: (i // spb, 0, i % spb, 0))

    def qt_shape(rows_):
        return jax.ShapeDtypeStruct((bsz, rows_, s_len), BF16)

    def k_shape(nh):
        return jax.ShapeDtypeStruct((bsz, nh, s_len, K_LANES), BF16)

    return pl.pallas_call(
        _inproj_kernel,
        grid=(n // tm,),
        in_specs=[
            pl.BlockSpec((tm, d), lambda i: (i, 0)),
            pl.BlockSpec((1, 1, N_MOD * d), lambda i: (i // spb, 0, 0)),
            pl.BlockSpec((1, d), lambda i: (0, 0)),
            pl.BlockSpec((IN_COLS, d), lambda i: (0, 0)),
            pl.BlockSpec((IN_COLS, 1), lambda i: (0, 0)),
            pl.BlockSpec((HEAD_DIM, tm), lambda i: (0, i % spb)),
        ],
        out_specs=[
            qt_spec(A_Q), k_spec(A_KV), qt_spec(A_V),
            qt_spec(B_Q), k_spec(B_HEADS), qt_spec(B_V),
            qt_spec(C_Q), k_spec(C_KV), qt_spec(C_V),
        ],
        out_shape=[
            qt_shape(A_Q), k_shape(A_KV), qt_shape(A_V),
            qt_shape(B_Q), k_shape(B_HEADS), qt_shape(B_V),
            qt_shape(C_Q), k_shape(C_KV), qt_shape(C_V),
        ],
        compiler_params=_cparams(("arbitrary",)),
        name="inproj",
    )(x2d, mod3, g_attn, w_in_t, gcol, cs)


def _pipe_init(m_ref, l_ref, alpha_ref, acc_ref, p_ref):
    m_ref[...] = jnp.full(m_ref.shape, NEG_BIG, F32)
    l_ref[...] = jnp.zeros(l_ref.shape, F32)
    alpha_ref[...] = jnp.ones(alpha_ref.shape, F32)
    acc_ref[...] = jnp.zeros(acc_ref.shape, F32)
    p_ref[...] = jnp.zeros(p_ref.shape, p_ref.dtype)


def _pipe_pv(vt, p_ref, alpha_ref, acc_ref):
    acc_ref[...] = alpha_ref[...] * acc_ref[...] + _dot(vt, p_ref[...])


def _pipe_sm(c, s_ref, bm_ref, m_ref, l_ref, alpha_ref, p_ref):
    m_old = m_ref[...]
    m_new = jnp.maximum(m_old, bm_ref[...] + c)
    alpha = jnp.exp2(m_old - m_new)
    p = jnp.exp2(s_ref[...] - (m_new - c))
    l_ref[...] = alpha * l_ref[...] + jnp.sum(p, axis=0, keepdims=True)
    p_ref[...] = p.astype(p_ref.dtype)
    m_ref[...] = m_new
    alpha_ref[...] = alpha


def _pipe_qk(kc, q_ref, s_ref, bm_ref, tile=None):
    s = _dot(kc, q_ref[...])
    if tile is not None:
        s = s + tile
    s_ref[...] = s
    bm_ref[...] = jnp.max(s, axis=0, keepdims=True)


def _pipe_scratch(tk, n):
    return [
        pltpu.VMEM((K_LANES, n), BF16),
        pltpu.VMEM((tk, n), F32),
        pltpu.VMEM((tk, n), BF16),
        pltpu.VMEM((1, n), F32),
        pltpu.VMEM((1, n), F32),
        pltpu.VMEM((1, n), F32),
        pltpu.VMEM((1, n), F32),
        pltpu.VMEM((HEAD_DIM, n), F32),
    ]


def _attn_c_kernel(q_ref, k_ref, v_ref, o_ref, qs_ref, s_ref, p_ref, bm_ref, m_ref, l_ref, alpha_ref, acc_ref,
                   *, tk):
    g = C_HEADS // C_KV
    tq = q_ref.shape[2]
    nk = k_ref.shape[2] // tk
    q3 = q_ref[0]
    qs_ref[0:HEAD_DIM] = jnp.concatenate([q3[h * HEAD_DIM:(h + 1) * HEAD_DIM] for h in range(g)], axis=1)
    qs_ref[HEAD_DIM:K_LANES] = jnp.zeros((K_LANES - HEAD_DIM, g * tq), qs_ref.dtype)
    _pipe_init(m_ref, l_ref, alpha_ref, acc_ref, p_ref)

    def kchunk(j):
        return k_ref[0, 0, pl.ds(pl.multiple_of(j * tk, tk), tk), :]

    def vchunk(j):
        return v_ref[0, :, pl.ds(pl.multiple_of(j * tk, tk), tk)]

    _pipe_qk(kchunk(0), qs_ref, s_ref, bm_ref)

    def step(j):
        _pipe_pv(vchunk(jnp.maximum(j - 1, 0)), p_ref, alpha_ref, acc_ref)
        _pipe_sm(0.0, s_ref, bm_ref, m_ref, l_ref, alpha_ref, p_ref)
        _pipe_qk(kchunk(j + 1), qs_ref, s_ref, bm_ref)

    def group(jg, carry):
        for u in range(C_UNROLL):
            step(C_UNROLL * jg + u)
        return carry

    n_full = nk - 1
    lax.fori_loop(0, n_full // C_UNROLL, group, 0)
    for j in range(n_full - n_full % C_UNROLL, n_full):
        step(j)
    if nk > 1:
        _pipe_pv(vchunk(nk - 2), p_ref, alpha_ref, acc_ref)
    _pipe_sm(0.0, s_ref, bm_ref, m_ref, l_ref, alpha_ref, p_ref)
    _pipe_pv(vchunk(nk - 1), p_ref, alpha_ref, acc_ref)
    o = acc_ref[...] / l_ref[...]
    o_ref[0] = jnp.concatenate([o[:, h * tq:(h + 1) * tq] for h in range(g)], axis=0).astype(o_ref.dtype)


def _attn_c(qt, k, vt):
    bsz, _, s_len = qt.shape
    g = C_HEADS // C_KV
    tq = min(TQ_C, s_len)
    tk = min(TK_C, s_len)
    return pl.pallas_call(
        functools.partial(_attn_c_kernel, tk=tk),
        grid=(bsz, C_KV, s_len // tq),
        in_specs=[
            pl.BlockSpec((1, g * HEAD_DIM, tq), lambda b, kv, i: (b, kv, i)),
            pl.BlockSpec((1, 1, s_len, K_LANES), lambda b, kv, i: (b, kv, 0, 0)),
            pl.BlockSpec((1, HEAD_DIM, s_len), lambda b, kv, i: (b, kv, 0)),
        ],
        out_specs=pl.BlockSpec((1, g * HEAD_DIM, tq), lambda b, kv, i: (b, kv, i)),
        out_shape=jax.ShapeDtypeStruct((bsz, C_Q, s_len), BF16),
        scratch_shapes=_pipe_scratch(tk, g * tq),
        compiler_params=_cparams(("arbitrary", "arbitrary", "arbitrary")),
        name="attn_c",
    )(qt, k, vt)


B_WIN = 3
B_TILES = 5
B_UNROLL = 2
C_UNROLL = 1


def _attn_b_kernel(far_ref, lam_ref, q_ref, k_ref, v_ref, bias_ref, gcol_ref, o_ref,
                   qs_ref, s_ref, p_ref, bm_ref, m_ref, l_ref, alpha_ref, acc_ref):
    h = pl.program_id(1)
    i = pl.program_id(2)
    t = q_ref.shape[2]
    nk = k_ref.shape[2] // t
    win = min(B_WIN, nk)
    nfar = nk - win
    q = q_ref[0]
    z = jnp.zeros((B_QK, t), q.dtype)
    qs_ref[0:HEAD_DIM] = jnp.concatenate(
        [jnp.concatenate([q[0:B_QK], z], axis=0), jnp.concatenate([z, q[B_QK:HEAD_DIM]], axis=0)], axis=1)
    qs_ref[HEAD_DIM:K_LANES] = jnp.zeros((K_LANES - HEAD_DIM, 2 * t), qs_ref.dtype)
    _pipe_init(m_ref, l_ref, alpha_ref, acc_ref, p_ref)
    c_neg = far_ref[h, 0]
    c_pos = far_ref[h, 1]
    w0 = jnp.clip(i - 1, 0, nk - win)

    def kchunk(j):
        return k_ref[0, 0, pl.ds(pl.multiple_of(j * t, t), t), :]

    def vchunk(j):
        return v_ref[0, :, pl.ds(pl.multiple_of(j * t, t), t)]

    def far_chunk(f):
        return jnp.where(f < w0, f, f + win)

    def qk_win(j):
        b = bias_ref[0, j - i + B_TILES // 2]
        _pipe_qk(kchunk(j), qs_ref, s_ref, bm_ref, jnp.concatenate([b, b], axis=1))

    def sm(c):
        _pipe_sm(c, s_ref, bm_ref, m_ref, l_ref, alpha_ref, p_ref)

    def pv(j):
        _pipe_pv(vchunk(j), p_ref, alpha_ref, acc_ref)

    qk_win(w0)
    for w in range(win):
        if w > 0:
            pv(w0 + w - 1)
        sm(0.0)
        if w + 1 < win:
            qk_win(w0 + w + 1)
        elif nfar > 0:
            _pipe_qk(kchunk(far_chunk(0)), qs_ref, s_ref, bm_ref)

    def far_step(f):
        pv(jnp.where(f == 0, w0 + win - 1, far_chunk(f - 1)))
        sm(jnp.where(f < w0, c_neg, c_pos))
        _pipe_qk(kchunk(far_chunk(f + 1)), qs_ref, s_ref, bm_ref)

    if nfar > 0:
        n_full = nfar - 1

        def group(fg, carry):
            for u in range(B_UNROLL):
                far_step(B_UNROLL * fg + u)
            return carry

        lax.fori_loop(0, n_full // B_UNROLL, group, 0)
        for f in range(n_full - n_full % B_UNROLL, n_full):
            far_step(f)
        pv(far_chunk(nfar - 2) if nfar > 1 else w0 + win - 1)
        sm(jnp.where(nfar - 1 < w0, c_neg, c_pos))
        pv(far_chunk(nfar - 1))
    else:
        pv(w0 + win - 1)

    o = acc_ref[...] / l_ref[...]
    o = o[:, 0:t] - lam_ref[0] * o[:, t:2 * t]
    ms = jnp.mean(o * o, axis=0, keepdims=True)
    o_ref[0] = (o * lax.rsqrt(ms + EPS) * gcol_ref[...]).astype(o_ref.dtype)


def _attn_b(qt, k, vt, bias_tiles, far, lam, gcol):
    bsz, _, s_len = qt.shape
    t = bias_tiles.shape[-1]
    return pl.pallas_call(
        _attn_b_kernel,
        grid_spec=pltpu.PrefetchScalarGridSpec(
            num_scalar_prefetch=2,
            grid=(bsz, B_HEADS, s_len // t),
            in_specs=[
                pl.BlockSpec((1, HEAD_DIM, t), lambda b, h, i, *_: (b, h, i)),
                pl.BlockSpec((1, 1, s_len, K_LANES), lambda b, h, i, *_: (b, h, 0, 0)),
                pl.BlockSpec((1, HEAD_DIM, s_len), lambda b, h, i, *_: (b, h, 0)),
                pl.BlockSpec((1, B_TILES, t, t), lambda b, h, i, *_: (h, 0, 0, 0)),
                pl.BlockSpec((HEAD_DIM, 1), lambda b, h, i, *_: (0, 0)),
            ],
            out_specs=pl.BlockSpec((1, HEAD_DIM, t), lambda b, h, i, *_: (b, h, i)),
            scratch_shapes=_pipe_scratch(t, 2 * t),
        ),
        out_shape=jax.ShapeDtypeStruct((bsz, B_Q, s_len), BF16),
        compiler_params=_cparams(("arbitrary", "arbitrary", "arbitrary")),
        name="attn_b",
    )(far, lam, qt, k, vt, bias_tiles, gcol)


SUM_ROWS = 16
SHIFT_LIMIT = 100.0
FAST_C_UNROLL = 5
FAST_B_UNROLL = 14


def _fast_qk_exp(kc, qaug_ref, p_ref, tile=None):
    s = _dot(kc, qaug_ref[...])
    if tile is not None:
        s = s + tile
    p_ref[...] = jnp.exp2(s).astype(p_ref.dtype)


def _fast_pv(vc, p_ref, acc_ref, cls):
    va = jnp.concatenate([vc, jnp.ones((SUM_ROWS, vc.shape[1]), vc.dtype)], axis=0)
    acc_ref[cls] = acc_ref[cls] + _dot(va, p_ref[...])


def _fast_scratch(tk, n, nacc):
    return [
        pltpu.VMEM((K_LANES, n), BF16),
        pltpu.VMEM((tk, n), BF16),
        pltpu.VMEM((nacc, HEAD_DIM + SUM_ROWS, n), F32),
    ]


def _shift_rows(shift, n):
    row = lax.broadcasted_iota(jnp.int32, (K_LANES - HEAD_DIM, n), 0)
    return jnp.where(row == 0, -shift, 0.0).astype(BF16)


def _attn_c_fast_kernel(shift_ref, q_ref, k_ref, v_ref, o_ref, qaug_ref, p_ref, acc_ref, *, tk):
    g = C_HEADS // C_KV
    tq = q_ref.shape[2]
    nk = k_ref.shape[2] // tk
    q3 = q_ref[0]
    qaug_ref[0:HEAD_DIM] = jnp.concatenate([q3[h * HEAD_DIM:(h + 1) * HEAD_DIM] for h in range(g)], axis=1)
    qaug_ref[HEAD_DIM:K_LANES] = _shift_rows(shift_ref[0], g * tq)
    acc_ref[...] = jnp.zeros(acc_ref.shape, F32)

    def kchunk(j):
        return k_ref[0, 0, pl.ds(pl.multiple_of(j * tk, tk), tk), :]

    def vchunk(j):
        return v_ref[0, :, pl.ds(pl.multiple_of(j * tk, tk), tk)]

    _fast_qk_exp(kchunk(0), qaug_ref, p_ref)

    def step(j):
        _fast_pv(vchunk(j - 1), p_ref, acc_ref, 0)
        _fast_qk_exp(kchunk(j), qaug_ref, p_ref)

    def group(jg, carry):
        for u in range(FAST_C_UNROLL):
            step(1 + FAST_C_UNROLL * jg + u)
        return carry

    n_full = nk - 1
    lax.fori_loop(0, n_full // FAST_C_UNROLL, group, 0)
    for j in range(1 + n_full - n_full % FAST_C_UNROLL, nk):
        step(j)
    _fast_pv(vchunk(nk - 1), p_ref, acc_ref, 0)
    acc = acc_ref[0]
    o = acc[0:HEAD_DIM] / acc[HEAD_DIM:HEAD_DIM + 1]
    o_ref[0] = jnp.concatenate([o[:, h * tq:(h + 1) * tq] for h in range(g)], axis=0).astype(o_ref.dtype)


def _attn_c_fast(qt, k, vt, shift):
    bsz, _, s_len = qt.shape
    g = C_HEADS // C_KV
    tq = min(TQ_C, s_len)
    tk = min(TK_C, s_len)
    return pl.pallas_call(
        functools.partial(_attn_c_fast_kernel, tk=tk),
        grid_spec=pltpu.PrefetchScalarGridSpec(
            num_scalar_prefetch=1,
            grid=(bsz, C_KV, s_len // tq),
            in_specs=[
                pl.BlockSpec((1, g * HEAD_DIM, tq), lambda b, kv, i, *_: (b, kv, i)),
                pl.BlockSpec((1, 1, s_len, K_LANES), lambda b, kv, i, *_: (b, kv, 0, 0)),
                pl.BlockSpec((1, HEAD_DIM, s_len), lambda b, kv, i, *_: (b, kv, 0)),
            ],
            out_specs=pl.BlockSpec((1, g * HEAD_DIM, tq), lambda b, kv, i, *_: (b, kv, i)),
            scratch_shapes=_fast_scratch(tk, g * tq, 1),
        ),
        out_shape=jax.ShapeDtypeStruct((bsz, C_Q, s_len), BF16),
        compiler_params=_cparams(("arbitrary", "arbitrary", "arbitrary")),
        name="attn_c_fast",
    )(shift, qt, k, vt)


def _attn_b_fast_kernel(far_ref, lam_ref, shift_ref, q_ref, k_ref, v_ref, bias_ref, gcol_ref, o_ref,
                        qaug_ref, p_ref, acc_ref):
    h = pl.program_id(1)
    i = pl.program_id(2)
    t = q_ref.shape[2]
    nk = k_ref.shape[2] // t
    win = min(B_WIN, nk)
    nfar = nk - win
    q = q_ref[0]
    z = jnp.zeros((B_QK, t), q.dtype)
    qaug_ref[0:HEAD_DIM] = jnp.concatenate(
        [jnp.concatenate([q[0:B_QK], z], axis=0), jnp.concatenate([z, q[B_QK:HEAD_DIM]], axis=0)], axis=1)
    qaug_ref[HEAD_DIM:K_LANES] = _shift_rows(shift_ref[h], 2 * t)
    acc_ref[...] = jnp.zeros(acc_ref.shape, F32)
    w0 = jnp.clip(i - 1, 0, nk - win)

    def kchunk(j):
        return k_ref[0, 0, pl.ds(pl.multiple_of(j * t, t), t), :]

    def vchunk(j):
        return v_ref[0, :, pl.ds(pl.multiple_of(j * t, t), t)]

    def far_chunk(f):
        return jnp.where(f < w0, f, f + win)

    def far_cls(f):
        return jnp.where(f < w0, 1, 2)

    def qk_win(j):
        b = bias_ref[0, j - i + B_TILES // 2]
        _fast_qk_exp(kchunk(j), qaug_ref, p_ref, jnp.concatenate([b, b], axis=1))

    qk_win(w0)
    for w in range(1, win):
        _fast_pv(vchunk(w0 + w - 1), p_ref, acc_ref, 0)
        qk_win(w0 + w)
    _fast_pv(vchunk(w0 + win - 1), p_ref, acc_ref, 0)

    if nfar > 0:
        _fast_qk_exp(kchunk(far_chunk(0)), qaug_ref, p_ref)

        def far_step(f):
            _fast_pv(vchunk(far_chunk(f - 1)), p_ref, acc_ref, far_cls(f - 1))
            _fast_qk_exp(kchunk(far_chunk(f)), qaug_ref, p_ref)

        n_full = nfar - 1

        def group(fg, carry):
            for u in range(FAST_B_UNROLL):
                far_step(1 + FAST_B_UNROLL * fg + u)
            return carry

        lax.fori_loop(0, n_full // FAST_B_UNROLL, group, 0)
        for f in range(1 + n_full - n_full % FAST_B_UNROLL, nfar):
            far_step(f)
        _fast_pv(vchunk(far_chunk(nfar - 1)), p_ref, acc_ref, far_cls(nfar - 1))

    wneg = jnp.exp2(jnp.full((1, 1), far_ref[h, 0], F32))
    wpos = jnp.exp2(jnp.full((1, 1), far_ref[h, 1], F32))
    acc = acc_ref[0] + wneg * acc_ref[1] + wpos * acc_ref[2]
    o = acc[0:HEAD_DIM] / acc[HEAD_DIM:HEAD_DIM + 1]
    o = o[:, 0:t] - lam_ref[0] * o[:, t:2 * t]
    ms = jnp.mean(o * o, axis=0, keepdims=True)
    o_ref[0] = (o * lax.rsqrt(ms + EPS) * gcol_ref[...]).astype(o_ref.dtype)


def _attn_b_fast(qt, k, vt, bias_tiles, far, lam, gcol, shift):
    bsz, _, s_len = qt.shape
    t = bias_tiles.shape[-1]
    return pl.pallas_call(
        _attn_b_fast_kernel,
        grid_spec=pltpu.PrefetchScalarGridSpec(
            num_scalar_prefetch=3,
            grid=(bsz, B_HEADS, s_len // t),
            in_specs=[
                pl.BlockSpec((1, HEAD_DIM, t), lambda b, h, i, *_: (b, h, i)),
                pl.BlockSpec((1, 1, s_len, K_LANES), lambda b, h, i, *_: (b, h, 0, 0)),
                pl.BlockSpec((1, HEAD_DIM, s_len), lambda b, h, i, *_: (b, h, 0)),
                pl.BlockSpec((1, B_TILES, t, t), lambda b, h, i, *_: (h, 0, 0, 0)),
                pl.BlockSpec((HEAD_DIM, 1), lambda b, h, i, *_: (0, 0)),
            ],
            out_specs=pl.BlockSpec((1, HEAD_DIM, t), lambda b, h, i, *_: (b, h, i)),
            scratch_shapes=_fast_scratch(t, 2 * t, 3),
        ),
        out_shape=jax.ShapeDtypeStruct((bsz, B_Q, s_len), BF16),
        compiler_params=_cparams(("arbitrary", "arbitrary", "arbitrary")),
        name="attn_b_fast",
    )(far, lam, shift, qt, k, vt, bias_tiles, gcol)


def _attn_a_kernel(sink_ref, q_ref, kp_ref, kc_ref, kn_ref, vp_ref, vc_ref, vn_ref, bias_ref, o_ref):
    n = pl.program_id(1)
    nb = pl.num_programs(1)
    g = A_HEADS // A_KV
    blk = A_BLK
    row = lax.broadcasted_iota(jnp.int32, (3 * blk, 1), 0)
    valid = ((row >= blk) | (n > 0)) & ((row < 2 * blk) | (n < nb - 1))
    gw = g * blk
    zero_q = jnp.zeros((K_LANES, gw), q_ref.dtype)
    qrows = []
    for kv in range(A_KV):
        q3 = q_ref[0, kv * g * HEAD_DIM:(kv + 1) * g * HEAD_DIM]
        qcat = jnp.concatenate([q3[h * HEAD_DIM:(h + 1) * HEAD_DIM] for h in range(g)], axis=1)
        qaug = jnp.concatenate([qcat, jnp.zeros((K_LANES - HEAD_DIM, gw), qcat.dtype)], axis=0)
        qrows.append(jnp.concatenate([qaug if c == kv else zero_q for c in range(A_KV)], axis=1))
    qbd = jnp.concatenate(qrows, axis=0)
    kwin = jnp.concatenate(
        [jnp.concatenate([r[0, kv] for r in (kp_ref, kc_ref, kn_ref)], axis=0) for kv in range(A_KV)], axis=1)
    vwin = jnp.concatenate([vp_ref[0], vc_ref[0], vn_ref[0]], axis=1)
    bias = jnp.concatenate([bias_ref[h] for h in range(A_HEADS)], axis=1)
    s = jnp.where(valid, _dot(kwin, qbd) + bias, NEG_BIG)
    sink = jnp.concatenate([jnp.full((1, blk), sink_ref[h], F32) for h in range(A_HEADS)], axis=1)
    m = jnp.maximum(jnp.max(s, axis=0, keepdims=True), sink)
    p = jnp.exp2(s - m)
    denom = jnp.sum(p, axis=0, keepdims=True) + jnp.exp2(sink - m)
    o_all = _dot(vwin, p.astype(BF16)) / denom
    for kv in range(A_KV):
        o = o_all[kv * HEAD_DIM:(kv + 1) * HEAD_DIM, kv * gw:(kv + 1) * gw]
        o_ref[0, kv * g * HEAD_DIM:(kv + 1) * g * HEAD_DIM] = jnp.concatenate(
            [o[:, h * blk:(h + 1) * blk] for h in range(g)], axis=0).astype(o_ref.dtype)


def _attn_a(qt, k, vt, bias, sink):
    bsz, _, s_len = qt.shape
    nb = s_len // A_BLK

    def kspec(off):
        return pl.BlockSpec((1, A_KV, A_BLK, K_LANES),
                            lambda b, n, *_: (b, 0, jnp.clip(n + off, 0, nb - 1), 0))

    def vspec(off):
        return pl.BlockSpec((1, A_V, A_BLK), lambda b, n, *_: (b, 0, jnp.clip(n + off, 0, nb - 1)))

    return pl.pallas_call(
        _attn_a_kernel,
        grid_spec=pltpu.PrefetchScalarGridSpec(
            num_scalar_prefetch=1,
            grid=(bsz, nb),
            in_specs=[
                pl.BlockSpec((1, A_Q, A_BLK), lambda b, n, *_: (b, 0, n)),
                kspec(-1), kspec(0), kspec(1), vspec(-1), vspec(0), vspec(1),
                pl.BlockSpec((A_HEADS, 3 * A_BLK, A_BLK), lambda b, n, *_: (0, 0, 0)),
            ],
            out_specs=pl.BlockSpec((1, A_Q, A_BLK), lambda b, n, *_: (b, 0, n)),
        ),
        out_shape=jax.ShapeDtypeStruct((bsz, A_Q, s_len), BF16),
        compiler_params=_cparams(("arbitrary", "arbitrary")),
        name="attn_a",
    )(sink, qt, k, k, k, vt, vt, vt, bias)


def _outproj_kernel(oa_ref, ob_ref, oc_ref, wa_ref, wb_ref, wc_ref, x_ref, mod_ref, g_ref,
                    wrh_ref, wrl_ref, x1_ref, h2_ref, aff_ref):
    def tn(a, b):
        return lax.dot_general(a, b, (((0,), (0,)), ((), ())), preferred_element_type=F32)

    out = tn(oa_ref[0], wa_ref[...]) + tn(ob_ref[0], wb_ref[...]) + tn(oc_ref[0], wc_ref[...])
    mod = mod_ref[0]
    d = out.shape[1]
    gt_a = mod[:, 2 * d:3 * d]
    sh_f, sc_f = mod[:, 3 * d:4 * d], mod[:, 4 * d:5 * d]
    x1 = x_ref[...] + gt_a * out
    x1_ref[...] = x1
    ms = jnp.mean(x1 * x1, axis=-1, keepdims=True)
    h2 = x1 * lax.rsqrt(ms + EPS) * g_ref[...]
    h2 = h2 * (1.0 + sc_f) + sh_f
    for c in range(d // LANES):
        h2_ref[:, c, :] = h2[:, c * LANES:(c + 1) * LANES]
    hi = h2.astype(BF16)
    lo = (h2 - hi.astype(F32)).astype(BF16)
    lt = _dot_nt(wrh_ref[...], hi) + _dot_nt(wrh_ref[...], lo) + _dot_nt(wrl_ref[...], hi)
    lt = lt - jnp.max(lt, axis=0, keepdims=True)
    e = jnp.exp(lt)
    aff_ref[...] = e / jnp.sum(e, axis=0, keepdims=True)


def _outproj(oa, ob, oc, wa, wb, wc, x2d, mod3, g_ffn, wr_hi, wr_lo, s_len):
    n, d = x2d.shape
    tm = min(TM_PROJ, s_len)
    spb = s_len // tm

    def ot_spec(rows_):
        return pl.BlockSpec((1, rows_, tm), lambda i: (i // spb, 0, i % spb))

    def full(shape):
        return pl.BlockSpec(shape, lambda i: tuple(0 for _ in shape))

    return pl.pallas_call(
        _outproj_kernel,
        grid=(n // tm,),
        in_specs=[
            ot_spec(A_Q), ot_spec(B_Q), ot_spec(C_Q),
            full((A_Q, d)), full((B_Q, d)), full((C_Q, d)),
            pl.BlockSpec((tm, d), lambda i: (i, 0)),
            pl.BlockSpec((1, 1, N_MOD * d), lambda i: (i // spb, 0, 0)),
            full((1, d)), full((N_EXPERTS, d)), full((N_EXPERTS, d)),
        ],
        out_specs=[
            pl.BlockSpec((tm, d), lambda i: (i, 0)),
            pl.BlockSpec((tm, d // LANES, LANES), lambda i: (i, 0, 0)),
            pl.BlockSpec((N_EXPERTS, tm), lambda i: (0, i)),
        ],
        out_shape=[
            jax.ShapeDtypeStruct((n, d), F32),
            jax.ShapeDtypeStruct((n, d // LANES, LANES), F32),
            jax.ShapeDtypeStruct((N_EXPERTS, n), F32),
        ],
        compiler_params=_cparams(("arbitrary",)),
        name="outproj",
    )(oa, ob, oc, wa, wb, wc, x2d, mod3, g_ffn, wr_hi, wr_lo)


def _routing_kernel(aff_ref, idx_ref, gate_ref, pexc_ref, thr_ref, need_ref, *, cap, pc):
    ne, nblk, _ = aff_ref.shape
    bits = lax.bitcast_convert_type(aff_ref[...], jnp.int32)

    def bis(it, t):
        cand = t | jnp.left_shift(jnp.int32(1), 30 - it)
        cnt = jnp.sum((bits >= cand).astype(F32), axis=(1, 2), keepdims=True)
        return jnp.where(cnt >= cap, cand, t)

    thr = lax.fori_loop(0, 31, bis, jnp.zeros((ne, 1, 1), jnp.int32))
    n_gt = jnp.sum((bits > thr).astype(F32), axis=(1, 2), keepdims=True)
    thr_ref[...] = jnp.broadcast_to(thr, thr_ref.shape)
    need_ref[...] = jnp.broadcast_to(cap - n_gt, need_ref.shape)

    def tri(nn, fn):
        a = lax.broadcasted_iota(jnp.int32, (nn, nn), 0)
        b = lax.broadcasted_iota(jnp.int32, (nn, nn), 1)
        return jnp.where(fn(a, b), 1.0, 0.0).astype(BF16)

    ones_l = jnp.ones((LANES, LANES), BF16)
    su_l = tri(LANES, lambda a, b: a < b)
    li_l = tri(LANES, lambda a, b: b <= a)
    sl_b = tri(nblk, lambda a, b: b < a)
    li_b = tri(nblk, lambda a, b: b <= a)
    blk_col = lax.broadcasted_iota(jnp.int32, (nblk, 1), 0).astype(F32)
    lane_col = lax.broadcasted_iota(jnp.int32, (LANES, 1), 0).astype(F32)

    def per_expert(e, carry):
        a = aff_ref[e]
        be = lax.bitcast_convert_type(a, jnp.int32)
        te = thr_ref[e]
        gt = be > te
        eq = (be == te)
        eqb = jnp.where(eq, 1.0, 0.0).astype(BF16)
        rank = _dot(eqb, su_l) + _dot(sl_b, _dot(eqb, ones_l).astype(BF16))
        sel = gt | (eq & (rank < need_ref[e]))
        mb = jnp.where(sel, 1.0, 0.0).astype(BF16)
        cnt_b = _dot(mb, ones_l)
        pinc = _dot(li_b, cnt_b.astype(BF16))
        pinc_col = pinc[:, 0:1]
        pexc_col = pinc_col - cnt_b[:, 0:1]
        pexc_ref[e] = (pinc - cnt_b).T[0:1].astype(jnp.int32)
        mt = jnp.where(sel, 1.0, 0.0).T.astype(BF16)
        at = a.T
        at_hi = at.astype(BF16)
        at_lo = (at - at_hi.astype(F32)).astype(BF16)
        for c in range(cap // pc):
            p_row = (lax.broadcasted_iota(jnp.int32, (1, pc), 1) + c * pc).astype(F32)
            blk_row = jnp.sum(jnp.where(pinc_col <= p_row, 1.0, 0.0), axis=0, keepdims=True)
            oh = blk_col == blk_row
            ohb = jnp.where(oh, 1.0, 0.0).astype(BF16)
            r_row = p_row - jnp.sum(jnp.where(oh, pexc_col, 0.0), axis=0, keepdims=True)
            gt_rows = _dot(mt, ohb)
            pc_incl = _dot(li_l, gt_rows.astype(BF16))
            lane_row = jnp.sum(jnp.where(pc_incl <= r_row, 1.0, 0.0), axis=0, keepdims=True)
            idx_ref[e, :, c * pc:(c + 1) * pc] = (blk_row * LANES + lane_row).astype(jnp.int32)
            ag = _dot(at_hi, ohb) + _dot(at_lo, ohb)
            gate_ref[e, :, c * pc:(c + 1) * pc] = jnp.sum(
                jnp.where(lane_col == lane_row, ag, 0.0), axis=0, keepdims=True)
        return carry

    lax.fori_loop(0, ne, per_expert, 0)


def _routing(aff3, cap):
    ne, nblk, _ = aff3.shape
    pc = min(ROUTE_PC, cap)
    return pl.pallas_call(
        functools.partial(_routing_kernel, cap=cap, pc=pc),
        grid=(1,),
        in_specs=[pl.BlockSpec((ne, nblk, LANES), lambda i: (0, 0, 0))],
        out_specs=[
            pl.BlockSpec((ne, 1, cap), lambda i: (0, 0, 0)),
            pl.BlockSpec((ne, 1, cap), lambda i: (0, 0, 0)),
            pl.BlockSpec((ne, 1, nblk), lambda i: (0, 0, 0)),
        ],
        out_shape=[
            jax.ShapeDtypeStruct((ne, 1, cap), jnp.int32),
            jax.ShapeDtypeStruct((ne, 1, cap), F32),
            jax.ShapeDtypeStruct((ne, 1, nblk), jnp.int32),
        ],
        scratch_shapes=[pltpu.VMEM((ne, 1, LANES), jnp.int32), pltpu.VMEM((ne, 1, LANES), F32)],
        compiler_params=_cparams(("arbitrary",)),
        name="routing",
    )(aff3)


TOK_LANES = LANES
TOK_RADIX = 256
IDX_SLOTS = 3


def _moe_kernel(idx_ref, idxn_ref, idxnn_ref, gate_ref, h_hbm, wg_ref, wu_ref, wd_ref, y_ref,
                idx_smem, xbuf, sems):
    ch = xbuf.shape[1]
    d = xbuf.shape[2] * xbuf.shape[3]
    nsteps = pl.num_programs(0) * pl.num_programs(1)
    s = pl.program_id(0) * pl.num_programs(1) + pl.program_id(1)
    slot = s % 2
    nslot = 1 - slot
    k1 = (s + 1) % IDX_SLOTS
    k2 = (s + 2) % IDX_SLOTS

    def idx_copy(src_ref, sl):
        return pltpu.make_async_copy(src_ref.at[0, 0], idx_smem.at[sl], sems.at[2 + sl])

    def row_copy(sl, r, t):
        return pltpu.make_async_copy(h_hbm.at[pl.ds(t, 1)], xbuf.at[sl, pl.ds(r, 1)], sems.at[sl])

    @pl.when(s == 0)
    def _():
        c0 = idx_copy(idx_ref, 0)
        c0.start()
        c0.wait()
        for r in range(ch):
            row_copy(0, r, idx_smem[0, r]).start()
        c1 = idx_copy(idxn_ref, 1)
        c1.start()
        c1.wait()

    idx_copy(idxnn_ref, k2).start()
    for r in range(ch):
        row_copy(slot, r, 0).wait()
    for r in range(ch):
        row_copy(nslot, r, idx_smem[k1, r]).start(priority=r % 2)

    x = jnp.concatenate([xbuf[slot, :, c, :] for c in range(xbuf.shape[2])], axis=1).astype(BF16)
    gg = _dot(x, wg_ref[0])
    uu = _dot(x, wu_ref[0])
    hid = (gg * (1.0 / (1.0 + jnp.exp(-gg))) * uu).astype(BF16)
    y = _dot(hid, wd_ref[0])
    sub = lax.broadcasted_iota(jnp.int32, (LANES, ch), 0)
    cols = jnp.where(sub == 0, gate_ref[0], jnp.where(sub == 1, idx_ref[0].astype(F32), 0.0)).T
    y_ref[:, 0:d] = (y * cols[:, 0:1]).astype(y_ref.dtype)
    tok = cols[:, 1:2]
    hi = jnp.floor(tok * (1.0 / TOK_RADIX))
    lane = lax.broadcasted_iota(jnp.int32, (ch, TOK_LANES), 1)
    y_ref[:, d:d + TOK_LANES] = jnp.where(
        lane == 0, hi, jnp.where(lane == 1, tok - hi * TOK_RADIX, 0.0)).astype(y_ref.dtype)

    idx_copy(idxnn_ref, k2).wait()

    @pl.when(s == nsteps - 1)
    def _():
        for r in range(ch):
            row_copy(nslot, r, 0).wait()


def _moe(idx3, gate3, h2, wg, wu, wd, cap):
    n, dt, _ = h2.shape
    d = dt * LANES
    ne, _, ff = wg.shape
    ch = min(MOE_CH, cap)
    nch = cap // ch
    nsteps = ne * nch
    return pl.pallas_call(
        _moe_kernel,
        grid=(ne, nch),
        in_specs=[
            pl.BlockSpec((1, 1, ch), lambda e, c: (e * nch + c, 0, 0)),
            pl.BlockSpec((1, 1, ch), lambda e, c: (jnp.minimum(e * nch + c + 1, nsteps - 1), 0, 0)),
            pl.BlockSpec((1, 1, ch), lambda e, c: (jnp.minimum(e * nch + c + 2, nsteps - 1), 0, 0)),
            pl.BlockSpec((1, 1, ch), lambda e, c: (e * nch + c, 0, 0)),
            pl.BlockSpec(memory_space=pl.ANY),
            pl.BlockSpec((1, d, ff), lambda e, c: (e, 0, 0)),
            pl.BlockSpec((1, d, ff), lambda e, c: (e, 0, 0)),
            pl.BlockSpec((1, ff, d), lambda e, c: (e, 0, 0)),
        ],
        out_specs=pl.BlockSpec((ch, d + TOK_LANES), lambda e, c: (e * nch + c, 0)),
        out_shape=jax.ShapeDtypeStruct((ne * cap, d + TOK_LANES), BF16),
        scratch_shapes=[
            pltpu.SMEM((IDX_SLOTS, ch), jnp.int32),
            pltpu.VMEM((2, ch, dt, LANES), F32),
            pltpu.SemaphoreType.DMA((2 + IDX_SLOTS,)),
        ],
        compiler_params=_cparams(("arbitrary", "arbitrary")),
        name="moe_ffn",
    )(idx3.reshape(nsteps, 1, ch), idx3.reshape(nsteps, 1, ch), idx3.reshape(nsteps, 1, ch),
      gate3.reshape(nsteps, 1, ch), h2, wg, wu, wd)


def _combine_kernel(starts_ref, y_hbm, x_ref, mod_ref, o_ref, ybuf, xtra, sems, *, cap, ne):
    tt, d = x_ref.shape
    wr = ybuf.shape[2]
    nt = pl.num_programs(0)
    t = pl.program_id(0)
    slot = t % 2
    nslot = 1 - slot

    def bounds(e, tile):
        lo = starts_ref[e * (nt + 1) + tile]
        hi = starts_ref[e * (nt + 1) + tile + 1]
        base = jnp.minimum((lo // 16) * 16, cap - wr)
        return lo, hi, base

    def win_copy(e, base, sl):
        return pltpu.make_async_copy(y_hbm.at[pl.ds(pl.multiple_of(e * cap + base, 16), wr)],
                                     ybuf.at[sl, e], sems.at[sl])

    def fetch(tile, sl):
        for e in range(ne):
            win_copy(e, bounds(e, tile)[2], sl).start()

    @pl.when(t == 0)
    def _():
        fetch(0, 0)

    for e in range(ne):
        win_copy(e, 0, slot).wait()
    fetch(jnp.minimum(t + 1, nt - 1), nslot)

    tok0 = (t * tt).astype(F32)
    lane_tok = lax.broadcasted_iota(jnp.int32, (1, tt), 1).astype(F32) + tok0
    row = lax.broadcasted_iota(jnp.int32, (wr, 1), 0)

    def one_hot(yb, first, lo_eff, hi):
        tokc = yb[:, d:d + 1].astype(F32) * TOK_RADIX + yb[:, d + 1:d + 2].astype(F32)
        pos = row + first
        return jnp.where((pos >= lo_eff) & (pos < hi) & (tokc == lane_tok), 1.0, 0.0).astype(BF16)

    def scatter_rows(oh, rows):
        return lax.dot_general(oh, rows, (((0,), (0,)), ((), ())), preferred_element_type=F32)

    ohs = []
    for e in range(ne):
        lo, hi, base = bounds(e, t)
        ohs.append(one_hot(ybuf[slot, e], base, lo, hi))
    rows_all = ybuf[slot].reshape(ne * wr, d + TOK_LANES)[:, 0:d]
    acc = scatter_rows(jnp.concatenate(ohs, axis=0), rows_all)
    o_ref[...] = x_ref[...] + mod_ref[0][:, 5 * d:6 * d] * acc

    for e in range(ne):
        lo, hi, base = bounds(e, t)
        n_more = jnp.maximum(hi - (base + wr) + wr - 1, 0) // wr

        def more(k, carry, e=e, hi=hi, base=base):
            done = base + wr * (k + 1)
            first = jnp.minimum(done, cap - wr)
            cp = pltpu.make_async_copy(y_hbm.at[pl.ds(pl.multiple_of(e * cap + first, 16), wr)], xtra, sems.at[2])
            cp.start()
            cp.wait()
            yb = xtra[...]
            o_ref[...] = o_ref[...] + mod_ref[0][:, 5 * d:6 * d] * scatter_rows(
                one_hot(yb, first, done, hi), yb[:, 0:d])
            return carry

        lax.fori_loop(0, n_more, more, 0)

    @pl.when(t == nt - 1)
    def _():
        for e in range(ne):
            win_copy(e, 0, nslot).wait()


def _combine(starts, y, x1, mod3, s_len, cap):
    n, d = x1.shape
    ne = y.shape[0] // cap
    tt = min(COMBINE_TT, s_len)
    spb = s_len // tt
    wr = min(COMBINE_WR, cap)
    return pl.pallas_call(
        functools.partial(_combine_kernel, cap=cap, ne=ne),
        grid_spec=pltpu.PrefetchScalarGridSpec(
            num_scalar_prefetch=1,
            grid=(n // tt,),
            in_specs=[
                pl.BlockSpec(memory_space=pl.ANY),
                pl.BlockSpec((tt, d), lambda i, *_: (i, 0)),
                pl.BlockSpec((1, 1, N_MOD * d), lambda i, *_: (i // spb, 0, 0)),
            ],
            out_specs=pl.BlockSpec((tt, d), lambda i, *_: (i, 0)),
            scratch_shapes=[
                pltpu.VMEM((2, ne, wr, d + TOK_LANES), BF16),
                pltpu.VMEM((wr, d + TOK_LANES), BF16),
                pltpu.SemaphoreType.DMA((3,)),
            ],
        ),
        out_shape=jax.ShapeDtypeStruct((n, d), F32),
        compiler_params=_cparams(("arbitrary",)),
        name="combine",
    )(starts, y, x1, mod3)


def _t5_bucket(rel):
    half = NUM_BUCKETS // 2
    max_exact = half // 2
    n = jnp.abs(rel)
    nf = jnp.maximum(n, 1).astype(F32)
    large = max_exact + (jnp.log(nf / max_exact) / math.log(MAX_DISTANCE / max_exact)
                         * (half - max_exact)).astype(jnp.int32)
    large = jnp.minimum(large, half - 1)
    return jnp.where(rel > 0, half, 0) + jnp.where(n < max_exact, n, large)


def _rope_table(s_len):
    quarter = HEAD_DIM // 4
    freqs = ROPE_THETA ** (-jnp.arange(quarter, dtype=F32) / quarter)
    t = jnp.arange(s_len)
    ang_r = freqs[:, None] * (t // GRID_W).astype(F32)[None, :]
    ang_c = freqs[:, None] * (t % GRID_W).astype(F32)[None, :]
    return jnp.concatenate([jnp.cos(ang_r), jnp.sin(ang_r), jnp.cos(ang_c), jnp.sin(ang_c)], axis=0)


def _toeplitz(fn, nj, ni):
    period = nj + ni
    m = jnp.arange(period)
    b = fn(jnp.where(m < nj, m, m - period))
    flat = jnp.tile(b, ni)[..., :ni * (period - 1)]
    rows = flat.reshape(b.shape[:-1] + (ni, period - 1))
    return jnp.swapaxes(rows[..., :nj], -1, -2)


def _bias_tables(rel_bias, t_b):
    def bias_of(rel, heads):
        return rel_bias[_t5_bucket(rel)][:, heads].T * LOG2E

    def fn_a(r):
        rel = r - A_BLK
        return jnp.where(jnp.abs(rel) <= WINDOW, bias_of(rel, slice(0, A_HEADS)), NEG_BIG)

    bias_a = _toeplitz(fn_a, 3 * A_BLK, A_BLK).astype(F32)
    tiles = [
        _toeplitz(lambda r, dd=dd: bias_of(dd * t_b + r, slice(A_HEADS, None)), t_b, t_b)
        for dd in range(-(B_TILES // 2), B_TILES // 2 + 1)
    ]
    bias_b = jnp.stack(tiles, axis=1).astype(F32)
    half = NUM_BUCKETS // 2
    far = jnp.stack([rel_bias[half - 1, A_HEADS:], rel_bias[NUM_BUCKETS - 1, A_HEADS:]], axis=1) * LOG2E
    return bias_a, bias_b, far.astype(F32)


def _gain_column(ga, gb, gc):
    sa = HEAD_DIM ** -0.5 * LOG2E
    sb = B_QK ** -0.5 * LOG2E
    parts = [
        jnp.tile(ga[0] * sa, A_HEADS), jnp.tile(ga[1], A_KV), jnp.ones((A_V,), F32),
        jnp.tile(gb[0] * sb, 2 * B_HEADS), jnp.tile(gb[1], 2 * B_HEADS), jnp.ones((B_V,), F32),
        jnp.tile(gc[0] * sa, C_HEADS), jnp.tile(gc[1], C_KV), jnp.ones((C_V,), F32),
    ]
    return jnp.concatenate(parts).reshape(IN_COLS, 1).astype(F32)


def _trunk(x, mod_l, prep, s_len, bsz):
    n = bsz * s_len
    d = x.shape[-1]
    x2d = x.reshape(n, d)
    t_b = min(T_B, s_len)
    cap = CAPACITY_FACTOR * n // N_EXPERTS
    cs = _rope_table(s_len)
    for l, lp in enumerate(prep["layers"]):
        mod3 = mod_l[l][:, None, :]
        qa, ka, va, qb, kb, vb, qc, kc, vc = _inproj(
            x2d, mod3, lp["g_attn"], lp["w_in_t"], lp["gcol"], cs, bsz, s_len)
        oa = _attn_a(qa, ka, va, prep["bias_a"], lp["sink"])
        bias_b = prep["bias_b"][t_b]
        ob = lax.cond(
            lp["b_bounded"],
            lambda q, k, v: _attn_b_fast(q, k, v, bias_b, prep["far_b"], lp["lam"], lp["subln_col"], lp["b_shift"]),
            lambda q, k, v: _attn_b(q, k, v, bias_b, prep["far_b"], lp["lam"], lp["subln_col"]),
            qb, kb, vb)
        oc = lax.cond(
            lp["c_bounded"],
            lambda q, k, v: _attn_c_fast(q, k, v, lp["c_shift"]),
            _attn_c,
            qc, kc, vc)
        x1, h2, aff = _outproj(oa, ob, oc, lp["wo_a"], lp["wo_b"], lp["wo_c"], x2d, mod3,
                               lp["g_ffn"], lp["wr_hi"], lp["wr_lo"], s_len)
        idx3, gate3, pexc = _routing(aff.reshape(N_EXPERTS, n // LANES, LANES), cap)
        y = _moe(idx3, gate3, h2, lp["wg"], lp["wu"], lp["wd"], cap)
        tt = min(COMBINE_TT, s_len)
        starts = jnp.concatenate(
            [pexc[:, 0, ::tt // LANES], jnp.full((N_EXPERTS, 1), cap, jnp.int32)], axis=1).reshape(-1)
        x2d = _combine(starts, y, x1, mod3, s_len, cap)
    return x2d.reshape(bsz, s_len, d)


def kernel(x_prompt, x_sample, c_prompt, c_sample, w_mod, b_mod, g_attn, g_ffn, w_in, w_out,
           qk_gain_a, qk_gain_b, qk_gain_c, sink_a, lam_b, subln_b, rel_bias,
           w_router, w_gate, w_up, w_down):
    depth = w_mod.shape[0]
    bp, sp, d = x_prompt.shape
    bs, ss, _ = x_sample.shape
    rows = -(-(bp + bs) // 16) * 16
    c_all = jnp.zeros((rows, d), F32).at[:bp].set(c_prompt).at[bp:bp + bs].set(c_sample)
    mod_all = _modulation(c_all, w_mod, b_mod)
    mod_p, mod_s = mod_all[:, :bp], mod_all[:, bp:bp + bs]

    tbs = sorted({min(T_B, sp), min(T_B, ss)})
    tabs = {t: _bias_tables(rel_bias, t) for t in tbs}
    prep = {"bias_a": tabs[tbs[0]][0], "far_b": tabs[tbs[0]][2],
            "bias_b": {t: tabs[t][1] for t in tbs}, "layers": []}
    for l in range(depth):
        lam_init = 0.8 - 0.6 * math.exp(-0.3 * l)
        lp = lam_b[l].astype(F32)
        lam = jnp.exp(jnp.sum(lp[0] * lp[1])) - jnp.exp(jnp.sum(lp[2] * lp[3])) + lam_init
        wo = w_out[l].astype(BF16)
        wr_t = w_router[l].T
        wr_hi = wr_t.astype(BF16)
        gmax = lambda g_: jnp.max(jnp.abs(g_))
        c_c = 1.02 * HEAD_DIM * gmax(qk_gain_c[l][0]) * gmax(qk_gain_c[l][1]) * (HEAD_DIM ** -0.5 * LOG2E)
        c_b = 1.02 * B_QK * gmax(qk_gain_b[l][0]) * gmax(qk_gain_b[l][1]) * (B_QK ** -0.5 * LOG2E)
        bias_vals = rel_bias[:, A_HEADS:] * LOG2E
        prep["layers"].append({
            "c_shift": c_c.reshape(1).astype(F32),
            "c_bounded": 2.0 * c_c <= SHIFT_LIMIT,
            "b_shift": (c_b + jnp.max(bias_vals, axis=0)).astype(F32),
            "b_bounded": 2.0 * c_b + 2.0 * jnp.max(jnp.abs(bias_vals)) <= SHIFT_LIMIT,
            "g_attn": g_attn[l].reshape(1, d), "g_ffn": g_ffn[l].reshape(1, d),
            "w_in_t": w_in[l].T.astype(BF16),
            "gcol": _gain_column(qk_gain_a[l], qk_gain_b[l], qk_gain_c[l]),
            "sink": (sink_a[l] * LOG2E).astype(F32),
            "lam": lam.reshape(1).astype(F32),
            "subln_col": (subln_b[l] * (1.0 - lam_init)).reshape(HEAD_DIM, 1).astype(F32),
            "wo_a": wo[0:A_Q], "wo_b": wo[A_Q:A_Q + B_Q], "wo_c": wo[A_Q + B_Q:],
            "wr_hi": wr_hi, "wr_lo": (wr_t - wr_hi.astype(F32)).astype(BF16),
            "wg": w_gate[l].astype(BF16), "wu": w_up[l].astype(BF16), "wd": w_down[l].astype(BF16),
        })
    y_p = _trunk(x_prompt, mod_p, prep, sp, bp)
    y_s = _trunk(x_sample, mod_s, prep, ss, bs)
    return (y_p, y_s)
```
